```python
import math
import jax, jax.numpy as jnp
from jax import lax
import numpy as np

D_MODEL = 1024
BATCH = 8
SEQ = 4096
DEPTH = 2

N_META = 16
DN_HEADS = 4
DN_DK = 128
DN_DV = 128
DN_CONV = 5
DN_CHUNK = 64
DIFF_HEADS = 4
DIFF_D = 64
DIFF_DV = 2 * DIFF_D
MLA_HEADS = 4
MLA_NOPE = 128
MLA_ROPE = 64
MLA_DV = 128
MLA_Q_RANK = 256
MLA_KV_RANK = 128
ROPE_THETA = 10000.0
T5_BUCKETS = 32
T5_MAX_DIST = 128
Q_BLOCK = 128
D_MIX = DN_HEADS * DN_DV + DIFF_HEADS * DIFF_DV + MLA_HEADS * MLA_DV
IN_SPLITS = (
    DN_HEADS * DN_DK,
    DN_HEADS * DN_DK,
    DN_HEADS * DN_DV,
    DN_HEADS * DN_DV,
    2 * DN_HEADS,
    2 * DN_HEADS,
    2 * DIFF_HEADS * DIFF_D,
    2 * DIFF_HEADS * DIFF_D,
    DIFF_HEADS * DIFF_DV,
    MLA_Q_RANK,
    MLA_KV_RANK,
    MLA_ROPE,
)
IN_COLS = 2 * DN_HEADS * DN_DK + 2 * DN_HEADS * DN_DV + 4 * DN_HEADS + 4 * DIFF_HEADS * DIFF_D + DIFF_HEADS * DIFF_DV + MLA_Q_RANK + MLA_KV_RANK + MLA_ROPE
D_FF = 2816
N_EXPERTS = 8
TOP_K = 2
D_FF_EXPERT = 3584
RMS_EPS = 1e-6

kernel_name = "hybrid_parallel_groups_encoder"


def rms_norm(x, w):
    xf = x.astype(jnp.float32)
    y = xf * lax.rsqrt(jnp.mean(xf * xf, axis=-1, keepdims=True) + RMS_EPS)
    return (y * w.astype(jnp.float32)).astype(x.dtype)


def l2_normalize(x):
    xf = x.astype(jnp.float32)
    return (xf * lax.rsqrt(jnp.sum(xf * xf, axis=-1, keepdims=True) + RMS_EPS)).astype(x.dtype)


def swiglu(h, w_gate, w_up, w_down):
    return (jax.nn.silu(h @ w_gate) * (h @ w_up)) @ w_down


def split_columns(proj):
    outs, start = [], 0
    for size in IN_SPLITS:
        outs.append(proj[..., start:start + size])
        start += size
    return outs


def over_query_blocks(attend, q, pos):
    out_meta = attend(q[..., :N_META, :], pos[:N_META])
    n_blk = (q.shape[-2] - N_META) // Q_BLOCK
    q_real = q[..., N_META:, :].reshape(q.shape[:-2] + (n_blk, Q_BLOCK, q.shape[-1]))
    q_real = jnp.moveaxis(q_real, -3, 0)
    pos_real = pos[N_META:].reshape(n_blk, Q_BLOCK)
    out = lax.map(lambda args: attend(args[0], args[1]), (q_real, pos_real))
    out = jnp.moveaxis(out, 0, -3)
    out = out.reshape(out.shape[:-3] + (n_blk * Q_BLOCK, out.shape[-1]))
    return jnp.concatenate([out_meta, out], axis=-2)


def t5_bucket(rel):
    half = T5_BUCKETS // 2
    exact = half // 2
    n = jnp.abs(rel)
    large = exact + (jnp.log(jnp.maximum(n, exact).astype(jnp.float32) / exact)
                     / math.log(T5_MAX_DIST / exact) * (half - exact)).astype(jnp.int32)
    large = jnp.minimum(large, half - 1)
    return jnp.where(rel > 0, half, 0) + jnp.where(n < exact, n, large)


def t5_bias(table, q_pos, k_pos):
    bucket = t5_bucket(k_pos[None, :] - q_pos[:, None])
    return jnp.transpose(table[bucket], (2, 0, 1)).astype(jnp.float32)


def rope(x, pos):
    half = MLA_ROPE // 2
    inv = 1.0 / (ROPE_THETA ** (jnp.arange(half, dtype=jnp.float32) / half))
    ang = pos.astype(jnp.float32)[:, None] * inv[None, :]
    cos, sin = jnp.cos(ang).astype(x.dtype), jnp.sin(ang).astype(x.dtype)
    x1, x2 = x[..., :half], x[..., half:]
    return jnp.concatenate([x1 * cos - x2 * sin, x1 * sin + x2 * cos], axis=-1)


def centred_depthwise_conv(x, w):
    k = w.shape[0]
    return lax.conv_general_dilated(
        x, w[:, None, :].astype(x.dtype), window_strides=(1,), padding=((k // 2, k // 2),),
        dimension_numbers=("NWC", "WIO", "NWC"), feature_group_count=x.shape[-1])


def meta_first_reverse(t):
    return jnp.concatenate([t[:, :N_META], jnp.flip(t[:, N_META:], axis=1)], axis=1)


def chunk_gated_delta_rule(q, k, v, beta, g):
    out_dtype = v.dtype
    q, k, v, beta, g = (t.astype(jnp.float32) for t in (q, k, v, beta, g))
    pad = DN_CHUNK - N_META
    padf = lambda t: jnp.pad(t, [(0, 0), (pad, 0)] + [(0, 0)] * (t.ndim - 2))
    q, k, v, beta, g = (padf(t) for t in (q, k, v, beta, g))
    b_, lp, h, dk = k.shape
    dv = v.shape[-1]
    n = lp // DN_CHUNK
    vec = lambda t: t.reshape(b_, n, DN_CHUNK, h, t.shape[-1]).transpose(1, 0, 3, 2, 4)
    sca = lambda t: t.reshape(b_, n, DN_CHUNK, h).transpose(1, 0, 3, 2)
    q, k, v, beta, g = vec(q), vec(k), vec(v), sca(beta), sca(g)
    gc = jnp.cumsum(g, axis=-1)
    tril = jnp.tril(jnp.ones((DN_CHUNK, DN_CHUNK), dtype=bool))
    strict = jnp.tril(jnp.ones((DN_CHUNK, DN_CHUNK), dtype=bool), -1)
    diff = gc[..., :, None] - gc[..., None, :]
    decay = jnp.where(tril, jnp.exp(jnp.where(tril, diff, 0.0)), 0.0)
    k_beta = k * beta[..., None]
    v_beta = v * beta[..., None]
    a_mat = jnp.where(strict, jnp.einsum("nbhid,nbhjd->nbhij", k_beta, k) * decay, 0.0)
    lhs = a_mat + jnp.eye(DN_CHUNK, dtype=jnp.float32)
    value = lax.linalg.triangular_solve(lhs, v_beta, left_side=True, lower=True, unit_diagonal=True)
    k_cumdecay = lax.linalg.triangular_solve(lhs, k_beta * jnp.exp(gc)[..., None],
                                             left_side=True, lower=True, unit_diagonal=True)
    qk = jnp.einsum("nbhid,nbhjd->nbhij", q, k) * decay

    def step(state, xs):
        q_i, k_i, value_i, kcd_i, qk_i, gc_i = xs
        v_new = value_i - kcd_i @ state
        out = (q_i * jnp.exp(gc_i)[..., None]) @ state + qk_i @ v_new
        g_last = gc_i[..., -1]
        k_w = k_i * jnp.exp(g_last[..., None] - gc_i)[..., None]
        state = state * jnp.exp(g_last)[..., None, None] + jnp.einsum("bhcd,bhce->bhde", k_w, v_new)
        return state, out

    state0 = jnp.zeros((b_, h, dk, dv), jnp.float32)
    _, out = lax.scan(step, state0, (q, k, value, k_cumdecay, qk, gc))
    out = out.transpose(1, 0, 3, 2, 4).reshape(b_, lp, h, dv)[:, pad:]
    return out.astype(out_dtype)


def gated_deltanet(q, k, v, z, b, a, conv_w, a_log, dt_bias, norm_w):
    bsz, seqlen, _ = q.shape
    qkv = jax.nn.silu(centred_depthwise_conv(jnp.concatenate([q, k, v], axis=-1), conv_w))
    q = qkv[..., :DN_HEADS * DN_DK]
    k = qkv[..., DN_HEADS * DN_DK:2 * DN_HEADS * DN_DK]
    v = qkv[..., 2 * DN_HEADS * DN_DK:]
    q = l2_normalize(q.reshape(bsz, seqlen, DN_HEADS, DN_DK)) * (DN_DK ** -0.5)
    k = l2_normalize(k.reshape(bsz, seqlen, DN_HEADS, DN_DK))
    v = v.reshape(bsz, seqlen, DN_HEADS, DN_DV)
    beta = jax.nn.sigmoid(b.reshape(bsz, seqlen, 2, DN_HEADS).astype(jnp.float32))
    g = -jnp.exp(a_log.astype(jnp.float32)) * jax.nn.softplus(
        a.reshape(bsz, seqlen, 2, DN_HEADS).astype(jnp.float32) + dt_bias.astype(jnp.float32))
    o_fwd = chunk_gated_delta_rule(q, k, v, beta[:, :, 0], g[:, :, 0])
    rev = meta_first_reverse
    o_bwd = rev(chunk_gated_delta_rule(rev(q), rev(k), rev(v), rev(beta[:, :, 1]), rev(g[:, :, 1])))
    o = rms_norm(o_fwd + o_bwd, norm_w) * jax.nn.silu(z.reshape(bsz, seqlen, DN_HEADS, DN_DV))
    return o.reshape(bsz, seqlen, DN_HEADS * DN_DV)


def diff_attention(q, k, v, q_norm_w, k_norm_w, lam_params, subln_w, rel_bias_table, pos, layer):
    bsz, seqlen, _ = q.shape
    q = rms_norm(q.reshape(bsz, seqlen, DIFF_HEADS, 2, DIFF_D), q_norm_w).transpose(0, 2, 3, 1, 4)
    k = rms_norm(k.reshape(bsz, seqlen, DIFF_HEADS, 2, DIFF_D), k_norm_w).transpose(0, 2, 3, 1, 4)
    v = v.reshape(bsz, seqlen, DIFF_HEADS, DIFF_DV).transpose(0, 2, 1, 3)
    lam_init = 0.8 - 0.6 * math.exp(-0.3 * layer)
    lp = lam_params.astype(jnp.float32)
    lam = jnp.exp(jnp.sum(lp[0] * lp[1])) - jnp.exp(jnp.sum(lp[2] * lp[3])) + lam_init
    scale = DIFF_D ** -0.5

    def attend(q_blk, pos_blk):
        s = jnp.einsum("bhcqd,bhckd->bhcqk", q_blk, k).astype(jnp.float32) * scale
        s = s + t5_bias(rel_bias_table, pos_blk, pos)[None, :, None]
        p = jax.nn.softmax(s, axis=-1)
        attn = p[:, :, 0] - lam * p[:, :, 1]
        return jnp.einsum("bhqk,bhkd->bhqd", attn.astype(v.dtype), v)

    o = over_query_blocks(attend, q, pos)
    o = rms_norm(o, subln_w) * (1.0 - lam_init)
    return o.transpose(0, 2, 1, 3).reshape(bsz, seqlen, DIFF_HEADS * DIFF_DV)


def mla_attention(c_q, c_kv, k_rope, q_lat_norm_w, w_uq, kv_lat_norm_w, w_ukv, q_norm_w, k_norm_w, pos):
    bsz, seqlen, _ = c_q.shape
    q = (rms_norm(c_q, q_lat_norm_w) @ w_uq).reshape(bsz, seqlen, MLA_HEADS, MLA_NOPE + MLA_ROPE)
    kv = (rms_norm(c_kv, kv_lat_norm_w) @ w_ukv).reshape(bsz, seqlen, MLA_HEADS, MLA_NOPE + MLA_DV)
    k_nope, v = kv[..., :MLA_NOPE], kv[..., MLA_NOPE:]
    k = jnp.concatenate([k_nope, jnp.broadcast_to(k_rope[:, :, None, :], (bsz, seqlen, MLA_HEADS, MLA_ROPE))], axis=-1)
    q = rms_norm(q, q_norm_w).transpose(0, 2, 1, 3)
    k = rms_norm(k, k_norm_w).transpose(0, 2, 1, 3)
    q = jnp.concatenate([q[..., :MLA_NOPE], rope(q[..., MLA_NOPE:], pos)], axis=-1)
    k = jnp.concatenate([k[..., :MLA_NOPE], rope(k[..., MLA_NOPE:], pos)], axis=-1)
    v = v.transpose(0, 2, 1, 3)
    scale = (MLA_NOPE + MLA_ROPE) ** -0.5

    def attend(q_blk, pos_blk):
        s = jnp.einsum("bhqd,bhkd->bhqk", q_blk, k).astype(jnp.float32) * scale
        p = jax.nn.softmax(s, axis=-1)
        return jnp.einsum("bhqk,bhkd->bhqd", p.astype(v.dtype), v)

    o = over_query_blocks(attend, q, pos)
    return o.transpose(0, 2, 1, 3).reshape(bsz, seqlen, MLA_HEADS * MLA_DV)


def moe_swiglu(h, router_w, w_gate, w_up, w_down):
    logits = (h @ router_w).astype(jnp.float32)
    top_vals, top_idx = lax.top_k(logits, TOP_K)
    gates = jax.nn.softmax(top_vals, axis=-1)
    combine = jnp.sum(jax.nn.one_hot(top_idx, N_EXPERTS, dtype=jnp.float32) * gates[..., None], axis=-2)
    out = jnp.zeros_like(h)
    for e in range(N_EXPERTS):
        out = out + combine[..., e:e + 1].astype(h.dtype) * swiglu(h, w_gate[e], w_up[e], w_down[e])
    return out


def setup_inputs(seed: int = 0) -> dict:
    key = jax.random.key(seed)
    ks = iter(jax.random.split(key, 40))
    nrm = lambda shape, scale: jax.random.normal(next(ks), shape, jnp.float32) * scale
    gain = lambda shape: 1.0 + 0.05 * jax.random.normal(next(ks), shape, jnp.float32)
    n_dense = (DEPTH + 1) // 2
    n_moe = DEPTH // 2
    a_init = jax.random.uniform(next(ks), (DEPTH, 2, DN_HEADS), jnp.float32, minval=1.0, maxval=16.0)
    dt = jnp.exp(jax.random.uniform(next(ks), (DEPTH, 2, DN_HEADS), jnp.float32,
                                    minval=math.log(1e-3), maxval=math.log(1e-1)))
    return {
        "x": nrm((BATCH, SEQ, D_MODEL), 1.0),
        "meta_tokens": nrm((N_META, D_MODEL), 1.0),
        "rel_bias_table": nrm((T5_BUCKETS, DIFF_HEADS), 0.5),
        "attn_norm_w": gain((DEPTH, D_MODEL)),
        "w_in": nrm((DEPTH, D_MODEL, IN_COLS), D_MODEL ** -0.5),
        "dn_conv_w": nrm((DEPTH, DN_CONV, 2 * DN_HEADS * DN_DK + DN_HEADS * DN_DV), DN_CONV ** -0.5),
        "dn_a_log": jnp.log(a_init),
        "dn_dt_bias": dt + jnp.log(-jnp.expm1(-dt)),
        "dn_norm_w": gain((DEPTH, DN_DV)),
        "diff_q_norm_w": gain((DEPTH, DIFF_D)),
        "diff_k_norm_w": gain((DEPTH, DIFF_D)),
        "diff_lambda": nrm((DEPTH, 4, DIFF_D), 0.1),
        "diff_subln_w": gain((DEPTH, DIFF_DV)),
        "mla_q_lat_norm_w": gain((DEPTH, MLA_Q_RANK)),
        "mla_w_uq": nrm((DEPTH, MLA_Q_RANK, MLA_HEADS * (MLA_NOPE + MLA_ROPE)), MLA_Q_RANK ** -0.5),
        "mla_kv_lat_norm_w": gain((DEPTH, MLA_KV_RANK)),
        "mla_w_ukv": nrm((DEPTH, MLA_KV_RANK, MLA_HEADS * (MLA_NOPE + MLA_DV)), MLA_KV_RANK ** -0.5),
        "mla_q_norm_w": gain((DEPTH, MLA_NOPE + MLA_ROPE)),
        "mla_k_norm_w": gain((DEPTH, MLA_NOPE + MLA_ROPE)),
        "w_out": nrm((DEPTH, D_MIX, D_MODEL), D_MIX ** -0.5),
        "ffn_norm_w": gain((DEPTH, D_MODEL)),
        "ffn_w_gate": nrm((n_dense, D_MODEL, D_FF), D_MODEL ** -0.5),
        "ffn_w_up": nrm((n_dense, D_MODEL, D_FF), D_MODEL ** -0.5),
        "ffn_w_down": nrm((n_dense, D_FF, D_MODEL), D_FF ** -0.5),
        "router_w": nrm((n_moe, D_MODEL, N_EXPERTS), D_MODEL ** -0.5),
        "moe_w_gate": nrm((n_moe, N_EXPERTS, D_MODEL, D_FF_EXPERT), D_MODEL ** -0.5),
        "moe_w_up": nrm((n_moe, N_EXPERTS, D_MODEL, D_FF_EXPERT), D_MODEL ** -0.5),
        "moe_w_down": nrm((n_moe, N_EXPERTS, D_FF_EXPERT, D_MODEL), D_FF_EXPERT ** -0.5),
    }


def reference(x, meta_tokens, rel_bias_table, attn_norm_w, w_in, dn_conv_w, dn_a_log, dn_dt_bias,
              dn_norm_w, diff_q_norm_w, diff_k_norm_w, diff_lambda, diff_subln_w, mla_q_lat_norm_w,
              mla_w_uq, mla_kv_lat_norm_w, mla_w_ukv, mla_q_norm_w, mla_k_norm_w, w_out, ffn_norm_w,
              ffn_w_gate, ffn_w_up, ffn_w_down, router_w, moe_w_gate, moe_w_up, moe_w_down):
    bsz = x.shape[0]
    meta = jnp.broadcast_to(meta_tokens[None].astype(x.dtype), (bsz, N_META, D_MODEL))
    h = jnp.concatenate([meta, x], axis=1)
    pos = jnp.arange(h.shape[1], dtype=jnp.int32)
    for l in range(DEPTH):
        u = rms_norm(h, attn_norm_w[l])
        (dn_q, dn_k, dn_v, dn_z, dn_b, dn_a, df_q, df_k, df_v, ml_cq, ml_ckv, ml_kr) = split_columns(u @ w_in[l])
        y_dn = gated_deltanet(dn_q, dn_k, dn_v, dn_z, dn_b, dn_a, dn_conv_w[l], dn_a_log[l], dn_dt_bias[l], dn_norm_w[l])
        y_df = diff_attention(df_q, df_k, df_v, diff_q_norm_w[l], diff_k_norm_w[l], diff_lambda[l],
                              diff_subln_w[l], rel_bias_table, pos, l)
        y_ml = mla_attention(ml_cq, ml_ckv, ml_kr, mla_q_lat_norm_w[l], mla_w_uq[l], mla_kv_lat_norm_w[l],
                             mla_w_ukv[l], mla_q_norm_w[l], mla_k_norm_w[l], pos)
        h = h + jnp.concatenate([y_dn, y_df, y_ml], axis=-1) @ w_out[l]
        u = rms_norm(h, ffn_norm_w[l])
        if l % 2 == 0:
            h = h + swiglu(u, ffn_w_gate[l // 2], ffn_w_up[l // 2], ffn_w_down[l // 2])
        else:
            h = h + moe_swiglu(u, router_w[l // 2], moe_w_gate[l // 2], moe_w_up[l // 2], moe_w_down[l // 2])
    return h[:, N_META:]
```

```python
import functools
import math

import jax
import jax.numpy as jnp
from jax import lax
from jax.experimental import pallas as pl
from jax.experimental.pallas import tpu as pltpu

F32 = jnp.float32
BF16 = jnp.bfloat16

N_META = 16
N_PAD = 112
ROW0 = N_PAD + N_META
DN_HEADS = 4
DN_D = 128
DN_CONV = 5
DN_CHUNK = 64
DIFF_HEADS = 4
DIFF_D = 64
MLA_HEADS = 4
MLA_NOPE = 128
MLA_ROPE = 64
MLA_Q_RANK = 256
MLA_KV_RANK = 128
ROPE_THETA = 10000.0
T5_BUCKETS = 32
T5_MAX_DIST = 128
N_EXPERTS = 8
EPS = 1e-6
NEG_BIG = -1e30
LANES = 128
VMEM_LIMIT = 56 * 1024 * 1024

MAIN_COLS = 4096
GATE_COLS = 128


def _dot(a, b):
    return jnp.dot(a, b, preferred_element_type=F32)


def _dot_nt(a, b):
    return lax.dot_general(a, b, (((1,), (1,)), ((), ())), preferred_element_type=F32)


def _ones_bf16(n):
    return jnp.ones((n, n), BF16)


def _row_tile(lp):
    return 384 if lp % 384 == 0 else 128


def _cparams(sem):
    return pltpu.CompilerParams(dimension_semantics=sem, vmem_limit_bytes=VMEM_LIMIT)


def _inproj_kernel(h_ref, nw_ref, w_ref, main_ref, gate_ref):
    x = h_ref[0]
    ms = jnp.mean(x * x, axis=-1, keepdims=True)
    u = (x * lax.rsqrt(ms + EPS) * nw_ref[...]).astype(BF16)
    for c0 in range(0, MAIN_COLS, 512):
        main_ref[0, :, c0:c0 + 512] = _dot(u, w_ref[:, c0:c0 + 512]).astype(BF16)
    gate_ref[0] = _dot(u, w_ref[:, MAIN_COLS:MAIN_COLS + GATE_COLS])


def _inproj(h, norm_w, w):
    b, lp, d = h.shape
    tm = _row_tile(lp)
    ncol = MAIN_COLS + GATE_COLS
    return pl.pallas_call(
        _inproj_kernel,
        grid=(b, lp // tm),
        in_specs=[
            pl.BlockSpec((1, tm, d), lambda i, j: (i, j, 0)),
            pl.BlockSpec((1, d), lambda i, j: (0, 0)),
            pl.BlockSpec((d, ncol), lambda i, j: (0, 0)),
        ],
        out_specs=[
            pl.BlockSpec((1, tm, MAIN_COLS), lambda i, j: (i, j, 0)),
            pl.BlockSpec((1, tm, GATE_COLS), lambda i, j: (i, j, 0)),
        ],
        out_shape=[
            jax.ShapeDtypeStruct((b, lp, MAIN_COLS), BF16),
            jax.ShapeDtypeStruct((b, lp, GATE_COLS), F32),
        ],
        compiler_params=_cparams(("parallel", "parallel")),
        name="inproj",
    )(h, norm_w, w)


def _dnprep_kernel(x_ref, cw_ref, o_ref, xs_ref):
    j = pl.program_id(1)
    lp = x_ref.shape[1]
    halo = 8
    xs_ref[0:halo + N_PAD, :] = jnp.zeros((halo + N_PAD, LANES), F32)
    xs_ref[halo + N_PAD:halo + lp, :] = x_ref[0, N_PAD:, :].astype(F32)
    xs_ref[halo + lp:2 * halo + lp, :] = jnp.zeros((halo, LANES), F32)
    q_scale = jnp.where(j < DN_HEADS, DN_D ** -0.5, 1.0).astype(F32)
    is_qk = j < 2 * DN_HEADS
    rc = 128
    for c in range(lp // rc):
        base = c * rc
        acc = cw_ref[0:1, :] * xs_ref[base + halo - 2:base + halo - 2 + rc, :]
        for t in range(1, DN_CONV):
            off = base + halo - 2 + t
            acc = acc + cw_ref[t:t + 1, :] * xs_ref[off:off + rc, :]
        y = acc * jax.nn.sigmoid(acc)
        ss = jnp.sum(y * y, axis=-1, keepdims=True)
        r = jnp.where(is_qk, lax.rsqrt(ss + EPS) * q_scale, 1.0)
        o_ref[0, base:base + rc, :] = (y * r).astype(BF16)


def _dnprep(main, conv_w):
    b, lp, _ = main.shape
    ncb = 3 * DN_HEADS
    return pl.pallas_call(
        _dnprep_kernel,
        grid=(b, ncb),
        in_specs=[
            pl.BlockSpec((1, lp, LANES), lambda i, j: (i, 0, j)),
            pl.BlockSpec((DN_CONV, LANES), lambda i, j: (0, j)),
        ],
        out_specs=pl.BlockSpec((1, lp, LANES), lambda i, j: (i, 0, j)),
        out_shape=jax.ShapeDtypeStruct((b, lp, ncb * LANES), BF16),
        scratch_shapes=[pltpu.VMEM((lp + 16, LANES), F32)],
        compiler_params=_cparams(("parallel", "parallel")),
        name="dnprep",
    )(main, conv_w)


def _unit_tri_inverse(a, eye, blk16, blk32):
    a0 = jnp.where(blk16, a, 0.0)
    a0b = a0.astype(BF16)
    a2 = _dot(a0b, a0b)
    a2b = a2.astype(BF16)
    a4 = _dot(a2b, a2b)
    a4b = a4.astype(BF16)
    a8 = _dot(a4b, a4b)
    p = eye - a0
    p = p + _dot(p.astype(BF16), a2b)
    p = p + _dot(p.astype(BF16), a4b)
    p = p + _dot(p.astype(BF16), a8.astype(BF16))
    off1 = jnp.where(blk32 & (~blk16), a, 0.0).astype(BF16)
    pb = p.astype(BF16)
    p = p - _dot(_dot(pb, off1).astype(BF16), pb)
    off2 = jnp.where(blk32, 0.0, a).astype(BF16)
    pb = p.astype(BF16)
    return p - _dot(_dot(pb, off2).astype(BF16), pb)


def _dn_kernel(qf_ref, kf_ref, vf_ref, qb_ref, kb_ref, vb_ref, gcf_ref, gcb_ref,
               grf_ref, grb_ref, pc_ref, pr_ref, of_ref, ob_ref, st_ref):
    s = pl.program_id(1)
    c = DN_CHUNK

    @pl.when(s == 0)
    def _():
        st_ref[...] = jnp.zeros(st_ref.shape, F32)

    row = lax.broadcasted_iota(jnp.int32, (c, c), 0)
    col = lax.broadcasted_iota(jnp.int32, (c, c), 1)
    eye = (row == col).astype(F32)
    blk16 = (row // 16) == (col // 16)
    blk32 = (row // 32) == (col // 32)
    negA_c, dtb_c = pc_ref[0:1, :], pc_ref[1:2, :]
    negA_r, dtb_r = pr_ref[:, 0:c], pr_ref[:, c:2 * c]

    dirs = (
        (qf_ref, kf_ref, vf_ref, gcf_ref, grf_ref, of_ref, s, None),
        (qb_ref, kb_ref, vb_ref, gcb_ref, grb_ref, ob_ref, None, s),
    )
    for d, (q_ref, k_ref, v_ref, gc_ref, gr_ref, o_ref, sf, sb) in enumerate(dirs):
        if d == 0:
            causal = s >= 0
            chunk = s
        else:
            causal = s <= 1
            chunk = jnp.where(s <= 1, s, (pl.num_programs(1) + 1) - s)
        sgn = jnp.where(causal, 1, -1)
        m_in = (row - col) * sgn >= 0
        m_strict = (row - col) * sgn > 0
        m_f = m_in.astype(F32)
        m_t = ((col - row) * sgn >= 0).astype(F32)
        pos_c = chunk * c + lax.broadcasted_iota(jnp.int32, (c, LANES), 0)
        live_c = pos_c >= N_PAD
        gcol = gc_ref[0]
        beta_c = jnp.where(live_c, jax.nn.sigmoid(gcol), 0.0)
        g_c = jnp.where(live_c, negA_c * jax.nn.softplus(gcol + dtb_c), 0.0)
        pos_r = chunk * c + lax.broadcasted_iota(jnp.int32, (16, c), 1)
        grow = gr_ref[0, 0]
        g_r = jnp.where(pos_r >= N_PAD, negA_r * jax.nn.softplus(grow + dtb_r), 0.0)
        gc_c = jnp.dot(m_f, g_c, preferred_element_type=F32, precision=lax.Precision.HIGHEST)
        gc_r = jnp.dot(g_r, m_t, preferred_element_type=F32, precision=lax.Precision.HIGHEST)
        gtot_c = jnp.sum(g_c, axis=0, keepdims=True)
        eg_c = jnp.exp(gc_c)
        ew_c = jnp.exp(gtot_c - gc_c)
        et_c = jnp.exp(gtot_c)
        for h in range(DN_HEADS):
            bc = d * DN_HEADS + h
            ac = 2 * DN_HEADS + d * DN_HEADS + h
            q = q_ref[0, :, h * DN_D:(h + 1) * DN_D]
            k = k_ref[0, :, h * DN_D:(h + 1) * DN_D]
            v = v_ref[0, :, h * DN_D:(h + 1) * DN_D]
            kf = k.astype(F32)
            beta = beta_c[:, bc:bc + 1]
            kb = kf * beta
            vb = v.astype(F32) * beta
            diff = gc_c[:, ac:ac + 1] - gc_r[ac:ac + 1, :]
            decay = jnp.where(m_in, jnp.exp(jnp.where(m_in, diff, 0.0)), 0.0)
            kq = _dot_nt(jnp.concatenate([kb.astype(BF16), q], axis=0), k)
            a = jnp.where(m_strict, kq[0:c] * decay, 0.0)
            qk = kq[c:2 * c] * decay
            t = _unit_tri_inverse(a, eye, blk16, blk32)
            eg = eg_c[:, ac:ac + 1]
            x = jnp.concatenate([vb, kb * eg], axis=1).astype(BF16)
            tx = _dot(t.astype(BF16), x)
            value, kcd = tx[:, 0:DN_D], tx[:, DN_D:2 * DN_D]
            st = st_ref[bc]
            lhs = jnp.concatenate([kcd, q.astype(F32) * eg], axis=0).astype(BF16)
            r = _dot(lhs, st.astype(BF16))
            v_new = value - r[0:c]
            v_new_b = v_new.astype(BF16)
            out = r[c:2 * c] + _dot(qk.astype(BF16), v_new_b)
            o_ref[0, :, h * DN_D:(h + 1) * DN_D] = out.astype(BF16)
            kw = kf * ew_c[:, ac:ac + 1]
            st_ref[bc] = st * et_c[:, ac:ac + 1] + _dot(kw.T.astype(BF16), v_new_b)


def _deltanet(qkv, gate_col, gate_row, pc, pr):
    b, lp, _ = qkv.shape
    n = lp // DN_CHUNK
    hw = DN_HEADS * DN_D

    def fwd_chunk(s):
        return s

    def bwd_chunk(s):
        return jnp.where(s <= 1, s, n + 1 - s)

    def spec3(cb, chunk):
        return pl.BlockSpec((1, DN_CHUNK, hw), lambda i, s: (i, chunk(s), cb))

    in_specs = [spec3(0, fwd_chunk), spec3(1, fwd_chunk), spec3(2, fwd_chunk),
                spec3(0, bwd_chunk), spec3(1, bwd_chunk), spec3(2, bwd_chunk),
                pl.BlockSpec((1, DN_CHUNK, GATE_COLS), lambda i, s: (i, fwd_chunk(s), 0)),
                pl.BlockSpec((1, DN_CHUNK, GATE_COLS), lambda i, s: (i, bwd_chunk(s), 0)),
                pl.BlockSpec((1, 1, 16, DN_CHUNK), lambda i, s: (i, fwd_chunk(s), 0, 0)),
                pl.BlockSpec((1, 1, 16, DN_CHUNK), lambda i, s: (i, bwd_chunk(s), 0, 0)),
                pl.BlockSpec((2, GATE_COLS), lambda i, s: (0, 0)),
                pl.BlockSpec((16, 2 * DN_CHUNK), lambda i, s: (0, 0))]
    out_specs = [pl.BlockSpec((1, DN_CHUNK, hw), lambda i, s: (i, fwd_chunk(s), 0)),
                 pl.BlockSpec((1, DN_CHUNK, hw), lambda i, s: (i, bwd_chunk(s), 0))]
    return pl.pallas_call(
        _dn_kernel,
        grid=(b, n),
        in_specs=in_specs,
        out_specs=out_specs,
        out_shape=[jax.ShapeDtypeStruct((b, lp, hw), BF16)] * 2,
        scratch_shapes=[pltpu.VMEM((2 * DN_HEADS, DN_D, DN_D), F32)],
        compiler_params=_cparams(("parallel", "arbitrary")),
        name="deltanet",
    )(qkv, qkv, qkv, qkv, qkv, qkv, gate_col, gate_col, gate_row, gate_row, pc, pr)


def _attnprep_kernel(x_ref, dqw_ref, dkw_ref, qlw_ref, wuq_ref, klw_ref, wukv_ref,
                     mqw_ref, mkw_ref, cos_ref, sin_ref,
                     dq_ref, dk_ref, mq_ref, mk_ref, mv_ref):
    ones = _ones_bf16(LANES)
    row = lax.broadcasted_iota(jnp.int32, (LANES, LANES), 0)
    col = lax.broadcasted_iota(jnp.int32, (LANES, LANES), 1)
    half_ones = ((row // DIFF_D) == (col // DIFF_D)).astype(BF16)
    cos_t, sin_t = cos_ref[...], sin_ref[...]

    def rope(t):
        return t * cos_t + pltpu.roll(t, 64, 1) * sin_t

    for src, w_ref, dst in ((0, dqw_ref, dq_ref), (512, dkw_ref, dk_ref)):
        for h in range(DIFF_HEADS):
            y = x_ref[0, :, src + h * LANES:src + (h + 1) * LANES].astype(F32)
            ms = _dot((y * y).astype(BF16), half_ones) * (1.0 / DIFF_D)
            dst[0, :, h * LANES:(h + 1) * LANES] = (y * lax.rsqrt(ms + EPS) * w_ref[...]).astype(BF16)

    cq0 = x_ref[0, :, 1536:1664].astype(F32)
    cq1 = x_ref[0, :, 1664:1792].astype(F32)
    ms = _dot((cq0 * cq0 + cq1 * cq1).astype(BF16), ones) * (1.0 / MLA_Q_RANK)
    r = lax.rsqrt(ms + EPS)
    cqn = jnp.concatenate([cq0 * r * qlw_ref[:, 0:LANES], cq1 * r * qlw_ref[:, LANES:2 * LANES]],
                          axis=1).astype(BF16)
    q = _dot(cqn, wuq_ref[...])
    inv_d = 1.0 / (MLA_NOPE + MLA_ROPE)
    for h in range(MLA_HEADS):
        q0 = q[:, 256 * h:256 * h + LANES]
        q1 = q[:, 256 * h + LANES:256 * (h + 1)]
        ms = _dot((q0 * q0 + q1 * q1).astype(BF16), ones) * inv_d
        r = lax.rsqrt(ms + EPS)
        mq_ref[0, :, 256 * h:256 * h + LANES] = (q0 * r * mqw_ref[:, 0:LANES]).astype(BF16)
        mq_ref[0, :, 256 * h + LANES:256 * (h + 1)] = rope(q1 * r * mqw_ref[:, LANES:2 * LANES]).astype(BF16)

    ckv = x_ref[0, :, 1792:1920].astype(F32)
    ms = _dot((ckv * ckv).astype(BF16), ones) * (1.0 / MLA_KV_RANK)
    ckvn = (ckv * lax.rsqrt(ms + EPS) * klw_ref[...]).astype(BF16)
    kv = _dot(ckvn, wukv_ref[...])
    kr = x_ref[0, :, 1920:2048].astype(F32)
    kr2 = kr * kr
    for h in range(MLA_HEADS):
        k0 = kv[:, LANES * h:LANES * (h + 1)]
        ms = _dot((k0 * k0 + kr2).astype(BF16), ones) * inv_d
        r = lax.rsqrt(ms + EPS)
        mk_ref[0, :, 256 * h:256 * h + LANES] = (k0 * r * mkw_ref[:, 0:LANES]).astype(BF16)
        mk_ref[0, :, 256 * h + LANES:256 * (h + 1)] = rope(kr * r * mkw_ref[:, LANES:2 * LANES]).astype(BF16)
    mv_ref[0] = kv[:, 512:1024].astype(BF16)


def _attnprep(main, dqw, dkw, qlw, wuq, klw, wukv, mqw, mkw, cos_t, sin_t):
    b, lp, _ = main.shape
    tm = _row_tile(lp)

    def full(a):
        return pl.BlockSpec(a.shape, lambda i, j: (0,) * a.ndim)

    def rows(width):
        return pl.BlockSpec((1, tm, width), lambda i, j: (i, j, 0))

    return pl.pallas_call(
        _attnprep_kernel,
        grid=(b, lp // tm),
        in_specs=[pl.BlockSpec((1, tm, 2048), lambda i, j: (i, j, 1)),
                  full(dqw), full(dkw), full(qlw), full(wuq), full(klw), full(wukv),
                  full(mqw), full(mkw),
                  pl.BlockSpec((tm, LANES), lambda i, j: (j, 0)),
                  pl.BlockSpec((tm, LANES), lambda i, j: (j, 0))],
        out_specs=[rows(512), rows(512), rows(1024), rows(1024), rows(512)],
        out_shape=[jax.ShapeDtypeStruct((b, lp, 512), BF16),
                   jax.ShapeDtypeStruct((b, lp, 512), BF16),
                   jax.ShapeDtypeStruct((b, lp, 1024), BF16),
                   jax.ShapeDtypeStruct((b, lp, 1024), BF16),
                   jax.ShapeDtypeStruct((b, lp, 512), BF16)],
        compiler_params=_cparams(("parallel", "parallel")),
        name="attnprep",
    )(main, dqw, dkw, qlw, wuq, klw, wukv, mqw, mkw, cos_t, sin_t)


def _diffattn_kernel(q_ref, k_ref, v_ref, band_ref, far_ref, lam_ref, sw_ref, o_ref):
    qi = pl.program_id(2)
    tq = q_ref.shape[1]
    lp = k_ref.shape[1]
    lane = lax.broadcasted_iota(jnp.int32, (tq, LANES), 1)
    q = q_ref[0]
    qz = jnp.concatenate([jnp.where(lane < DIFF_D, q, jnp.zeros_like(q)),
                          jnp.where(lane >= DIFF_D, q, jnp.zeros_like(q))], axis=0)
    s = _dot_nt(qz, k_ref[0])
    c_neg = far_ref[0, 0:1, :]
    c_pos = far_ref[0, 1:2, :]
    blocks = []
    for kb in range(lp // LANES):
        d = kb - qi
        near = band_ref[0, jnp.clip(d + 1, 0, 2)]
        bias = jnp.where(d <= -2, c_neg, jnp.where(d >= 2, c_pos, near))
        if kb == 0:
            bias = jnp.where(lane < N_PAD, NEG_BIG, bias)
        blocks.append(s[:, kb * LANES:(kb + 1) * LANES] + jnp.concatenate([bias, bias], axis=0))
    s = jnp.concatenate(blocks, axis=1)
    m = jnp.max(s, axis=-1, keepdims=True)
    p = jnp.exp(s - m)
    l = jnp.sum(p, axis=-1, keepdims=True)
    o2 = _dot(p.astype(BF16), v_ref[0])
    lam = lam_ref[0:1, 0:1]
    o = o2[0:tq] / l[0:tq] - lam * (o2[tq:2 * tq] / l[tq:2 * tq])
    ms = jnp.mean(o * o, axis=-1, keepdims=True)
    o_ref[0] = (o * lax.rsqrt(ms + EPS) * sw_ref[...]).astype(BF16)


def _diffattn(dq, dk, main, band, far, lam, sw):
    b, lp, _ = dq.shape
    tq = 128
    return pl.pallas_call(
        _diffattn_kernel,
        grid=(b, DIFF_HEADS, lp // tq),
        in_specs=[pl.BlockSpec((1, tq, LANES), lambda i, h, j: (i, j, h)),
                  pl.BlockSpec((1, lp, LANES), lambda i, h, j: (i, 0, h)),
                  pl.BlockSpec((1, lp, LANES), lambda i, h, j: (i, 0, 24 + h)),
                  pl.BlockSpec((1, 3, tq, LANES), lambda i, h, j: (h, 0, 0, 0)),
                  pl.BlockSpec((1, 2, LANES), lambda i, h, j: (h, 0, 0)),
                  pl.BlockSpec((1, LANES), lambda i, h, j: (0, 0)),
                  pl.BlockSpec((1, LANES), lambda i, h, j: (0, 0))],
        out_specs=pl.BlockSpec((1, tq, LANES), lambda i, h, j: (i, j, h)),
        out_shape=jax.ShapeDtypeStruct((b, lp, DIFF_HEADS * LANES), BF16),
        compiler_params=_cparams(("parallel", "parallel", "arbitrary")),
        name="diffattn",
    )(dq, dk, main, band, far, lam, sw)


def _mlaattn_kernel(q_ref, k_ref, v_ref, o_ref):
    tq = q_ref.shape[1]
    s = _dot_nt(q_ref[0], k_ref[0])
    lane = lax.broadcasted_iota(jnp.int32, (tq, LANES), 1)
    s0 = jnp.where(lane < N_PAD, NEG_BIG, s[:, 0:LANES])
    s = jnp.concatenate([s0, s[:, LANES:]], axis=1)
    m = jnp.max(s, axis=-1, keepdims=True)
    p = jnp.exp(s - m)
    l = jnp.sum(p, axis=-1, keepdims=True)
    o_ref[0] = (_dot(p.astype(BF16), v_ref[0]) / l).astype(BF16)


def _mlaattn(mq, mk, mv):
    b, lp, _ = mq.shape
    tq = 128
    return pl.pallas_call(
        _mlaattn_kernel,
        grid=(b, MLA_HEADS, lp // tq),
        in_specs=[pl.BlockSpec((1, tq, 256), lambda i, h, j: (i, j, h)),
                  pl.BlockSpec((1, lp, 256), lambda i, h, j: (i, 0, h)),
                  pl.BlockSpec((1, lp, LANES), lambda i, h, j: (i, 0, h))],
        out_specs=pl.BlockSpec((1, tq, LANES), lambda i, h, j: (i, j, h)),
        out_shape=jax.ShapeDtypeStruct((b, lp, MLA_HEADS * LANES), BF16),
        compiler_params=_cparams(("parallel", "parallel", "arbitrary")),
        name="mlaattn",
    )(mq, mk, mv)


def _outproj_kernel(of_ref, ob_ref, z_ref, df_ref, ml_ref, h_ref, dnw_ref, w_ref, o_ref):
    ones = _ones_bf16(LANES)
    parts = []
    for h in range(DN_HEADS):
        sl = slice(h * DN_D, (h + 1) * DN_D)
        o = of_ref[0, :, sl].astype(F32) + ob_ref[0, :, sl].astype(F32)
        ms = _dot((o * o).astype(BF16), ones) * (1.0 / DN_D)
        z = z_ref[0, :, sl].astype(F32)
        parts.append((o * lax.rsqrt(ms + EPS) * dnw_ref[...] * (z * jax.nn.sigmoid(z))).astype(BF16))
    ydn = jnp.concatenate(parts, axis=1)
    acc = h_ref[0] + _dot(ydn, w_ref[0:512, :])
    acc = acc + _dot(df_ref[0], w_ref[512:1024, :])
    acc = acc + _dot(ml_ref[0], w_ref[1024:1536, :])
    o_ref[0] = acc


def _outproj(o_f, o_b, main, ydf, yml, h, dnw, w):
    b, lp, d = h.shape
    tm = _row_tile(lp)

    def rows(width, cb=0):
        return pl.BlockSpec((1, tm, width), lambda i, j: (i, j, cb))

    return pl.pallas_call(
        _outproj_kernel,
        grid=(b, lp // tm),
        in_specs=[rows(512), rows(512), rows(512, 3), rows(512), rows(512), rows(d),
                  pl.BlockSpec((1, LANES), lambda i, j: (0, 0)),
                  pl.BlockSpec(w.shape, lambda i, j: (0, 0))],
        out_specs=rows(d),
        out_shape=jax.ShapeDtypeStruct((b, lp, d), F32),
        compiler_params=_cparams(("parallel", "parallel")),
        name="outproj",
    )(o_f, o_b, main, ydf, yml, h, dnw, w)


def _ffn_kernel(h_ref, nw_ref, wg_ref, wu_ref, wd_ref, o_ref, a_ref):
    x = h_ref[...]
    ms = jnp.mean(x * x, axis=-1, keepdims=True)
    u = (x * lax.rsqrt(ms + EPS) * nw_ref[...]).astype(BF16)
    f = wg_ref.shape[1]
    fc = 256
    for c0 in range(0, f, fc):
        g = _dot(u, wg_ref[:, c0:c0 + fc])
        up = _dot(u, wu_ref[:, c0:c0 + fc])
        a_ref[:, c0:c0 + fc] = (g * jax.nn.sigmoid(g) * up).astype(BF16)
    o_ref[...] = x + _dot(a_ref[...], wd_ref[...])


def _ffn(h2, nw, wg, wu, wd):
    n, d = h2.shape
    f = wg.shape[1]
    tm = 512 if n % 512 == 0 else 128

    def const(a):
        return pl.BlockSpec(a.shape, lambda i: (0, 0), pipeline_mode=pl.Buffered(1))

    return pl.pallas_call(
        _ffn_kernel,
        grid=(n // tm,),
        in_specs=[pl.BlockSpec((tm, d), lambda i: (i, 0)),
                  pl.BlockSpec((1, d), lambda i: (0, 0)),
                  const(wg), const(wu), const(wd)],
        out_specs=pl.BlockSpec((tm, d), lambda i: (i, 0)),
        out_shape=jax.ShapeDtypeStruct((n, d), F32),
        scratch_shapes=[pltpu.VMEM((tm, f), BF16)],
        compiler_params=_cparams(("parallel",)),
        name="ffn",
    )(h2, nw, wg, wu, wd)


def _router_kernel(h_ref, nw_ref, rw_ref, u_ref, idx_ref, gate_ref):
    x = h_ref[...]
    ms = jnp.mean(x * x, axis=-1, keepdims=True)
    u = x * lax.rsqrt(ms + EPS) * nw_ref[...]
    u_ref[...] = u.astype(BF16)
    logits = jnp.dot(u, rw_ref[...], preferred_element_type=F32, precision=lax.Precision.HIGHEST)
    lane = lax.broadcasted_iota(jnp.int32, logits.shape, 1)
    logits = jnp.where(lane < N_EXPERTS, logits, -jnp.inf)
    m1 = jnp.max(logits, axis=-1, keepdims=True)
    i1 = jnp.min(jnp.where(logits == m1, lane, LANES), axis=-1, keepdims=True)
    rest = jnp.where(lane == i1, -jnp.inf, logits)
    m2 = jnp.max(rest, axis=-1, keepdims=True)
    i2 = jnp.min(jnp.where(rest == m2, lane, LANES), axis=-1, keepdims=True)
    e2 = jnp.exp(m2 - m1)
    g1 = 1.0 / (1.0 + e2)
    g2 = e2 / (1.0 + e2)
    idx_ref[...] = jnp.where(lane == 0, i1, jnp.where(lane == 1, i2, 0))
    gate_ref[...] = jnp.where(lane == 0, g1, jnp.where(lane == 1, g2, 0.0))


def _router(h2, nw, rw):
    n, d = h2.shape
    tm = 512 if n % 512 == 0 else 128
    return pl.pallas_call(
        _router_kernel,
        grid=(n // tm,),
        in_specs=[pl.BlockSpec((tm, d), lambda i: (i, 0)),
                  pl.BlockSpec((1, d), lambda i: (0, 0)),
                  pl.BlockSpec((d, LANES), lambda i: (0, 0))],
        out_specs=[pl.BlockSpec((tm, d), lambda i: (i, 0)),
                   pl.BlockSpec((tm, LANES), lambda i: (i, 0)),
                   pl.BlockSpec((tm, LANES), lambda i: (i, 0))],
        out_shape=[jax.ShapeDtypeStruct((n, d), BF16),
                   jax.ShapeDtypeStruct((n, LANES), jnp.int32),
                   jax.ShapeDtypeStruct((n, LANES), F32)],
        compiler_params=_cparams(("parallel",)),
        name="router",
    )(h2, nw, rw)


def _moe_kernel(te_ref, tv_ref, x_ref, g_ref, wg_ref, wu_ref, wd_ref, o_ref):
    i = pl.program_id(0)
    f = pl.program_id(1)

    @pl.when(f == 0)
    def _():
        o_ref[...] = jnp.zeros(o_ref.shape, F32)

    @pl.when(tv_ref[i] > 0)
    def _():
        x = x_ref[...]
        g = _dot(x, wg_ref[0])
        up = _dot(x, wu_ref[0])
        a = (g * jax.nn.sigmoid(g) * up * g_ref[...]).astype(BF16)
        o_ref[...] += _dot(a, wd_ref[0])


def _moe(x_sorted, gate_sorted, tile_expert, tile_valid, wg, wu, wd, tm):
    p, d = x_sorted.shape
    f = wg.shape[2]
    nf = 2
    fh = f // nf
    grid_spec = pltpu.PrefetchScalarGridSpec(
        num_scalar_prefetch=2,
        grid=(p // tm, nf),
        in_specs=[pl.BlockSpec((tm, d), lambda i, j, te, tv: (i, 0)),
                  pl.BlockSpec((tm, 1), lambda i, j, te, tv: (i, 0)),
                  pl.BlockSpec((1, d, fh), lambda i, j, te, tv: (te[i], 0, j)),
                  pl.BlockSpec((1, d, fh), lambda i, j, te, tv: (te[i], 0, j)),
                  pl.BlockSpec((1, fh, d), lambda i, j, te, tv: (te[i], j, 0))],
        out_specs=pl.BlockSpec((tm, d), lambda i, j, te, tv: (i, 0)),
    )
    return pl.pallas_call(
        _moe_kernel,
        grid_spec=grid_spec,
        out_shape=jax.ShapeDtypeStruct((p, d), F32),
        compiler_params=_cparams(("arbitrary", "arbitrary")),
        name="moe",
    )(tile_expert, tile_valid, x_sorted, gate_sorted, wg, wu, wd)


def _spread_rope_cols(w):
    z = jnp.zeros(w.shape[:-1] + (32,), w.dtype)
    return jnp.concatenate([w[..., 0:32], z, w[..., 32:64], z], axis=-1)


def _relayout_w_in(w):
    dnw = 4 * DN_HEADS * DN_D
    n_gate = 4 * DN_HEADS
    rest = w[:, dnw + n_gate:]
    body = rest[:, :rest.shape[1] - MLA_ROPE]
    kr = _spread_rope_cols(rest[:, rest.shape[1] - MLA_ROPE:])
    gates = jnp.concatenate([w[:, dnw:dnw + n_gate],
                             jnp.zeros((w.shape[0], GATE_COLS - n_gate), w.dtype)], axis=1)
    out = jnp.concatenate([w[:, :dnw], body, kr, gates], axis=1)
    assert out.shape[1] == MAIN_COLS + GATE_COLS
    return out.astype(BF16)


def _relayout_w_uq(w):
    per = MLA_NOPE + MLA_ROPE
    heads = [jnp.concatenate([w[:, h * per:h * per + MLA_NOPE],
                              _spread_rope_cols(w[:, h * per + MLA_NOPE:(h + 1) * per])], axis=1)
             for h in range(MLA_HEADS)]
    return jnp.concatenate(heads, axis=1).astype(BF16)


def _relayout_w_ukv(w):
    per = MLA_NOPE + LANES
    ks = [w[:, h * per:h * per + MLA_NOPE] for h in range(MLA_HEADS)]
    vs = [w[:, h * per + MLA_NOPE:(h + 1) * per] for h in range(MLA_HEADS)]
    return jnp.concatenate(ks + vs, axis=1).astype(BF16)


def _qk_norm_vec(w, scale):
    return (jnp.concatenate([w[:MLA_NOPE], _spread_rope_cols(w[MLA_NOPE:])]) * scale)[None, :]


def _t5_bucket(rel):
    half = T5_BUCKETS // 2
    exact = half // 2
    n = jnp.abs(rel)
    large = exact + (jnp.log(jnp.maximum(n, exact).astype(F32) / exact)
                     / math.log(T5_MAX_DIST / exact) * (half - exact)).astype(jnp.int32)
    large = jnp.minimum(large, half - 1)
    return jnp.where(rel > 0, half, 0) + jnp.where(n < exact, n, large)


def _t5_tables(table, tq):
    i = jnp.arange(tq)[:, None]
    j = jnp.arange(LANES)[None, :]
    rels = jnp.stack([j - i + (d * LANES) for d in (-1, 0, 1)])
    band = jnp.transpose(table[_t5_bucket(rels)], (3, 0, 1, 2)).astype(F32)
    far_rel = jnp.array([-T5_MAX_DIST, T5_MAX_DIST], jnp.int32)
    far = jnp.transpose(table[_t5_bucket(far_rel)], (1, 0)).astype(F32)
    far = jnp.broadcast_to(far[:, :, None], far.shape + (LANES,))
    return band, far


def _rope_tables(lp):
    half = MLA_ROPE // 2
    inv = 1.0 / (ROPE_THETA ** (jnp.arange(half, dtype=F32) / half))
    pos = (jnp.arange(lp, dtype=jnp.int32) - N_PAD).astype(F32)
    ang = pos[:, None] * inv[None, :]
    c, s, z = jnp.cos(ang), jnp.sin(ang), jnp.zeros((lp, half), F32)
    return jnp.concatenate([c, z, c, z], axis=1), jnp.concatenate([-s, z, s, z], axis=1)


def _moe_plan(idx, gates, n, tm):
    e_flat = idx.reshape(-1)
    onehot = (e_flat[:, None] == jnp.arange(N_EXPERTS)[None, :]).astype(jnp.int32)
    counts = jnp.sum(onehot, axis=0)
    rank = jnp.sum((jnp.cumsum(onehot, axis=0) - onehot) * onehot, axis=1)
    tiles_per = (counts + tm - 1) // tm
    tile_end = jnp.cumsum(tiles_per)
    start = (tile_end - tiles_per) * tm
    dest = start[e_flat] + rank
    n_tiles = (2 * n) // tm + N_EXPERTS
    p = n_tiles * tm
    slot_tok = jnp.zeros((p,), jnp.int32).at[dest].set(jnp.arange(2 * n, dtype=jnp.int32) // 2)
    slot_gate = jnp.zeros((p,), F32).at[dest].set(gates.reshape(-1))
    t = jnp.arange(n_tiles, dtype=jnp.int32)
    tile_expert = jnp.minimum(jnp.sum((t[:, None] >= tile_end[None, :]).astype(jnp.int32), axis=1),
                              N_EXPERTS - 1).astype(jnp.int32)
    tile_valid = (t < tile_end[-1]).astype(jnp.int32)
    return slot_tok, slot_gate, dest.reshape(n, 2), tile_expert, tile_valid


def kernel(x, meta_tokens, rel_bias_table, attn_norm_w, w_in, dn_conv_w, dn_a_log, dn_dt_bias, dn_norm_w, diff_q_norm_w, diff_k_norm_w, diff_lambda, diff_subln_w, mla_q_lat_norm_w, mla_w_uq, mla_kv_lat_norm_w, mla_w_ukv, mla_q_norm_w, mla_k_norm_w, w_out, ffn_norm_w, ffn_w_gate, ffn_w_up, ffn_w_down, router_w, moe_w_gate, moe_w_up, moe_w_down):
    bsz, seq, d = x.shape
    depth = w_in.shape[0]
    lp = ROW0 + seq
    n = bsz * lp
    assert lp % LANES == 0 and meta_tokens.shape[0] == N_META

    meta = jnp.broadcast_to(meta_tokens[None].astype(x.dtype), (bsz, N_META, d))
    h = jnp.concatenate([jnp.zeros((bsz, N_PAD, d), x.dtype), meta, x], axis=1)

    cos_t, sin_t = _rope_tables(lp)
    band, far = _t5_tables(rel_bias_table, 128)
    n_chunks = lp // DN_CHUNK

    for l in range(depth):
        main, gate = _inproj(h, attn_norm_w[l][None, :], _relayout_w_in(w_in[l]))

        qkv = _dnprep(main, dn_conv_w[l])
        gate_row = jnp.transpose(gate[:, :, 0:16].reshape(bsz, n_chunks, DN_CHUNK, 16), (0, 1, 3, 2))
        neg_a = -jnp.exp(dn_a_log[l].astype(F32)).reshape(-1)
        dtb = dn_dt_bias[l].astype(F32).reshape(-1)
        z8 = jnp.zeros((8,), F32)
        pc = jnp.stack([jnp.concatenate([z8, neg_a, jnp.zeros((GATE_COLS - 16,), F32)]),
                        jnp.concatenate([z8, dtb, jnp.zeros((GATE_COLS - 16,), F32)])])
        pr = jnp.concatenate([jnp.broadcast_to(jnp.concatenate([z8, neg_a])[:, None], (16, DN_CHUNK)),
                              jnp.broadcast_to(jnp.concatenate([z8, dtb])[:, None], (16, DN_CHUNK))], axis=1)
        o_f, o_b = _deltanet(qkv, gate, gate_row, pc, pr)

        dscale = DIFF_D ** -0.5
        mscale = (MLA_NOPE + MLA_ROPE) ** -0.5
        dq, dk, mq, mk, mv = _attnprep(
            main,
            (jnp.tile(diff_q_norm_w[l], 2) * dscale)[None, :],
            jnp.tile(diff_k_norm_w[l], 2)[None, :],
            mla_q_lat_norm_w[l][None, :], _relayout_w_uq(mla_w_uq[l]),
            mla_kv_lat_norm_w[l][None, :], _relayout_w_ukv(mla_w_ukv[l]),
            _qk_norm_vec(mla_q_norm_w[l], mscale), _qk_norm_vec(mla_k_norm_w[l], 1.0),
            cos_t, sin_t)
        lam_init = 0.8 - 0.6 * math.exp(-0.3 * l)
        lpar = diff_lambda[l].astype(F32)
        lam = jnp.exp(jnp.sum(lpar[0] * lpar[1])) - jnp.exp(jnp.sum(lpar[2] * lpar[3])) + lam_init
        ydf = _diffattn(dq, dk, main, band, far, jnp.full((1, LANES), lam, F32),
                        (diff_subln_w[l] * (1.0 - lam_init))[None, :])
        yml = _mlaattn(mq, mk, mv)

        h = _outproj(o_f, o_b, main, ydf, yml, h, dn_norm_w[l][None, :], w_out[l].astype(BF16))

        h2 = h.reshape(n, d)
        if l % 2 == 0:
            i = l // 2
            h2 = _ffn(h2, ffn_norm_w[l][None, :], ffn_w_gate[i].astype(BF16),
                      ffn_w_up[i].astype(BF16), ffn_w_down[i].astype(BF16))
        else:
            i = l // 2
            rw = jnp.concatenate([router_w[i], jnp.zeros((d, LANES - N_EXPERTS), F32)], axis=1)
            u, idx, gates = _router(h2, ffn_norm_w[l][None, :], rw)
            tm = 512 if (2 * n) % 512 == 0 else 128
            slot_tok, slot_gate, tok_slots, tile_expert, tile_valid = _moe_plan(idx[:, 0:2], gates[:, 0:2], n, tm)
            x_sorted = jnp.take(u, slot_tok, axis=0)
            y_sorted = _moe(x_sorted, slot_gate[:, None], tile_expert, tile_valid,
                            moe_w_gate[i].astype(BF16), moe_w_up[i].astype(BF16),
                            moe_w_down[i].astype(BF16), tm)
            h2 = h2 + jnp.take(y_sorted, tok_slots[:, 0], axis=0) + jnp.take(y_sorted, tok_slots[:, 1], axis=0)
        h = h2.reshape(bsz, lp, d)
    return h[:, ROW0:]
```

```python
import functools
import math

import jax
import jax.numpy as jnp
from jax import lax
from jax.experimental import pallas as pl
from jax.experimental.pallas import tpu as pltpu

F32 = jnp.float32
BF16 = jnp.bfloat16

N_META = 16
N_PAD = 112
ROW0 = N_PAD + N_META
DN_HEADS = 4
DN_D = 128
DN_CONV = 5
DN_CHUNK = 64
DIFF_HEADS = 4
DIFF_D = 64
MLA_HEADS = 4
MLA_NOPE = 128
MLA_ROPE = 64
MLA_Q_RANK = 256
MLA_KV_RANK = 128
ROPE_THETA = 10000.0
T5_BUCKETS = 32
T5_MAX_DIST = 128
N_EXPERTS = 8
EPS = 1e-6
NEG_BIG = -1e30
LOG2E = 1.4426950408889634
MASK_LANE = 32
LANES = 128
VMEM_LIMIT = 56 * 1024 * 1024

MAIN_COLS = 4096
GATE_COLS = 128


def _dot(a, b):
    return jnp.dot(a, b, preferred_element_type=F32)


def _dot_nt(a, b):
    return lax.dot_general(a, b, (((1,), (1,)), ((), ())), preferred_element_type=F32)


def _ones_bf16(n):
    return jnp.ones((n, n), BF16)


def _row_tile(lp):
    return 384 if lp % 384 == 0 else 128


def _cparams(sem):
    return pltpu.CompilerParams(dimension_semantics=sem, vmem_limit_bytes=VMEM_LIMIT)


def _inproj_kernel(h_ref, nw_ref, w_ref, wvt_ref, main_ref, gate_ref, vt_ref):
    x = h_ref[0]
    ms = jnp.mean(x * x, axis=-1, keepdims=True)
    u = (x * lax.rsqrt(ms + EPS) * nw_ref[...]).astype(BF16)
    for c0 in range(0, MAIN_COLS, 512):
        main_ref[0, :, c0:c0 + 512] = _dot(u, w_ref[:, c0:c0 + 512]).astype(BF16)
    gate_ref[0] = _dot(u, w_ref[:, MAIN_COLS:MAIN_COLS + GATE_COLS])
    vt_ref[0] = _dot_nt(wvt_ref[...], u).astype(BF16)


def _inproj(h, norm_w, w, wvt):
    b, lp, d = h.shape
    tm = _row_tile(lp)
    ncol = MAIN_COLS + GATE_COLS
    nv = wvt.shape[0]
    return pl.pallas_call(
        _inproj_kernel,
        grid=(b, lp // tm),
        in_specs=[
            pl.BlockSpec((1, tm, d), lambda i, j: (i, j, 0)),
            pl.BlockSpec((1, d), lambda i, j: (0, 0)),
            pl.BlockSpec((d, ncol), lambda i, j: (0, 0)),
            pl.BlockSpec((nv, d), lambda i, j: (0, 0)),
        ],
        out_specs=[
            pl.BlockSpec((1, tm, MAIN_COLS), lambda i, j: (i, j, 0)),
            pl.BlockSpec((1, tm, GATE_COLS), lambda i, j: (i, j, 0)),
            pl.BlockSpec((1, nv, tm), lambda i, j: (i, 0, j)),
        ],
        out_shape=[
            jax.ShapeDtypeStruct((b, lp, MAIN_COLS), BF16),
            jax.ShapeDtypeStruct((b, lp, GATE_COLS), F32),
            jax.ShapeDtypeStruct((b, nv, lp), BF16),
        ],
        compiler_params=_cparams(("parallel", "parallel")),
        name="inproj",
    )(h, norm_w, w, wvt)


def _dnprep_kernel(x_ref, cw_ref, o_ref, xs_ref):
    j = pl.program_id(1)
    lp = x_ref.shape[1]
    halo = 8
    xs_ref[0:halo + N_PAD, :] = jnp.zeros((halo + N_PAD, LANES), F32)
    xs_ref[halo + N_PAD:halo + lp, :] = x_ref[0, N_PAD:, :].astype(F32)
    xs_ref[halo + lp:2 * halo + lp, :] = jnp.zeros((halo, LANES), F32)
    q_scale = jnp.where(j < DN_HEADS, DN_D ** -0.5, 1.0).astype(F32)
    is_qk = j < 2 * DN_HEADS
    rc = 128
    for c in range(lp // rc):
        base = c * rc
        acc = cw_ref[0:1, :] * xs_ref[base + halo - 2:base + halo - 2 + rc, :]
        for t in range(1, DN_CONV):
            off = base + halo - 2 + t
            acc = acc + cw_ref[t:t + 1, :] * xs_ref[off:off + rc, :]
        y = acc * jax.nn.sigmoid(acc)
        ss = jnp.sum(y * y, axis=-1, keepdims=True)
        r = jnp.where(is_qk, lax.rsqrt(ss + EPS) * q_scale, 1.0)
        o_ref[0, base:base + rc, :] = (y * r).astype(BF16)


def _dnprep(main, conv_w):
    b, lp, _ = main.shape
    ncb = 3 * DN_HEADS
    return pl.pallas_call(
        _dnprep_kernel,
        grid=(b, ncb),
        in_specs=[
            pl.BlockSpec((1, lp, LANES), lambda i, j: (i, 0, j)),
            pl.BlockSpec((DN_CONV, LANES), lambda i, j: (0, j)),
        ],
        out_specs=pl.BlockSpec((1, lp, LANES), lambda i, j: (i, 0, j)),
        out_shape=jax.ShapeDtypeStruct((b, lp, ncb * LANES), BF16),
        scratch_shapes=[pltpu.VMEM((lp + 16, LANES), F32)],
        compiler_params=_cparams(("parallel", "parallel")),
        name="dnprep",
    )(main, conv_w)


def _each(fn, *lists):
    return [fn(*args) for args in zip(*lists)]


def _unit_tri_inverse(a_list, eye, blk16, blk32):
    bf = lambda t: t.astype(BF16)
    a0 = _each(lambda a: jnp.where(blk16, a, 0.0), a_list)
    a0b = _each(bf, a0)
    a2b = _each(lambda x: bf(_dot(x, x)), a0b)
    a4b = _each(lambda x: bf(_dot(x, x)), a2b)
    a8b = _each(lambda x: bf(_dot(x, x)), a4b)
    p = _each(lambda x: eye - x, a0)
    for pw in (a2b, a4b, a8b):
        p = _each(lambda x, y: x + _dot(bf(x), y), p, pw)
    off1 = _each(lambda a: bf(jnp.where(blk32 & (~blk16), a, 0.0)), a_list)
    off2 = _each(lambda a: bf(jnp.where(blk32, 0.0, a)), a_list)
    for off in (off1, off2):
        pb = _each(bf, p)
        mid = _each(lambda x, y: bf(_dot(x, y)), pb, off)
        p = _each(lambda x, m, xb: x - _dot(m, xb), p, mid, pb)
    return p


def _dn_kernel(qf_ref, kf_ref, vf_ref, qb_ref, kb_ref, vb_ref, gcf_ref, gcb_ref,
               grf_ref, grb_ref, pc_ref, pr_ref, of_ref, ob_ref, st_ref):
    s = pl.program_id(1)
    c = DN_CHUNK
    nb = qf_ref.shape[0]

    @pl.when(s == 0)
    def _():
        st_ref[...] = jnp.zeros(st_ref.shape, F32)

    row = lax.broadcasted_iota(jnp.int32, (c, c), 0)
    col = lax.broadcasted_iota(jnp.int32, (c, c), 1)
    eye = (row == col).astype(F32)
    blk16 = (row // 16) == (col // 16)
    blk32 = (row // 32) == (col // 32)
    negA_c, dtb_c = pc_ref[0:1, :], pc_ref[1:2, :]
    negA_r, dtb_r = pr_ref[:, 0:c], pr_ref[:, c:2 * c]
    bf = lambda t: t.astype(BF16)

    per_dir = []
    for d, (gc_ref, gr_ref) in enumerate(((gcf_ref, grf_ref), (gcb_ref, grb_ref))):
        if d == 0:
            causal = s >= 0
            chunk = s
        else:
            causal = s <= 1
            chunk = jnp.where(s <= 1, s, (pl.num_programs(1) + 1) - s)
        sgn = jnp.where(causal, 1, -1)
        m_in = (row - col) * sgn >= 0
        m_strict = (row - col) * sgn > 0
        m_f = m_in.astype(F32)
        m_t = ((col - row) * sgn >= 0).astype(F32)
        pos_c = chunk * c + lax.broadcasted_iota(jnp.int32, (c, LANES), 0)
        live_c = pos_c >= N_PAD
        pos_r = chunk * c + lax.broadcasted_iota(jnp.int32, (16, c), 1)
        gates = []
        for bi in range(nb):
            gcol = gc_ref[bi]
            beta_c = jnp.where(live_c, jax.nn.sigmoid(gcol), 0.0)
            g_c = jnp.where(live_c, negA_c * jax.nn.softplus(gcol + dtb_c), 0.0)
            grow = gr_ref[bi, 0]
            g_r = jnp.where(pos_r >= N_PAD, negA_r * jax.nn.softplus(grow + dtb_r), 0.0)
            gc_c = jnp.dot(m_f, g_c, preferred_element_type=F32, precision=lax.Precision.HIGHEST)
            gc_r = jnp.dot(g_r, m_t, preferred_element_type=F32, precision=lax.Precision.HIGHEST)
            gtot_c = jnp.sum(g_c, axis=0, keepdims=True)
            gates.append(dict(beta_c=beta_c, gc_c=gc_c, gc_r=gc_r, eg_c=jnp.exp(gc_c),
                              ew_c=jnp.exp(gtot_c - gc_c), et_c=jnp.exp(gtot_c)))
        per_dir.append(dict(m_in=m_in, m_strict=m_strict, gates=gates))

    refs = ((qf_ref, kf_ref, vf_ref, of_ref), (qb_ref, kb_ref, vb_ref, ob_ref))
    chains = [(bi, d, h) for bi in range(nb) for d in range(2) for h in range(DN_HEADS)]
    sl = lambda h: slice(h * DN_D, (h + 1) * DN_D)
    bcol = lambda d, h: d * DN_HEADS + h
    acol = lambda d, h: 2 * DN_HEADS + d * DN_HEADS + h
    sidx = lambda bi, d, h: (bi * 2 + d) * DN_HEADS + h
    gate = lambda bi, d, name: per_dir[d]["gates"][bi][name]

    q = [refs[d][0][bi, :, sl(h)] for bi, d, h in chains]
    k = [refs[d][1][bi, :, sl(h)] for bi, d, h in chains]
    v = [refs[d][2][bi, :, sl(h)] for bi, d, h in chains]
    kf = _each(lambda t: t.astype(F32), k)
    beta = [gate(bi, d, "beta_c")[:, bcol(d, h):bcol(d, h) + 1] for bi, d, h in chains]
    eg = [gate(bi, d, "eg_c")[:, acol(d, h):acol(d, h) + 1] for bi, d, h in chains]
    kb = _each(lambda x, b: x * b, kf, beta)
    vb = _each(lambda x, b: x.astype(F32) * b, v, beta)

    def decay_of(bi, d, h):
        m_in = per_dir[d]["m_in"]
        diff = gate(bi, d, "gc_c")[:, acol(d, h):acol(d, h) + 1] - gate(bi, d, "gc_r")[acol(d, h):acol(d, h) + 1, :]
        return jnp.where(m_in, jnp.exp(jnp.where(m_in, diff, 0.0)), 0.0)

    decay = [decay_of(*ch) for ch in chains]
    kq = _each(lambda x, y, z: _dot_nt(jnp.concatenate([bf(x), y], axis=0), z), kb, q, k)
    a = [jnp.where(per_dir[d]["m_strict"], kq_i[0:c] * dec, 0.0)
         for (bi, d, h), kq_i, dec in zip(chains, kq, decay)]
    qk = _each(lambda x, dec: bf(x[c:2 * c] * dec), kq, decay)
    t = _unit_tri_inverse(a, eye, blk16, blk32)
    x = _each(lambda x1, x2, e: bf(jnp.concatenate([x1, x2 * e], axis=1)), vb, kb, eg)
    tx = _each(lambda ti, xi: _dot(bf(ti), xi), t, x)
    st = [st_ref[sidx(*ch)] for ch in chains]
    lhs = _each(lambda txi, qi, e: bf(jnp.concatenate([txi[:, DN_D:2 * DN_D], qi.astype(F32) * e], axis=0)),
                tx, q, eg)
    r = _each(lambda l, si: _dot(l, bf(si)), lhs, st)
    v_new = _each(lambda txi, ri: bf(txi[:, 0:DN_D] - ri[0:c]), tx, r)
    out = _each(lambda ri, qki, vn: ri[c:2 * c] + _dot(qki, vn), r, qk, v_new)
    kw = [bf((kfi * gate(bi, d, "ew_c")[:, acol(d, h):acol(d, h) + 1]).T) for (bi, d, h), kfi in zip(chains, kf)]
    upd = _each(_dot, kw, v_new)
    for (bi, d, h), o_i, st_i, u_i in zip(chains, out, st, upd):
        refs[d][3][bi, :, sl(h)] = o_i.astype(BF16)
        st_ref[sidx(bi, d, h)] = st_i * gate(bi, d, "et_c")[:, acol(d, h):acol(d, h) + 1] + u_i


DN_BATCH = 4


def _deltanet(qkv, gate_col, gate_row, pc, pr):
    b, lp, _ = qkv.shape
    n = lp // DN_CHUNK
    hw = DN_HEADS * DN_D
    nb = DN_BATCH if b % DN_BATCH == 0 else 1

    def fwd_chunk(s):
        return s

    def bwd_chunk(s):
        return jnp.where(s <= 1, s, n + 1 - s)

    def spec3(cb, chunk):
        return pl.BlockSpec((nb, DN_CHUNK, hw), lambda i, s: (i, chunk(s), cb))

    in_specs = [spec3(0, fwd_chunk), spec3(1, fwd_chunk), spec3(2, fwd_chunk),
                spec3(0, bwd_chunk), spec3(1, bwd_chunk), spec3(2, bwd_chunk),
                pl.BlockSpec((nb, DN_CHUNK, GATE_COLS), lambda i, s: (i, fwd_chunk(s), 0)),
                pl.BlockSpec((nb, DN_CHUNK, GATE_COLS), lambda i, s: (i, bwd_chunk(s), 0)),
                pl.BlockSpec((nb, 1, 16, DN_CHUNK), lambda i, s: (i, fwd_chunk(s), 0, 0)),
                pl.BlockSpec((nb, 1, 16, DN_CHUNK), lambda i, s: (i, bwd_chunk(s), 0, 0)),
                pl.BlockSpec((2, GATE_COLS), lambda i, s: (0, 0)),
                pl.BlockSpec((16, 2 * DN_CHUNK), lambda i, s: (0, 0))]
    out_specs = [pl.BlockSpec((nb, DN_CHUNK, hw), lambda i, s: (i, fwd_chunk(s), 0)),
                 pl.BlockSpec((nb, DN_CHUNK, hw), lambda i, s: (i, bwd_chunk(s), 0))]
    return pl.pallas_call(
        _dn_kernel,
        grid=(b // nb, n),
        in_specs=in_specs,
        out_specs=out_specs,
        out_shape=[jax.ShapeDtypeStruct((b, lp, hw), BF16)] * 2,
        scratch_shapes=[pltpu.VMEM((nb * 2 * DN_HEADS, DN_D, DN_D), F32)],
        compiler_params=_cparams(("parallel", "arbitrary")),
        name="deltanet",
    )(qkv, qkv, qkv, qkv, qkv, qkv, gate_col, gate_col, gate_row, gate_row, pc, pr)


def _attnprep_kernel(x_ref, dqw_ref, dkw_ref, qlw_ref, wuq_ref, klw_ref, wukv_ref, wvt_ref,
                     mqw_ref, mkw_ref, cos_ref, sin_ref,
                     dq_ref, dk_ref, mq_ref, mk_ref, mvt_ref):
    ones = _ones_bf16(LANES)
    row = lax.broadcasted_iota(jnp.int32, (LANES, LANES), 0)
    col = lax.broadcasted_iota(jnp.int32, (LANES, LANES), 1)
    half_ones = ((row // DIFF_D) == (col // DIFF_D)).astype(BF16)
    cos_t, sin_t = cos_ref[...], sin_ref[...]
    tm = x_ref.shape[1]
    lane_t = lax.broadcasted_iota(jnp.int32, (tm, LANES), 1)
    row_t = pl.program_id(1) * tm + lax.broadcasted_iota(jnp.int32, (tm, LANES), 0)
    q_flag = jnp.where(lane_t == MASK_LANE, 1.0, 0.0)
    k_flag = jnp.where((lane_t == MASK_LANE) & (row_t < N_PAD), NEG_BIG, 0.0)

    def rope(t):
        return t * cos_t + pltpu.roll(t, 64, 1) * sin_t

    for src, w_ref, dst in ((0, dqw_ref, dq_ref), (512, dkw_ref, dk_ref)):
        for h in range(DIFF_HEADS):
            y = x_ref[0, :, src + h * LANES:src + (h + 1) * LANES].astype(F32)
            ms = _dot((y * y).astype(BF16), half_ones) * (1.0 / DIFF_D)
            dst[0, :, h * LANES:(h + 1) * LANES] = (y * lax.rsqrt(ms + EPS) * w_ref[...]).astype(BF16)

    cq0 = x_ref[0, :, 1536:1664].astype(F32)
    cq1 = x_ref[0, :, 1664:1792].astype(F32)
    ms = _dot((cq0 * cq0 + cq1 * cq1).astype(BF16), ones) * (1.0 / MLA_Q_RANK)
    r = lax.rsqrt(ms + EPS)
    cqn = jnp.concatenate([cq0 * r * qlw_ref[:, 0:LANES], cq1 * r * qlw_ref[:, LANES:2 * LANES]],
                          axis=1).astype(BF16)
    q = _dot(cqn, wuq_ref[...])
    inv_d = 1.0 / (MLA_NOPE + MLA_ROPE)
    for h in range(MLA_HEADS):
        q0 = q[:, 256 * h:256 * h + LANES]
        q1 = q[:, 256 * h + LANES:256 * (h + 1)]
        ms = _dot((q0 * q0 + q1 * q1).astype(BF16), ones) * inv_d
        r = lax.rsqrt(ms + EPS)
        mq_ref[0, :, 256 * h:256 * h + LANES] = (q0 * r * mqw_ref[:, 0:LANES]).astype(BF16)
        mq_ref[0, :, 256 * h + LANES:256 * (h + 1)] = (
            rope(q1 * r * mqw_ref[:, LANES:2 * LANES]) + q_flag).astype(BF16)

    ckv = x_ref[0, :, 1792:1920].astype(F32)
    ms = _dot((ckv * ckv).astype(BF16), ones) * (1.0 / MLA_KV_RANK)
    ckvn = (ckv * lax.rsqrt(ms + EPS) * klw_ref[...]).astype(BF16)
    kv = _dot(ckvn, wukv_ref[...])
    mvt_ref[0] = _dot_nt(wvt_ref[...], ckvn).astype(BF16)
    kr = x_ref[0, :, 1920:2048].astype(F32)
    kr2 = kr * kr
    for h in range(MLA_HEADS):
        k0 = kv[:, LANES * h:LANES * (h + 1)]
        ms = _dot((k0 * k0 + kr2).astype(BF16), ones) * inv_d
        r = lax.rsqrt(ms + EPS)
        mk_ref[0, :, 256 * h:256 * h + LANES] = (k0 * r * mkw_ref[:, 0:LANES]).astype(BF16)
        mk_ref[0, :, 256 * h + LANES:256 * (h + 1)] = (
            rope(kr * r * mkw_ref[:, LANES:2 * LANES]) + k_flag).astype(BF16)


def _attnprep(main, dqw, dkw, qlw, wuq, klw, wukv, wvt, mqw, mkw, cos_t, sin_t):
    b, lp, _ = main.shape
    tm = _row_tile(lp)
    nv = wvt.shape[0]

    def full(a):
        return pl.BlockSpec(a.shape, lambda i, j: (0,) * a.ndim)

    def rows(width):
        return pl.BlockSpec((1, tm, width), lambda i, j: (i, j, 0))

    return pl.pallas_call(
        _attnprep_kernel,
        grid=(b, lp // tm),
        in_specs=[pl.BlockSpec((1, tm, 2048), lambda i, j: (i, j, 1)),
                  full(dqw), full(dkw), full(qlw), full(wuq), full(klw), full(wukv), full(wvt),
                  full(mqw), full(mkw),
                  pl.BlockSpec((tm, LANES), lambda i, j: (j, 0)),
                  pl.BlockSpec((tm, LANES), lambda i, j: (j, 0))],
        out_specs=[rows(512), rows(512), rows(1024), rows(1024),
                   pl.BlockSpec((1, nv, tm), lambda i, j: (i, 0, j))],
        out_shape=[jax.ShapeDtypeStruct((b, lp, 512), BF16),
                   jax.ShapeDtypeStruct((b, lp, 512), BF16),
                   jax.ShapeDtypeStruct((b, lp, 1024), BF16),
                   jax.ShapeDtypeStruct((b, lp, 1024), BF16),
                   jax.ShapeDtypeStruct((b, nv, lp), BF16)],
        compiler_params=_cparams(("parallel", "parallel")),
        name="attnprep",
    )(main, dqw, dkw, qlw, wuq, klw, wukv, wvt, mqw, mkw, cos_t, sin_t)


ONES_ROWS = 16


KEY_PARTS = 4


def _key_splits(lp):
    nblk = lp // LANES
    nparts = min(KEY_PARTS, nblk)
    edges = [(nblk * i // nparts) * LANES for i in range(nparts + 1)]
    return tuple(zip(edges[:-1], edges[1:]))


def _attend_t(score_fn, vx_ref, lp):
    splits = _key_splits(lp)
    s = [score_fn(k0, k1) for k0, k1 in splits]
    m = [x.max(axis=0, keepdims=True) for x in s]
    p = [jnp.exp2((x - mi).astype(BF16)) for x, mi in zip(s, m)]
    o = [_dot(vx_ref[:, k0:k1], pi) for (k0, k1), pi in zip(splits, p)]
    m_all = functools.reduce(jnp.maximum, m)
    acc = functools.reduce(lambda a, b: a + b, [oi * jnp.exp2(mi - m_all) for oi, mi in zip(o, m)])
    return acc[0:LANES] / acc[LANES:LANES + 1]


def _diffattn_kernel(q_ref, k_ref, vt_ref, bandt_ref, far_ref, lam_ref, sw_ref, o_ref, kx_ref, vx_ref):
    qi = pl.program_id(2)
    nq = pl.num_programs(2)
    tq = q_ref.shape[1]
    lp = k_ref.shape[1]
    c_neg = jnp.broadcast_to(far_ref[0, 0:1, :], (LANES, LANES)).astype(BF16)

    @pl.when(qi == 0)
    def _():
        kx_ref[:, 0:LANES] = k_ref[0]
        kx_ref[:, LANES:2 * LANES] = jnp.broadcast_to(far_ref[0, 1:2, :], (lp, LANES)).astype(BF16)
        vx_ref[0:LANES, :] = vt_ref[0]
        vx_ref[LANES:LANES + ONES_ROWS, :] = jnp.ones((ONES_ROWS, lp), BF16)

    def put(kb, tile):
        kx_ref[pl.ds(pl.multiple_of(kb * LANES, LANES), LANES), LANES:2 * LANES] = tile

    @pl.when(qi >= 2)
    def _():
        put(qi - 2, c_neg)

    @pl.when(qi >= 1)
    def _():
        put(qi - 1, bandt_ref[0, 0])

    put(qi, bandt_ref[0, 1])

    @pl.when(qi + 1 < nq)
    def _():
        put(qi + 1, bandt_ref[0, 2])

    @pl.when(qi <= 2)
    def _():
        kx_ref[0:N_PAD, LANES:2 * LANES] = jnp.full((N_PAD, LANES), NEG_BIG, BF16)

    lane = lax.broadcasted_iota(jnp.int32, (tq, LANES), 1)
    rowi = lax.broadcasted_iota(jnp.int32, (tq, LANES), 0)
    eye = jnp.where(lane == rowi, 1.0, 0.0).astype(BF16)
    q = q_ref[0]
    zero = jnp.zeros_like(q)
    lhs = jnp.concatenate([jnp.concatenate([jnp.where(lane < DIFF_D, q, zero), eye], axis=1),
                           jnp.concatenate([jnp.where(lane >= DIFF_D, q, zero), eye], axis=1)], axis=0)
    o2 = _attend_t(lambda k0, k1: _dot_nt(kx_ref[k0:k1, :], lhs), vx_ref, lp)
    o = (o2[:, 0:tq] - lam_ref[0:1, 0:1] * o2[:, tq:2 * tq]).T
    ms = jnp.mean(o * o, axis=-1, keepdims=True)
    o_ref[0] = (o * lax.rsqrt(ms + EPS) * sw_ref[...]).astype(BF16)


def _diffattn(dq, dk, dvt, bandt, far, lam, sw):
    b, lp, _ = dq.shape
    tq = LANES
    return pl.pallas_call(
        _diffattn_kernel,
        grid=(b, DIFF_HEADS, lp // tq),
        in_specs=[pl.BlockSpec((1, tq, LANES), lambda i, h, j: (i, j, h)),
                  pl.BlockSpec((1, lp, LANES), lambda i, h, j: (i, 0, h)),
                  pl.BlockSpec((1, LANES, lp), lambda i, h, j: (i, h, 0)),
                  pl.BlockSpec((1, 3, LANES, tq), lambda i, h, j: (h, 0, 0, 0)),
                  pl.BlockSpec((1, 2, LANES), lambda i, h, j: (h, 0, 0)),
                  pl.BlockSpec((1, LANES), lambda i, h, j: (0, 0)),
                  pl.BlockSpec((1, LANES), lambda i, h, j: (0, 0))],
        out_specs=pl.BlockSpec((1, tq, LANES), lambda i, h, j: (i, j, h)),
        out_shape=jax.ShapeDtypeStruct((b, lp, DIFF_HEADS * LANES), BF16),
        scratch_shapes=[pltpu.VMEM((lp, 2 * LANES), BF16), pltpu.VMEM((LANES + ONES_ROWS, lp), BF16)],
        compiler_params=_cparams(("parallel", "parallel", "arbitrary")),
        name="diffattn",
    )(dq, dk, dvt, bandt, far, lam, sw)


def _mlaattn_kernel(q_ref, k_ref, vt_ref, o_ref, vx_ref):
    lp = k_ref.shape[1]

    @pl.when(pl.program_id(2) == 0)
    def _():
        vx_ref[0:LANES, :] = vt_ref[0]
        vx_ref[LANES:LANES + ONES_ROWS, :] = jnp.ones((ONES_ROWS, lp), BF16)

    q = q_ref[0]
    o = _attend_t(lambda k0, k1: _dot_nt(k_ref[0, k0:k1, :], q), vx_ref, lp)
    o_ref[0] = o.T.astype(BF16)


MLA_TQ = 512


def _mlaattn(mq, mk, mvt):
    b, lp, _ = mq.shape
    tq = MLA_TQ
    return pl.pallas_call(
        _mlaattn_kernel,
        grid=(b, MLA_HEADS, pl.cdiv(lp, tq)),
        in_specs=[pl.BlockSpec((1, tq, 256), lambda i, h, j: (i, j, h)),
                  pl.BlockSpec((1, lp, 256), lambda i, h, j: (i, 0, h)),
                  pl.BlockSpec((1, LANES, lp), lambda i, h, j: (i, h, 0))],
        out_specs=pl.BlockSpec((1, tq, LANES), lambda i, h, j: (i, j, h)),
        out_shape=jax.ShapeDtypeStruct((b, lp, MLA_HEADS * LANES), BF16),
        scratch_shapes=[pltpu.VMEM((LANES + ONES_ROWS, lp), BF16)],
        compiler_params=_cparams(("parallel", "parallel", "arbitrary")),
        name="mlaattn",
    )(mq, mk, mvt)


def _outproj_kernel(of_ref, ob_ref, z_ref, df_ref, ml_ref, h_ref, dnw_ref, w_ref, o_ref):
    ones = _ones_bf16(LANES)
    parts = []
    for h in range(DN_HEADS):
        sl = slice(h * DN_D, (h + 1) * DN_D)
        o = of_ref[0, :, sl].astype(F32) + ob_ref[0, :, sl].astype(F32)
        ms = _dot((o * o).astype(BF16), ones) * (1.0 / DN_D)
        z = z_ref[0, :, sl].astype(F32)
        parts.append((o * lax.rsqrt(ms + EPS) * dnw_ref[...] * (z * jax.nn.sigmoid(z))).astype(BF16))
    ydn = jnp.concatenate(parts, axis=1)
    acc = h_ref[0] + _dot(ydn, w_ref[0:512, :])
    acc = acc + _dot(df_ref[0], w_ref[512:1024, :])
    acc = acc + _dot(ml_ref[0], w_ref[1024:1536, :])
    o_ref[0] = acc


def _outproj(o_f, o_b, main, ydf, yml, h, dnw, w):
    b, lp, d = h.shape
    tm = _row_tile(lp)

    def rows(width, cb=0):
        return pl.BlockSpec((1, tm, width), lambda i, j: (i, j, cb))

    return pl.pallas_call(
        _outproj_kernel,
        grid=(b, lp // tm),
        in_specs=[rows(512), rows(512), rows(512, 3), rows(512), rows(512), rows(d),
                  pl.BlockSpec((1, LANES), lambda i, j: (0, 0)),
                  pl.BlockSpec(w.shape, lambda i, j: (0, 0))],
        out_specs=rows(d),
        out_shape=jax.ShapeDtypeStruct((b, lp, d), F32),
        compiler_params=_cparams(("parallel", "parallel")),
        name="outproj",
    )(o_f, o_b, main, ydf, yml, h, dnw, w)


def _ffn_kernel(h_ref, nw_ref, wg_ref, wu_ref, wd_ref, o_ref, a_ref):
    x = h_ref[...]
    ms = jnp.mean(x * x, axis=-1, keepdims=True)
    u = (x * lax.rsqrt(ms + EPS) * nw_ref[...]).astype(BF16)
    f = wg_ref.shape[1]
    fc = 256
    for c0 in range(0, f, fc):
        g = _dot(u, wg_ref[:, c0:c0 + fc])
        up = _dot(u, wu_ref[:, c0:c0 + fc])
        a_ref[:, c0:c0 + fc] = (g * jax.nn.sigmoid(g) * up).astype(BF16)
    o_ref[...] = x + _dot(a_ref[...], wd_ref[...])


def _ffn(h2, nw, wg, wu, wd):
    n, d = h2.shape
    f = wg.shape[1]
    tm = 512 if n % 512 == 0 else 128

    def const(a):
        return pl.BlockSpec(a.shape, lambda i: (0, 0), pipeline_mode=pl.Buffered(1))

    return pl.pallas_call(
        _ffn_kernel,
        grid=(n // tm,),
        in_specs=[pl.BlockSpec((tm, d), lambda i: (i, 0)),
                  pl.BlockSpec((1, d), lambda i: (0, 0)),
                  const(wg), const(wu), const(wd)],
        out_specs=pl.BlockSpec((tm, d), lambda i: (i, 0)),
        out_shape=jax.ShapeDtypeStruct((n, d), F32),
        scratch_shapes=[pltpu.VMEM((tm, f), BF16)],
        compiler_params=_cparams(("parallel",)),
        name="ffn",
    )(h2, nw, wg, wu, wd)


def _router_kernel(h_ref, nw_ref, rw_ref, u_ref, idx_ref, gate_ref):
    x = h_ref[...]
    ms = jnp.mean(x * x, axis=-1, keepdims=True)
    u = x * lax.rsqrt(ms + EPS) * nw_ref[...]
    u_ref[...] = u.astype(BF16)
    logits = jnp.dot(u, rw_ref[...], preferred_element_type=F32, precision=lax.Precision.HIGHEST)
    lane = lax.broadcasted_iota(jnp.int32, logits.shape, 1)
    logits = jnp.where(lane < N_EXPERTS, logits, -jnp.inf)
    m1 = jnp.max(logits, axis=-1, keepdims=True)
    i1 = jnp.min(jnp.where(logits == m1, lane, LANES), axis=-1, keepdims=True)
    rest = jnp.where(lane == i1, -jnp.inf, logits)
    m2 = jnp.max(rest, axis=-1, keepdims=True)
    i2 = jnp.min(jnp.where(rest == m2, lane, LANES), axis=-1, keepdims=True)
    e2 = jnp.exp(m2 - m1)
    g1 = 1.0 / (1.0 + e2)
    g2 = e2 / (1.0 + e2)
    idx_ref[...] = jnp.where(lane == 0, i1, jnp.where(lane == 1, i2, 0))
    gate_ref[...] = jnp.where(lane == 0, g1, jnp.where(lane == 1, g2, 0.0))


def _router(h2, nw, rw):
    n, d = h2.shape
    tm = 512 if n % 512 == 0 else 128
    return pl.pallas_call(
        _router_kernel,
        grid=(n // tm,),
        in_specs=[pl.BlockSpec((tm, d), lambda i: (i, 0)),
                  pl.BlockSpec((1, d), lambda i: (0, 0)),
                  pl.BlockSpec((d, LANES), lambda i: (0, 0))],
        out_specs=[pl.BlockSpec((tm, d), lambda i: (i, 0)),
                   pl.BlockSpec((tm, LANES), lambda i: (i, 0)),
                   pl.BlockSpec((tm, LANES), lambda i: (i, 0))],
        out_shape=[jax.ShapeDtypeStruct((n, d), BF16),
                   jax.ShapeDtypeStruct((n, LANES), jnp.int32),
                   jax.ShapeDtypeStruct((n, LANES), F32)],
        compiler_params=_cparams(("parallel",)),
        name="router",
    )(h2, nw, rw)


def _moe_kernel(te_ref, tv_ref, x_ref, g_ref, wg_ref, wu_ref, wd_ref, o_ref):
    i = pl.program_id(0)
    f = pl.program_id(1)

    @pl.when(f == 0)
    def _():
        o_ref[...] = jnp.zeros(o_ref.shape, F32)

    @pl.when(tv_ref[i] > 0)
    def _():
        x = x_ref[...]
        g = _dot(x, wg_ref[0])
        up = _dot(x, wu_ref[0])
        a = (g * jax.nn.sigmoid(g) * up * g_ref[...]).astype(BF16)
        o_ref[...] += _dot(a, wd_ref[0])


def _moe(x_sorted, gate_sorted, tile_expert, tile_valid, wg, wu, wd, tm):
    p, d = x_sorted.shape
    f = wg.shape[2]
    nf = 2
    fh = f // nf
    grid_spec = pltpu.PrefetchScalarGridSpec(
        num_scalar_prefetch=2,
        grid=(p // tm, nf),
        in_specs=[pl.BlockSpec((tm, d), lambda i, j, te, tv: (i, 0)),
                  pl.BlockSpec((tm, 1), lambda i, j, te, tv: (i, 0)),
                  pl.BlockSpec((1, d, fh), lambda i, j, te, tv: (te[i], 0, j)),
                  pl.BlockSpec((1, d, fh), lambda i, j, te, tv: (te[i], 0, j)),
                  pl.BlockSpec((1, fh, d), lambda i, j, te, tv: (te[i], j, 0))],
        out_specs=pl.BlockSpec((tm, d), lambda i, j, te, tv: (i, 0)),
    )
    return pl.pallas_call(
        _moe_kernel,
        grid_spec=grid_spec,
        out_shape=jax.ShapeDtypeStruct((p, d), F32),
        compiler_params=_cparams(("arbitrary", "arbitrary")),
        name="moe",
    )(tile_expert, tile_valid, x_sorted, gate_sorted, wg, wu, wd)


def _spread_rope_cols(w):
    z = jnp.zeros(w.shape[:-1] + (32,), w.dtype)
    return jnp.concatenate([w[..., 0:32], z, w[..., 32:64], z], axis=-1)


def _relayout_w_in(w):
    dnw = 4 * DN_HEADS * DN_D
    n_gate = 4 * DN_HEADS
    rest = w[:, dnw + n_gate:]
    body = rest[:, :rest.shape[1] - MLA_ROPE]
    kr = _spread_rope_cols(rest[:, rest.shape[1] - MLA_ROPE:])
    gates = jnp.concatenate([w[:, dnw:dnw + n_gate],
                             jnp.zeros((w.shape[0], GATE_COLS - n_gate), w.dtype)], axis=1)
    out = jnp.concatenate([w[:, :dnw], body, kr, gates], axis=1)
    assert out.shape[1] == MAIN_COLS + GATE_COLS
    return out.astype(BF16)


def _relayout_w_uq(w):
    per = MLA_NOPE + MLA_ROPE
    heads = [jnp.concatenate([w[:, h * per:h * per + MLA_NOPE],
                              _spread_rope_cols(w[:, h * per + MLA_NOPE:(h + 1) * per])], axis=1)
             for h in range(MLA_HEADS)]
    return jnp.concatenate(heads, axis=1).astype(BF16)


def _relayout_w_ukv(w):
    per = MLA_NOPE + LANES
    ks = [w[:, h * per:h * per + MLA_NOPE] for h in range(MLA_HEADS)]
    vs = [w[:, h * per + MLA_NOPE:(h + 1) * per] for h in range(MLA_HEADS)]
    return jnp.concatenate(ks, axis=1).astype(BF16), jnp.concatenate(vs, axis=1).T.astype(BF16)


def _diff_v_weight_t(w):
    c0 = 4 * DN_HEADS * DN_D + 4 * DN_HEADS + 4 * DIFF_HEADS * DIFF_D
    return w[:, c0:c0 + DIFF_HEADS * LANES].T.astype(BF16)


def _qk_norm_vec(w, scale):
    return (jnp.concatenate([w[:MLA_NOPE], _spread_rope_cols(w[MLA_NOPE:])]) * scale)[None, :]


def _t5_bucket(rel):
    half = T5_BUCKETS // 2
    exact = half // 2
    n = jnp.abs(rel)
    large = exact + (jnp.log(jnp.maximum(n, exact).astype(F32) / exact)
                     / math.log(T5_MAX_DIST / exact) * (half - exact)).astype(jnp.int32)
    large = jnp.minimum(large, half - 1)
    return jnp.where(rel > 0, half, 0) + jnp.where(n < exact, n, large)


def _t5_tables(table, tq):
    i = jnp.arange(tq)[None, :]
    j = jnp.arange(LANES)[:, None]
    rels = jnp.stack([j - i + (d * LANES) for d in (-1, 0, 1)])
    table = table.astype(F32) * LOG2E
    bandt = jnp.transpose(table[_t5_bucket(rels)], (3, 0, 1, 2)).astype(BF16)
    far_rel = jnp.array([-T5_MAX_DIST, T5_MAX_DIST], jnp.int32)
    far = jnp.transpose(table[_t5_bucket(far_rel)], (1, 0))
    far = jnp.broadcast_to(far[:, :, None], far.shape + (LANES,))
    return bandt, far


def _rope_tables(lp):
    half = MLA_ROPE // 2
    inv = 1.0 / (ROPE_THETA ** (jnp.arange(half, dtype=F32) / half))
    pos = (jnp.arange(lp, dtype=jnp.int32) - N_PAD).astype(F32)
    ang = pos[:, None] * inv[None, :]
    c, s, z = jnp.cos(ang), jnp.sin(ang), jnp.zeros((lp, half), F32)
    return jnp.concatenate([c, z, c, z], axis=1), jnp.concatenate([-s, z, s, z], axis=1)


def _moe_plan(idx, gates, n, tm):
    e_flat = idx.reshape(-1)
    onehot = (e_flat[:, None] == jnp.arange(N_EXPERTS)[None, :]).astype(jnp.int32)
    counts = jnp.sum(onehot, axis=0)
    rank = jnp.sum((jnp.cumsum(onehot, axis=0) - onehot) * onehot, axis=1)
    tiles_per = (counts + tm - 1) // tm
    tile_end = jnp.cumsum(tiles_per)
    start = (tile_end - tiles_per) * tm
    dest = start[e_flat] + rank
    n_tiles = (2 * n) // tm + N_EXPERTS
    p = n_tiles * tm
    slot_tok = jnp.zeros((p,), jnp.int32).at[dest].set(jnp.arange(2 * n, dtype=jnp.int32) // 2)
    slot_gate = jnp.zeros((p,), F32).at[dest].set(gates.reshape(-1))
    t = jnp.arange(n_tiles, dtype=jnp.int32)
    tile_expert = jnp.minimum(jnp.sum((t[:, None] >= tile_end[None, :]).astype(jnp.int32), axis=1),
                              N_EXPERTS - 1).astype(jnp.int32)
    tile_valid = (t < tile_end[-1]).astype(jnp.int32)
    return slot_tok, slot_gate, dest.reshape(n, 2), tile_expert, tile_valid


def kernel(x, meta_tokens, rel_bias_table, attn_norm_w, w_in, dn_conv_w, dn_a_log, dn_dt_bias, dn_norm_w, diff_q_norm_w, diff_k_norm_w, diff_lambda, diff_subln_w, mla_q_lat_norm_w, mla_w_uq, mla_kv_lat_norm_w, mla_w_ukv, mla_q_norm_w, mla_k_norm_w, w_out, ffn_norm_w, ffn_w_gate, ffn_w_up, ffn_w_down, router_w, moe_w_gate, moe_w_up, moe_w_down):
    bsz, seq, d = x.shape
    depth = w_in.shape[0]
    lp = ROW0 + seq
    n = bsz * lp
    assert lp % LANES == 0 and meta_tokens.shape[0] == N_META

    meta = jnp.broadcast_to(meta_tokens[None].astype(x.dtype), (bsz, N_META, d))
    h = jnp.concatenate([jnp.zeros((bsz, N_PAD, d), x.dtype), meta, x], axis=1)

    cos_t, sin_t = _rope_tables(lp)
    bandt, far = _t5_tables(rel_bias_table, LANES)
    n_chunks = lp // DN_CHUNK

    for l in range(depth):
        main, gate, dvt = _inproj(h, attn_norm_w[l][None, :], _relayout_w_in(w_in[l]),
                                  _diff_v_weight_t(w_in[l]))

        qkv = _dnprep(main, dn_conv_w[l])
        gate_row = jnp.transpose(gate[:, :, 0:16].reshape(bsz, n_chunks, DN_CHUNK, 16), (0, 1, 3, 2))
        neg_a = -jnp.exp(dn_a_log[l].astype(F32)).reshape(-1)
        dtb = dn_dt_bias[l].astype(F32).reshape(-1)
        z8 = jnp.zeros((8,), F32)
        pc = jnp.stack([jnp.concatenate([z8, neg_a, jnp.zeros((GATE_COLS - 16,), F32)]),
                        jnp.concatenate([z8, dtb, jnp.zeros((GATE_COLS - 16,), F32)])])
        pr = jnp.concatenate([jnp.broadcast_to(jnp.concatenate([z8, neg_a])[:, None], (16, DN_CHUNK)),
                              jnp.broadcast_to(jnp.concatenate([z8, dtb])[:, None], (16, DN_CHUNK))], axis=1)
        o_f, o_b = _deltanet(qkv, gate, gate_row, pc, pr)

        dscale = DIFF_D ** -0.5 * LOG2E
        mscale = (MLA_NOPE + MLA_ROPE) ** -0.5 * LOG2E
        w_uk, w_uvt = _relayout_w_ukv(mla_w_ukv[l])
        dq, dk, mq, mk, mvt = _attnprep(
            main,
            (jnp.tile(diff_q_norm_w[l], 2) * dscale)[None, :],
            jnp.tile(diff_k_norm_w[l], 2)[None, :],
            mla_q_lat_norm_w[l][None, :], _relayout_w_uq(mla_w_uq[l]),
            mla_kv_lat_norm_w[l][None, :], w_uk, w_uvt,
            _qk_norm_vec(mla_q_norm_w[l], mscale), _qk_norm_vec(mla_k_norm_w[l], 1.0),
            cos_t, sin_t)
        lam_init = 0.8 - 0.6 * math.exp(-0.3 * l)
        lpar = diff_lambda[l].astype(F32)
        lam = jnp.exp(jnp.sum(lpar[0] * lpar[1])) - jnp.exp(jnp.sum(lpar[2] * lpar[3])) + lam_init
        ydf = _diffattn(dq, dk, dvt, bandt, far, jnp.full((1, LANES), lam, F32),
                        (diff_subln_w[l] * (1.0 - lam_init))[None, :])
        yml = _mlaattn(mq, mk, mvt)

        h = _outproj(o_f, o_b, main, ydf, yml, h, dn_norm_w[l][None, :], w_out[l].astype(BF16))

        h2 = h.reshape(n, d)
        if l % 2 == 0:
            i = l // 2
            h2 = _ffn(h2, ffn_norm_w[l][None, :], ffn_w_gate[i].astype(BF16),
                      ffn_w_up[i].astype(BF16), ffn_w_down[i].astype(BF16))
        else:
            i = l // 2
            rw = jnp.concatenate([router_w[i], jnp.zeros((d, LANES - N_EXPERTS), F32)], axis=1)
            u, idx, gates = _router(h2, ffn_norm_w[l][None, :], rw)
            tm = 512 if (2 * n) % 512 == 0 else 128
            slot_tok, slot_gate, tok_slots, tile_expert, tile_valid = _moe_plan(idx[:, 0:2], gates[:, 0:2], n, tm)
            x_sorted = jnp.take(u, slot_tok, axis=0)
            y_sorted = _moe(x_sorted, slot_gate[:, None], tile_expert, tile_valid,
                            moe_w_gate[i].astype(BF16), moe_w_up[i].astype(BF16),
                            moe_w_down[i].astype(BF16), tm)
            h2 = h2 + jnp.take(y_sorted, tok_slots[:, 0], axis=0) + jnp.take(y_sorted, tok_slots[:, 1], axis=0)
        h = h2.reshape(bsz, lp, d)
    return h[:, ROW0:]
```

```python
import functools
import math

import jax
import jax.numpy as jnp
from jax import lax
from jax.experimental import pallas as pl
from jax.experimental.pallas import tpu as pltpu

F32 = jnp.float32
BF16 = jnp.bfloat16

N_META = 16
N_PAD = 112
ROW0 = N_PAD + N_META
DN_HEADS = 4
DN_D = 128
DN_CONV = 5
DN_CHUNK = 64
DIFF_HEADS = 4
DIFF_D = 64
MLA_HEADS = 4
MLA_NOPE = 128
MLA_ROPE = 64
MLA_Q_RANK = 256
MLA_KV_RANK = 128
ROPE_THETA = 10000.0
T5_BUCKETS = 32
T5_MAX_DIST = 128
N_EXPERTS = 8
EPS = 1e-6
NEG_BIG = -1e30
LOG2E = 1.4426950408889634
MASK_LANE = 32
LANES = 128
VMEM_LIMIT = 56 * 1024 * 1024

MAIN_COLS = 4096
GATE_COLS = 128


def _dot(a, b):
    return jnp.dot(a, b, preferred_element_type=F32)


def _dot_nt(a, b):
    return lax.dot_general(a, b, (((1,), (1,)), ((), ())), preferred_element_type=F32)


def _ones_bf16(n):
    return jnp.ones((n, n), BF16)


def _row_tile(lp):
    return 384 if lp % 384 == 0 else 128


def _cparams(sem):
    return pltpu.CompilerParams(dimension_semantics=sem, vmem_limit_bytes=VMEM_LIMIT)


def _inproj_kernel(h_ref, nw_ref, w_ref, wvt_ref, main_ref, gate_ref, vt_ref):
    x = h_ref[0]
    ms = jnp.mean(x * x, axis=-1, keepdims=True)
    u = (x * lax.rsqrt(ms + EPS) * nw_ref[...]).astype(BF16)
    for c0 in range(0, MAIN_COLS, 512):
        main_ref[0, :, c0:c0 + 512] = _dot(u, w_ref[:, c0:c0 + 512]).astype(BF16)
    gate_ref[0] = _dot(u, w_ref[:, MAIN_COLS:MAIN_COLS + GATE_COLS])
    vt_ref[0] = _dot_nt(wvt_ref[...], u).astype(BF16)


def _inproj(h, norm_w, w, wvt):
    b, lp, d = h.shape
    tm = _row_tile(lp)
    ncol = MAIN_COLS + GATE_COLS
    nv = wvt.shape[0]
    return pl.pallas_call(
        _inproj_kernel,
        grid=(b, lp // tm),
        in_specs=[
            pl.BlockSpec((1, tm, d), lambda i, j: (i, j, 0)),
            pl.BlockSpec((1, d), lambda i, j: (0, 0)),
            pl.BlockSpec((d, ncol), lambda i, j: (0, 0)),
            pl.BlockSpec((nv, d), lambda i, j: (0, 0)),
        ],
        out_specs=[
            pl.BlockSpec((1, tm, MAIN_COLS), lambda i, j: (i, j, 0)),
            pl.BlockSpec((1, tm, GATE_COLS), lambda i, j: (i, j, 0)),
            pl.BlockSpec((1, nv, tm), lambda i, j: (i, 0, j)),
        ],
        out_shape=[
            jax.ShapeDtypeStruct((b, lp, MAIN_COLS), BF16),
            jax.ShapeDtypeStruct((b, lp, GATE_COLS), F32),
            jax.ShapeDtypeStruct((b, nv, lp), BF16),
        ],
        compiler_params=_cparams(("parallel", "parallel")),
        name="inproj",
    )(h, norm_w, w, wvt)


def _dnprep_kernel(x_ref, cw_ref, o_ref, xs_ref):
    j = pl.program_id(1)
    lp = x_ref.shape[1]
    halo = 8
    xs_ref[0:halo + N_PAD, :] = jnp.zeros((halo + N_PAD, LANES), F32)
    xs_ref[halo + N_PAD:halo + lp, :] = x_ref[0, N_PAD:, :].astype(F32)
    xs_ref[halo + lp:2 * halo + lp, :] = jnp.zeros((halo, LANES), F32)
    q_scale = jnp.where(j < DN_HEADS, DN_D ** -0.5, 1.0).astype(F32)
    is_qk = j < 2 * DN_HEADS
    rc = 128
    for c in range(lp // rc):
        base = c * rc
        acc = cw_ref[0:1, :] * xs_ref[base + halo - 2:base + halo - 2 + rc, :]
        for t in range(1, DN_CONV):
            off = base + halo - 2 + t
            acc = acc + cw_ref[t:t + 1, :] * xs_ref[off:off + rc, :]
        y = acc * jax.nn.sigmoid(acc)
        ss = jnp.sum(y * y, axis=-1, keepdims=True)
        r = jnp.where(is_qk, lax.rsqrt(ss + EPS) * q_scale, 1.0)
        o_ref[0, base:base + rc, :] = (y * r).astype(BF16)


def _dnprep(main, conv_w):
    b, lp, _ = main.shape
    ncb = 3 * DN_HEADS
    return pl.pallas_call(
        _dnprep_kernel,
        grid=(b, ncb),
        in_specs=[
            pl.BlockSpec((1, lp, LANES), lambda i, j: (i, 0, j)),
            pl.BlockSpec((DN_CONV, LANES), lambda i, j: (0, j)),
        ],
        out_specs=pl.BlockSpec((1, lp, LANES), lambda i, j: (i, 0, j)),
        out_shape=jax.ShapeDtypeStruct((b, lp, ncb * LANES), BF16),
        scratch_shapes=[pltpu.VMEM((lp + 16, LANES), F32)],
        compiler_params=_cparams(("parallel", "parallel")),
        name="dnprep",
    )(main, conv_w)


def _each(fn, *lists):
    return [fn(*args) for args in zip(*lists)]


def _unit_tri_inverse(a_list, eye, blk16, blk32):
    bf = lambda t: t.astype(BF16)
    a0 = _each(lambda a: jnp.where(blk16, a, 0.0), a_list)
    a0b = _each(bf, a0)
    a2b = _each(lambda x: bf(_dot(x, x)), a0b)
    a4b = _each(lambda x: bf(_dot(x, x)), a2b)
    a8b = _each(lambda x: bf(_dot(x, x)), a4b)
    p = _each(lambda x: eye - x, a0)
    for pw in (a2b, a4b, a8b):
        p = _each(lambda x, y: x + _dot(bf(x), y), p, pw)
    off1 = _each(lambda a: bf(jnp.where(blk32 & (~blk16), a, 0.0)), a_list)
    off2 = _each(lambda a: bf(jnp.where(blk32, 0.0, a)), a_list)
    for off in (off1, off2):
        pb = _each(bf, p)
        mid = _each(lambda x, y: bf(_dot(x, y)), pb, off)
        p = _each(lambda x, m, xb: x - _dot(m, xb), p, mid, pb)
    return p


def _dn_kernel(qf_ref, kf_ref, vf_ref, qb_ref, kb_ref, vb_ref, gcf_ref, gcb_ref,
               grf_ref, grb_ref, pc_ref, pr_ref, of_ref, ob_ref, st_ref):
    s = pl.program_id(1)
    c = DN_CHUNK
    nb = qf_ref.shape[0]

    @pl.when(s == 0)
    def _():
        st_ref[...] = jnp.zeros(st_ref.shape, F32)

    row = lax.broadcasted_iota(jnp.int32, (c, c), 0)
    col = lax.broadcasted_iota(jnp.int32, (c, c), 1)
    eye = (row == col).astype(F32)
    blk16 = (row // 16) == (col // 16)
    blk32 = (row // 32) == (col // 32)
    negA_c, dtb_c = pc_ref[0:1, :], pc_ref[1:2, :]
    negA_r, dtb_r = pr_ref[:, 0:c], pr_ref[:, c:2 * c]
    bf = lambda t: t.astype(BF16)

    per_dir = []
    for d, (gc_ref, gr_ref) in enumerate(((gcf_ref, grf_ref), (gcb_ref, grb_ref))):
        if d == 0:
            causal = s >= 0
            chunk = s
        else:
            causal = s <= 1
            chunk = jnp.where(s <= 1, s, (pl.num_programs(1) + 1) - s)
        sgn = jnp.where(causal, 1, -1)
        m_in = (row - col) * sgn >= 0
        m_strict = (row - col) * sgn > 0
        m_f = m_in.astype(F32)
        m_t = ((col - row) * sgn >= 0).astype(F32)
        pos_c = chunk * c + lax.broadcasted_iota(jnp.int32, (c, LANES), 0)
        live_c = pos_c >= N_PAD
        pos_r = chunk * c + lax.broadcasted_iota(jnp.int32, (16, c), 1)
        gates = []
        for bi in range(nb):
            gcol = gc_ref[bi]
            beta_c = jnp.where(live_c, jax.nn.sigmoid(gcol), 0.0)
            g_c = jnp.where(live_c, negA_c * jax.nn.softplus(gcol + dtb_c), 0.0)
            grow = gr_ref[bi, 0]
            g_r = jnp.where(pos_r >= N_PAD, negA_r * jax.nn.softplus(grow + dtb_r), 0.0)
            gc_c = jnp.dot(m_f, g_c, preferred_element_type=F32, precision=lax.Precision.HIGHEST)
            gc_r = jnp.dot(g_r, m_t, preferred_element_type=F32, precision=lax.Precision.HIGHEST)
            gtot_c = jnp.sum(g_c, axis=0, keepdims=True)
            gates.append(dict(beta_c=beta_c, gc_c=gc_c, gc_r=gc_r, eg_c=jnp.exp(gc_c),
                              ew_c=jnp.exp(gtot_c - gc_c), et_c=jnp.exp(gtot_c)))
        per_dir.append(dict(m_in=m_in, m_strict=m_strict, gates=gates))

    refs = ((qf_ref, kf_ref, vf_ref, of_ref), (qb_ref, kb_ref, vb_ref, ob_ref))
    chains = [(bi, d, h) for bi in range(nb) for d in range(2) for h in range(DN_HEADS)]
    sl = lambda h: slice(h * DN_D, (h + 1) * DN_D)
    bcol = lambda d, h: d * DN_HEADS + h
    acol = lambda d, h: 2 * DN_HEADS + d * DN_HEADS + h
    sidx = lambda bi, d, h: (bi * 2 + d) * DN_HEADS + h
    gate = lambda bi, d, name: per_dir[d]["gates"][bi][name]

    q = [refs[d][0][bi, :, sl(h)] for bi, d, h in chains]
    k = [refs[d][1][bi, :, sl(h)] for bi, d, h in chains]
    v = [refs[d][2][bi, :, sl(h)] for bi, d, h in chains]
    kf = _each(lambda t: t.astype(F32), k)
    beta = [gate(bi, d, "beta_c")[:, bcol(d, h):bcol(d, h) + 1] for bi, d, h in chains]
    eg = [gate(bi, d, "eg_c")[:, acol(d, h):acol(d, h) + 1] for bi, d, h in chains]
    kb = _each(lambda x, b: x * b, kf, beta)
    vb = _each(lambda x, b: x.astype(F32) * b, v, beta)

    def decay_of(bi, d, h):
        m_in = per_dir[d]["m_in"]
        diff = gate(bi, d, "gc_c")[:, acol(d, h):acol(d, h) + 1] - gate(bi, d, "gc_r")[acol(d, h):acol(d, h) + 1, :]
        return jnp.where(m_in, jnp.exp(jnp.where(m_in, diff, 0.0)), 0.0)

    decay = [decay_of(*ch) for ch in chains]
    kq = _each(lambda x, y, z: _dot_nt(jnp.concatenate([bf(x), y], axis=0), z), kb, q, k)
    a = [jnp.where(per_dir[d]["m_strict"], kq_i[0:c] * dec, 0.0)
         for (bi, d, h), kq_i, dec in zip(chains, kq, decay)]
    qk = _each(lambda x, dec: bf(x[c:2 * c] * dec), kq, decay)
    t = _unit_tri_inverse(a, eye, blk16, blk32)
    x = _each(lambda x1, x2, e: bf(jnp.concatenate([x1, x2 * e], axis=1)), vb, kb, eg)
    tx = _each(lambda ti, xi: _dot(bf(ti), xi), t, x)
    st = [st_ref[sidx(*ch)] for ch in chains]
    lhs = _each(lambda txi, qi, e: bf(jnp.concatenate([txi[:, DN_D:2 * DN_D], qi.astype(F32) * e], axis=0)),
                tx, q, eg)
    r = _each(lambda l, si: _dot(l, bf(si)), lhs, st)
    v_new = _each(lambda txi, ri: bf(txi[:, 0:DN_D] - ri[0:c]), tx, r)
    out = _each(lambda ri, qki, vn: ri[c:2 * c] + _dot(qki, vn), r, qk, v_new)
    kw = [bf((kfi * gate(bi, d, "ew_c")[:, acol(d, h):acol(d, h) + 1]).T) for (bi, d, h), kfi in zip(chains, kf)]
    upd = _each(_dot, kw, v_new)
    for (bi, d, h), o_i, st_i, u_i in zip(chains, out, st, upd):
        refs[d][3][bi, :, sl(h)] = o_i.astype(BF16)
        st_ref[sidx(bi, d, h)] = st_i * gate(bi, d, "et_c")[:, acol(d, h):acol(d, h) + 1] + u_i


DN_BATCH = 4


def _deltanet(qkv, gate_col, gate_row, pc, pr):
    b, lp, _ = qkv.shape
    n = lp // DN_CHUNK
    hw = DN_HEADS * DN_D
    nb = DN_BATCH if b % DN_BATCH == 0 else 1

    def fwd_chunk(s):
        return s

    def bwd_chunk(s):
        return jnp.where(s <= 1, s, n + 1 - s)

    def spec3(cb, chunk):
        return pl.BlockSpec((nb, DN_CHUNK, hw), lambda i, s: (i, chunk(s), cb))

    in_specs = [spec3(0, fwd_chunk), spec3(1, fwd_chunk), spec3(2, fwd_chunk),
                spec3(0, bwd_chunk), spec3(1, bwd_chunk), spec3(2, bwd_chunk),
                pl.BlockSpec((nb, DN_CHUNK, GATE_COLS), lambda i, s: (i, fwd_chunk(s), 0)),
                pl.BlockSpec((nb, DN_CHUNK, GATE_COLS), lambda i, s: (i, bwd_chunk(s), 0)),
                pl.BlockSpec((nb, 1, 16, DN_CHUNK), lambda i, s: (i, fwd_chunk(s), 0, 0)),
                pl.BlockSpec((nb, 1, 16, DN_CHUNK), lambda i, s: (i, bwd_chunk(s), 0, 0)),
                pl.BlockSpec((2, GATE_COLS), lambda i, s: (0, 0)),
                pl.BlockSpec((16, 2 * DN_CHUNK), lambda i, s: (0, 0))]
    out_specs = [pl.BlockSpec((nb, DN_CHUNK, hw), lambda i, s: (i, fwd_chunk(s), 0)),
                 pl.BlockSpec((nb, DN_CHUNK, hw), lambda i, s: (i, bwd_chunk(s), 0))]
    return pl.pallas_call(
        _dn_kernel,
        grid=(b // nb, n),
        in_specs=in_specs,
        out_specs=out_specs,
        out_shape=[jax.ShapeDtypeStruct((b, lp, hw), BF16)] * 2,
        scratch_shapes=[pltpu.VMEM((nb * 2 * DN_HEADS, DN_D, DN_D), F32)],
        compiler_params=_cparams(("parallel", "arbitrary")),
        name="deltanet",
    )(qkv, qkv, qkv, qkv, qkv, qkv, gate_col, gate_col, gate_row, gate_row, pc, pr)


def _attnprep_kernel(x_ref, dqw_ref, dkw_ref, qlw_ref, wuq_ref, klw_ref, wukv_ref, wvt_ref,
                     mqw_ref, mkw_ref, cos_ref, sin_ref,
                     dq_ref, dk_ref, mq_ref, mk_ref, mvt_ref):
    ones = _ones_bf16(LANES)
    row = lax.broadcasted_iota(jnp.int32, (LANES, LANES), 0)
    col = lax.broadcasted_iota(jnp.int32, (LANES, LANES), 1)
    half_ones = ((row // DIFF_D) == (col // DIFF_D)).astype(BF16)
    cos_t, sin_t = cos_ref[...], sin_ref[...]
    tm = x_ref.shape[1]
    lane_t = lax.broadcasted_iota(jnp.int32, (tm, LANES), 1)
    row_t = pl.program_id(1) * tm + lax.broadcasted_iota(jnp.int32, (tm, LANES), 0)
    q_flag = jnp.where(lane_t == MASK_LANE, 1.0, 0.0)
    k_flag = jnp.where((lane_t == MASK_LANE) & (row_t < N_PAD), NEG_BIG, 0.0)

    def rope(t):
        return t * cos_t + pltpu.roll(t, 64, 1) * sin_t

    for src, w_ref, dst in ((0, dqw_ref, dq_ref), (512, dkw_ref, dk_ref)):
        for h in range(DIFF_HEADS):
            y = x_ref[0, :, src + h * LANES:src + (h + 1) * LANES].astype(F32)
            ms = _dot((y * y).astype(BF16), half_ones) * (1.0 / DIFF_D)
            dst[0, :, h * LANES:(h + 1) * LANES] = (y * lax.rsqrt(ms + EPS) * w_ref[...]).astype(BF16)

    cq0 = x_ref[0, :, 1536:1664].astype(F32)
    cq1 = x_ref[0, :, 1664:1792].astype(F32)
    ms = _dot((cq0 * cq0 + cq1 * cq1).astype(BF16), ones) * (1.0 / MLA_Q_RANK)
    r = lax.rsqrt(ms + EPS)
    cqn = jnp.concatenate([cq0 * r * qlw_ref[:, 0:LANES], cq1 * r * qlw_ref[:, LANES:2 * LANES]],
                          axis=1).astype(BF16)
    q = _dot(cqn, wuq_ref[...])
    inv_d = 1.0 / (MLA_NOPE + MLA_ROPE)
    for h in range(MLA_HEADS):
        q0 = q[:, 256 * h:256 * h + LANES]
        q1 = q[:, 256 * h + LANES:256 * (h + 1)]
        ms = _dot((q0 * q0 + q1 * q1).astype(BF16), ones) * inv_d
        r = lax.rsqrt(ms + EPS)
        mq_ref[0, :, 256 * h:256 * h + LANES] = (q0 * r * mqw_ref[:, 0:LANES]).astype(BF16)
        mq_ref[0, :, 256 * h + LANES:256 * (h + 1)] = (
            rope(q1 * r * mqw_ref[:, LANES:2 * LANES]) + q_flag).astype(BF16)

    ckv = x_ref[0, :, 1792:1920].astype(F32)
    ms = _dot((ckv * ckv).astype(BF16), ones) * (1.0 / MLA_KV_RANK)
    ckvn = (ckv * lax.rsqrt(ms + EPS) * klw_ref[...]).astype(BF16)
    kv = _dot(ckvn, wukv_ref[...])
    mvt_ref[0] = _dot_nt(wvt_ref[...], ckvn).astype(BF16)
    kr = x_ref[0, :, 1920:2048].astype(F32)
    kr2 = kr * kr
    for h in range(MLA_HEADS):
        k0 = kv[:, LANES * h:LANES * (h + 1)]
        ms = _dot((k0 * k0 + kr2).astype(BF16), ones) * inv_d
        r = lax.rsqrt(ms + EPS)
        mk_ref[0, :, 256 * h:256 * h + LANES] = (k0 * r * mkw_ref[:, 0:LANES]).astype(BF16)
        mk_ref[0, :, 256 * h + LANES:256 * (h + 1)] = (
            rope(kr * r * mkw_ref[:, LANES:2 * LANES]) + k_flag).astype(BF16)


def _attnprep(main, dqw, dkw, qlw, wuq, klw, wukv, wvt, mqw, mkw, cos_t, sin_t):
    b, lp, _ = main.shape
    tm = _row_tile(lp)
    nv = wvt.shape[0]

    def full(a):
        return pl.BlockSpec(a.shape, lambda i, j: (0,) * a.ndim)

    def rows(width):
        return pl.BlockSpec((1, tm, width), lambda i, j: (i, j, 0))

    return pl.pallas_call(
        _attnprep_kernel,
        grid=(b, lp // tm),
        in_specs=[pl.BlockSpec((1, tm, 2048), lambda i, j: (i, j, 1)),
                  full(dqw), full(dkw), full(qlw), full(wuq), full(klw), full(wukv), full(wvt),
                  full(mqw), full(mkw),
                  pl.BlockSpec((tm, LANES), lambda i, j: (j, 0)),
                  pl.BlockSpec((tm, LANES), lambda i, j: (j, 0))],
        out_specs=[rows(512), rows(512), rows(1024), rows(1024),
                   pl.BlockSpec((1, nv, tm), lambda i, j: (i, 0, j))],
        out_shape=[jax.ShapeDtypeStruct((b, lp, 512), BF16),
                   jax.ShapeDtypeStruct((b, lp, 512), BF16),
                   jax.ShapeDtypeStruct((b, lp, 1024), BF16),
                   jax.ShapeDtypeStruct((b, lp, 1024), BF16),
                   jax.ShapeDtypeStruct((b, nv, lp), BF16)],
        compiler_params=_cparams(("parallel", "parallel")),
        name="attnprep",
    )(main, dqw, dkw, qlw, wuq, klw, wukv, wvt, mqw, mkw, cos_t, sin_t)


ONES_ROWS = 16


KEY_PARTS = 6


def _key_splits(lp):
    nblk = lp // LANES
    nparts = min(KEY_PARTS, nblk)
    edges = [(nblk * i // nparts) * LANES for i in range(nparts + 1)]
    return tuple(zip(edges[:-1], edges[1:]))


def _attend_t(score_fns, vx_ref, lp):
    splits = _key_splits(lp)
    pieces = [(fn, k0, k1) for fn in score_fns for k0, k1 in splits]
    s = [fn(k0, k1) for fn, k0, k1 in pieces]
    m = [x.max(axis=0, keepdims=True) for x in s]
    p = [jnp.exp2((x - mi).astype(BF16)) for x, mi in zip(s, m)]
    o = [_dot(vx_ref[:, k0:k1], pi) for (_, k0, k1), pi in zip(pieces, p)]
    outs = []
    n = len(splits)
    for t in range(len(score_fns)):
        mt, ot = m[t * n:(t + 1) * n], o[t * n:(t + 1) * n]
        m_all = functools.reduce(jnp.maximum, mt)
        acc = functools.reduce(lambda a, b: a + b, [oi * jnp.exp2(mi - m_all) for oi, mi in zip(ot, mt)])
        outs.append(acc[0:LANES] / acc[LANES:LANES + 1])
    return outs


def _diffattn_kernel(q_ref, k_ref, vt_ref, bandt_ref, far_ref, lam_ref, sw_ref, o_ref, kx_ref, vx_ref):
    qi = pl.program_id(2)
    tq = LANES
    lp = k_ref.shape[1]
    nt = kx_ref.shape[0]
    nblk = lp // LANES
    c_neg = jnp.broadcast_to(far_ref[0, 0:1, :], (LANES, LANES)).astype(BF16)

    @pl.when(qi == 0)
    def _():
        c_pos = jnp.broadcast_to(far_ref[0, 1:2, :], (lp, LANES)).astype(BF16)
        for t in range(nt):
            kx_ref[t, :, 0:LANES] = k_ref[0]
            kx_ref[t, :, LANES:2 * LANES] = c_pos
        vx_ref[0:LANES, :] = vt_ref[0]
        vx_ref[LANES:LANES + ONES_ROWS, :] = jnp.ones((ONES_ROWS, lp), BF16)

    lane = lax.broadcasted_iota(jnp.int32, (tq, LANES), 1)
    rowi = lax.broadcasted_iota(jnp.int32, (tq, LANES), 0)
    eye = jnp.where(lane == rowi, 1.0, 0.0).astype(BF16)
    score_fns = []
    for t in range(nt):
        tile = qi * nt + t

        def put(kb, val, t=t):
            @pl.when((kb >= 0) & (kb < nblk))
            def _():
                kx_ref[t, pl.ds(pl.multiple_of(kb * LANES, LANES), LANES), LANES:2 * LANES] = val

        for back in range(nt):
            put(tile - 2 - back, c_neg)
        for dd in range(3):
            put(tile - 1 + dd, bandt_ref[0, dd])

        @pl.when(tile <= nt + 1)
        def _(t=t):
            kx_ref[t, 0:N_PAD, LANES:2 * LANES] = jnp.full((N_PAD, LANES), NEG_BIG, BF16)

        q = q_ref[0, t * tq:(t + 1) * tq, :]
        zero = jnp.zeros_like(q)
        lhs = jnp.concatenate([jnp.concatenate([jnp.where(lane < DIFF_D, q, zero), eye], axis=1),
                               jnp.concatenate([jnp.where(lane >= DIFF_D, q, zero), eye], axis=1)], axis=0)
        score_fns.append(lambda k0, k1, t=t, lhs=lhs: _dot_nt(kx_ref[t, k0:k1, :], lhs))
    for t, o2 in enumerate(_attend_t(score_fns, vx_ref, lp)):
        o = (o2[:, 0:tq] - lam_ref[0:1, 0:1] * o2[:, tq:2 * tq]).T
        ms = jnp.mean(o * o, axis=-1, keepdims=True)
        o_ref[0, t * tq:(t + 1) * tq, :] = (o * lax.rsqrt(ms + EPS) * sw_ref[...]).astype(BF16)


DIFF_TILES = 3


def _diffattn(dq, dk, dvt, bandt, far, lam, sw):
    b, lp, _ = dq.shape
    nblk = lp // LANES
    nt = DIFF_TILES if nblk % DIFF_TILES == 0 else 1
    rows = nt * LANES
    return pl.pallas_call(
        _diffattn_kernel,
        grid=(b, DIFF_HEADS, nblk // nt),
        in_specs=[pl.BlockSpec((1, rows, LANES), lambda i, h, j: (i, j, h)),
                  pl.BlockSpec((1, lp, LANES), lambda i, h, j: (i, 0, h)),
                  pl.BlockSpec((1, LANES, lp), lambda i, h, j: (i, h, 0)),
                  pl.BlockSpec((1, 3, LANES, LANES), lambda i, h, j: (h, 0, 0, 0)),
                  pl.BlockSpec((1, 2, LANES), lambda i, h, j: (h, 0, 0)),
                  pl.BlockSpec((1, LANES), lambda i, h, j: (0, 0)),
                  pl.BlockSpec((1, LANES), lambda i, h, j: (0, 0))],
        out_specs=pl.BlockSpec((1, rows, LANES), lambda i, h, j: (i, j, h)),
        out_shape=jax.ShapeDtypeStruct((b, lp, DIFF_HEADS * LANES), BF16),
        scratch_shapes=[pltpu.VMEM((nt, lp, 2 * LANES), BF16), pltpu.VMEM((LANES + ONES_ROWS, lp), BF16)],
        compiler_params=_cparams(("parallel", "parallel", "arbitrary")),
        name="diffattn",
    )(dq, dk, dvt, bandt, far, lam, sw)


def _mlaattn_kernel(q_ref, k_ref, vt_ref, o_ref, vx_ref):
    lp = k_ref.shape[1]

    @pl.when(pl.program_id(2) == 0)
    def _():
        vx_ref[0:LANES, :] = vt_ref[0]
        vx_ref[LANES:LANES + ONES_ROWS, :] = jnp.ones((ONES_ROWS, lp), BF16)

    q = q_ref[0]
    (o,) = _attend_t([lambda k0, k1: _dot_nt(k_ref[0, k0:k1, :], q)], vx_ref, lp)
    o_ref[0] = o.T.astype(BF16)


MLA_TQ = 512


def _mlaattn(mq, mk, mvt):
    b, lp, _ = mq.shape
    tq = MLA_TQ
    return pl.pallas_call(
        _mlaattn_kernel,
        grid=(b, MLA_HEADS, pl.cdiv(lp, tq)),
        in_specs=[pl.BlockSpec((1, tq, 256), lambda i, h, j: (i, j, h)),
                  pl.BlockSpec((1, lp, 256), lambda i, h, j: (i, 0, h)),
                  pl.BlockSpec((1, LANES, lp), lambda i, h, j: (i, h, 0))],
        out_specs=pl.BlockSpec((1, tq, LANES), lambda i, h, j: (i, j, h)),
        out_shape=jax.ShapeDtypeStruct((b, lp, MLA_HEADS * LANES), BF16),
        scratch_shapes=[pltpu.VMEM((LANES + ONES_ROWS, lp), BF16)],
        compiler_params=_cparams(("parallel", "parallel", "arbitrary")),
        name="mlaattn",
    )(mq, mk, mvt)


def _outproj_kernel(of_ref, ob_ref, z_ref, df_ref, ml_ref, h_ref, dnw_ref, w_ref, o_ref):
    ones = _ones_bf16(LANES)
    parts = []
    for h in range(DN_HEADS):
        sl = slice(h * DN_D, (h + 1) * DN_D)
        o = of_ref[0, :, sl].astype(F32) + ob_ref[0, :, sl].astype(F32)
        ms = _dot((o * o).astype(BF16), ones) * (1.0 / DN_D)
        z = z_ref[0, :, sl].astype(F32)
        parts.append((o * lax.rsqrt(ms + EPS) * dnw_ref[...] * (z * jax.nn.sigmoid(z))).astype(BF16))
    ydn = jnp.concatenate(parts, axis=1)
    acc = h_ref[0] + _dot(ydn, w_ref[0:512, :])
    acc = acc + _dot(df_ref[0], w_ref[512:1024, :])
    acc = acc + _dot(ml_ref[0], w_ref[1024:1536, :])
    o_ref[0] = acc


def _outproj(o_f, o_b, main, ydf, yml, h, dnw, w):
    b, lp, d = h.shape
    tm = _row_tile(lp)

    def rows(width, cb=0):
        return pl.BlockSpec((1, tm, width), lambda i, j: (i, j, cb))

    return pl.pallas_call(
        _outproj_kernel,
        grid=(b, lp // tm),
        in_specs=[rows(512), rows(512), rows(512, 3), rows(512), rows(512), rows(d),
                  pl.BlockSpec((1, LANES), lambda i, j: (0, 0)),
                  pl.BlockSpec(w.shape, lambda i, j: (0, 0))],
        out_specs=rows(d),
        out_shape=jax.ShapeDtypeStruct((b, lp, d), F32),
        compiler_params=_cparams(("parallel", "parallel")),
        name="outproj",
    )(o_f, o_b, main, ydf, yml, h, dnw, w)


def _ffn_kernel(h_ref, nw_ref, wg_ref, wu_ref, wd_ref, o_ref, a_ref):
    x = h_ref[...]
    ms = jnp.mean(x * x, axis=-1, keepdims=True)
    u = (x * lax.rsqrt(ms + EPS) * nw_ref[...]).astype(BF16)
    f = wg_ref.shape[1]
    fc = 256
    for c0 in range(0, f, fc):
        g = _dot(u, wg_ref[:, c0:c0 + fc])
        up = _dot(u, wu_ref[:, c0:c0 + fc])
        a_ref[:, c0:c0 + fc] = (g * jax.nn.sigmoid(g) * up).astype(BF16)
    o_ref[...] = x + _dot(a_ref[...], wd_ref[...])


def _ffn(h2, nw, wg, wu, wd):
    n, d = h2.shape
    f = wg.shape[1]
    tm = 512 if n % 512 == 0 else 128

    def const(a):
        return pl.BlockSpec(a.shape, lambda i: (0, 0), pipeline_mode=pl.Buffered(1))

    return pl.pallas_call(
        _ffn_kernel,
        grid=(n // tm,),
        in_specs=[pl.BlockSpec((tm, d), lambda i: (i, 0)),
                  pl.BlockSpec((1, d), lambda i: (0, 0)),
                  const(wg), const(wu), const(wd)],
        out_specs=pl.BlockSpec((tm, d), lambda i: (i, 0)),
        out_shape=jax.ShapeDtypeStruct((n, d), F32),
        scratch_shapes=[pltpu.VMEM((tm, f), BF16)],
        compiler_params=_cparams(("parallel",)),
        name="ffn",
    )(h2, nw, wg, wu, wd)


def _router_kernel(h_ref, nw_ref, rw_ref, u_ref, idx_ref, gate_ref):
    x = h_ref[...]
    ms = jnp.mean(x * x, axis=-1, keepdims=True)
    u = x * lax.rsqrt(ms + EPS) * nw_ref[...]
    u_ref[...] = u.astype(BF16)
    logits = jnp.dot(u, rw_ref[...], preferred_element_type=F32, precision=lax.Precision.HIGHEST)
    lane = lax.broadcasted_iota(jnp.int32, logits.shape, 1)
    logits = jnp.where(lane < N_EXPERTS, logits, -jnp.inf)
    m1 = jnp.max(logits, axis=-1, keepdims=True)
    i1 = jnp.min(jnp.where(logits == m1, lane, LANES), axis=-1, keepdims=True)
    rest = jnp.where(lane == i1, -jnp.inf, logits)
    m2 = jnp.max(rest, axis=-1, keepdims=True)
    i2 = jnp.min(jnp.where(rest == m2, lane, LANES), axis=-1, keepdims=True)
    e2 = jnp.exp(m2 - m1)
    g1 = 1.0 / (1.0 + e2)
    g2 = e2 / (1.0 + e2)
    idx_ref[...] = jnp.where(lane == 0, i1, jnp.where(lane == 1, i2, 0))
    gate_ref[...] = jnp.where(lane == 0, g1, jnp.where(lane == 1, g2, 0.0))


def _router(h2, nw, rw):
    n, d = h2.shape
    tm = 512 if n % 512 == 0 else 128
    return pl.pallas_call(
        _router_kernel,
        grid=(n // tm,),
        in_specs=[pl.BlockSpec((tm, d), lambda i: (i, 0)),
                  pl.BlockSpec((1, d), lambda i: (0, 0)),
                  pl.BlockSpec((d, LANES), lambda i: (0, 0))],
        out_specs=[pl.BlockSpec((tm, d), lambda i: (i, 0)),
                   pl.BlockSpec((tm, LANES), lambda i: (i, 0)),
                   pl.BlockSpec((tm, LANES), lambda i: (i, 0))],
        out_shape=[jax.ShapeDtypeStruct((n, d), BF16),
                   jax.ShapeDtypeStruct((n, LANES), jnp.int32),
                   jax.ShapeDtypeStruct((n, LANES), F32)],
        compiler_params=_cparams(("parallel",)),
        name="router",
    )(h2, nw, rw)


def _moe_kernel(te_ref, tv_ref, x_ref, wg_ref, wu_ref, wd_ref, o_ref):
    i = pl.program_id(0)
    f = pl.program_id(1)

    @pl.when(f == 0)
    def _():
        o_ref[...] = jnp.zeros(o_ref.shape, F32)

    @pl.when(tv_ref[i] > 0)
    def _():
        x = x_ref[...]
        g = _dot(x, wg_ref[0])
        up = _dot(x, wu_ref[0])
        a = (g * jax.nn.sigmoid(g) * up).astype(BF16)
        o_ref[...] += _dot(a, wd_ref[0])


def _moe(x_sorted, tile_expert, tile_valid, wg, wu, wd, tm):
    p, d = x_sorted.shape
    f = wg.shape[2]
    nf = 2
    fh = f // nf
    grid_spec = pltpu.PrefetchScalarGridSpec(
        num_scalar_prefetch=2,
        grid=(p // tm, nf),
        in_specs=[pl.BlockSpec((tm, d), lambda i, j, te, tv: (i, 0)),
                  pl.BlockSpec((1, d, fh), lambda i, j, te, tv: (te[i], 0, j)),
                  pl.BlockSpec((1, d, fh), lambda i, j, te, tv: (te[i], 0, j)),
                  pl.BlockSpec((1, fh, d), lambda i, j, te, tv: (te[i], j, 0))],
        out_specs=pl.BlockSpec((tm, d), lambda i, j, te, tv: (i, 0)),
    )
    return pl.pallas_call(
        _moe_kernel,
        grid_spec=grid_spec,
        out_shape=jax.ShapeDtypeStruct((p, d), F32),
        compiler_params=_cparams(("arbitrary", "arbitrary")),
        name="moe",
    )(tile_expert, tile_valid, x_sorted, wg, wu, wd)


def _spread_rope_cols(w):
    z = jnp.zeros(w.shape[:-1] + (32,), w.dtype)
    return jnp.concatenate([w[..., 0:32], z, w[..., 32:64], z], axis=-1)


def _relayout_w_in(w):
    dnw = 4 * DN_HEADS * DN_D
    n_gate = 4 * DN_HEADS
    rest = w[:, dnw + n_gate:]
    body = rest[:, :rest.shape[1] - MLA_ROPE]
    kr = _spread_rope_cols(rest[:, rest.shape[1] - MLA_ROPE:])
    gates = jnp.concatenate([w[:, dnw:dnw + n_gate],
                             jnp.zeros((w.shape[0], GATE_COLS - n_gate), w.dtype)], axis=1)
    out = jnp.concatenate([w[:, :dnw], body, kr, gates], axis=1)
    assert out.shape[1] == MAIN_COLS + GATE_COLS
    return out.astype(BF16)


def _relayout_w_uq(w):
    per = MLA_NOPE + MLA_ROPE
    heads = [jnp.concatenate([w[:, h * per:h * per + MLA_NOPE],
                              _spread_rope_cols(w[:, h * per + MLA_NOPE:(h + 1) * per])], axis=1)
             for h in range(MLA_HEADS)]
    return jnp.concatenate(heads, axis=1).astype(BF16)


def _relayout_w_ukv(w):
    per = MLA_NOPE + LANES
    ks = [w[:, h * per:h * per + MLA_NOPE] for h in range(MLA_HEADS)]
    vs = [w[:, h * per + MLA_NOPE:(h + 1) * per] for h in range(MLA_HEADS)]
    return jnp.concatenate(ks, axis=1).astype(BF16), jnp.concatenate(vs, axis=1).T.astype(BF16)


def _diff_v_weight_t(w):
    c0 = 4 * DN_HEADS * DN_D + 4 * DN_HEADS + 4 * DIFF_HEADS * DIFF_D
    return w[:, c0:c0 + DIFF_HEADS * LANES].T.astype(BF16)


def _qk_norm_vec(w, scale):
    return (jnp.concatenate([w[:MLA_NOPE], _spread_rope_cols(w[MLA_NOPE:])]) * scale)[None, :]


def _t5_bucket(rel):
    half = T5_BUCKETS // 2
    exact = half // 2
    n = jnp.abs(rel)
    large = exact + (jnp.log(jnp.maximum(n, exact).astype(F32) / exact)
                     / math.log(T5_MAX_DIST / exact) * (half - exact)).astype(jnp.int32)
    large = jnp.minimum(large, half - 1)
    return jnp.where(rel > 0, half, 0) + jnp.where(n < exact, n, large)


def _t5_tables(table, tq):
    i = jnp.arange(tq)[None, :]
    j = jnp.arange(LANES)[:, None]
    rels = jnp.stack([j - i + (d * LANES) for d in (-1, 0, 1)])
    table = table.astype(F32) * LOG2E
    bandt = jnp.transpose(table[_t5_bucket(rels)], (3, 0, 1, 2)).astype(BF16)
    far_rel = jnp.array([-T5_MAX_DIST, T5_MAX_DIST], jnp.int32)
    far = jnp.transpose(table[_t5_bucket(far_rel)], (1, 0))
    far = jnp.broadcast_to(far[:, :, None], far.shape + (LANES,))
    return bandt, far


def _rope_tables(lp):
    half = MLA_ROPE // 2
    inv = 1.0 / (ROPE_THETA ** (jnp.arange(half, dtype=F32) / half))
    pos = (jnp.arange(lp, dtype=jnp.int32) - N_PAD).astype(F32)
    ang = pos[:, None] * inv[None, :]
    c, s, z = jnp.cos(ang), jnp.sin(ang), jnp.zeros((lp, half), F32)
    return jnp.concatenate([c, z, c, z], axis=1), jnp.concatenate([-s, z, s, z], axis=1)


def _moe_plan(idx, n, tm):
    e_flat = idx.reshape(-1)
    onehot = (e_flat[:, None] == jnp.arange(N_EXPERTS)[None, :]).astype(jnp.int32)
    counts = jnp.sum(onehot, axis=0)
    rank = jnp.sum((jnp.cumsum(onehot, axis=0) - onehot) * onehot, axis=1)
    tiles_per = (counts + tm - 1) // tm
    tile_end = jnp.cumsum(tiles_per)
    start = (tile_end - tiles_per) * tm
    dest = start[e_flat] + rank
    n_tiles = (2 * n) // tm + N_EXPERTS
    p = n_tiles * tm
    slot_tok = jnp.zeros((p,), jnp.int32).at[dest].set(jnp.arange(2 * n, dtype=jnp.int32) // 2)
    t = jnp.arange(n_tiles, dtype=jnp.int32)
    tile_expert = jnp.minimum(jnp.sum((t[:, None] >= tile_end[None, :]).astype(jnp.int32), axis=1),
                              N_EXPERTS - 1).astype(jnp.int32)
    tile_valid = (t < tile_end[-1]).astype(jnp.int32)
    return slot_tok, dest.reshape(n, 2), tile_expert, tile_valid


def kernel(x, meta_tokens, rel_bias_table, attn_norm_w, w_in, dn_conv_w, dn_a_log, dn_dt_bias, dn_norm_w, diff_q_norm_w, diff_k_norm_w, diff_lambda, diff_subln_w, mla_q_lat_norm_w, mla_w_uq, mla_kv_lat_norm_w, mla_w_ukv, mla_q_norm_w, mla_k_norm_w, w_out, ffn_norm_w, ffn_w_gate, ffn_w_up, ffn_w_down, router_w, moe_w_gate, moe_w_up, moe_w_down):
    bsz, seq, d = x.shape
    depth = w_in.shape[0]
    lp = ROW0 + seq
    n = bsz * lp
    assert lp % LANES == 0 and meta_tokens.shape[0] == N_META

    meta = jnp.broadcast_to(meta_tokens[None].astype(x.dtype), (bsz, N_META, d))
    h = jnp.concatenate([jnp.zeros((bsz, N_PAD, d), x.dtype), meta, x], axis=1)

    cos_t, sin_t = _rope_tables(lp)
    bandt, far = _t5_tables(rel_bias_table, LANES)
    n_chunks = lp // DN_CHUNK

    for l in range(depth):
        main, gate, dvt = _inproj(h, attn_norm_w[l][None, :], _relayout_w_in(w_in[l]),
                                  _diff_v_weight_t(w_in[l]))

        qkv = _dnprep(main, dn_conv_w[l])
        gate_row = jnp.transpose(gate[:, :, 0:16].reshape(bsz, n_chunks, DN_CHUNK, 16), (0, 1, 3, 2))
        neg_a = -jnp.exp(dn_a_log[l].astype(F32)).reshape(-1)
        dtb = dn_dt_bias[l].astype(F32).reshape(-1)
        z8 = jnp.zeros((8,), F32)
        pc = jnp.stack([jnp.concatenate([z8, neg_a, jnp.zeros((GATE_COLS - 16,), F32)]),
                        jnp.concatenate([z8, dtb, jnp.zeros((GATE_COLS - 16,), F32)])])
        pr = jnp.concatenate([jnp.broadcast_to(jnp.concatenate([z8, neg_a])[:, None], (16, DN_CHUNK)),
                              jnp.broadcast_to(jnp.concatenate([z8, dtb])[:, None], (16, DN_CHUNK))], axis=1)
        o_f, o_b = _deltanet(qkv, gate, gate_row, pc, pr)

        dscale = DIFF_D ** -0.5 * LOG2E
        mscale = (MLA_NOPE + MLA_ROPE) ** -0.5 * LOG2E
        w_uk, w_uvt = _relayout_w_ukv(mla_w_ukv[l])
        dq, dk, mq, mk, mvt = _attnprep(
            main,
            (jnp.tile(diff_q_norm_w[l], 2) * dscale)[None, :],
            jnp.tile(diff_k_norm_w[l], 2)[None, :],
            mla_q_lat_norm_w[l][None, :], _relayout_w_uq(mla_w_uq[l]),
            mla_kv_lat_norm_w[l][None, :], w_uk, w_uvt,
            _qk_norm_vec(mla_q_norm_w[l], mscale), _qk_norm_vec(mla_k_norm_w[l], 1.0),
            cos_t, sin_t)
        lam_init = 0.8 - 0.6 * math.exp(-0.3 * l)
        lpar = diff_lambda[l].astype(F32)
        lam = jnp.exp(jnp.sum(lpar[0] * lpar[1])) - jnp.exp(jnp.sum(lpar[2] * lpar[3])) + lam_init
        ydf = _diffattn(dq, dk, dvt, bandt, far, jnp.full((1, LANES), lam, F32),
                        (diff_subln_w[l] * (1.0 - lam_init))[None, :])
        yml = _mlaattn(mq, mk, mvt)

        h = _outproj(o_f, o_b, main, ydf, yml, h, dn_norm_w[l][None, :], w_out[l].astype(BF16))

        h2 = h.reshape(n, d)
        if l % 2 == 0:
            i = l // 2
            h2 = _ffn(h2, ffn_norm_w[l][None, :], ffn_w_gate[i].astype(BF16),
                      ffn_w_up[i].astype(BF16), ffn_w_down[i].astype(BF16))
        else:
            i = l // 2
            rw = jnp.concatenate([router_w[i], jnp.zeros((d, LANES - N_EXPERTS), F32)], axis=1)
            u, idx, gates = _router(h2, ffn_norm_w[l][None, :], rw)
            tm = 512 if (2 * n) % 512 == 0 else 128
            slot_tok, tok_slots, tile_expert, tile_valid = _moe_plan(idx[:, 0:2], n, tm)
            x_sorted = jnp.take(u, slot_tok, axis=0)
            y_sorted = _moe(x_sorted, tile_expert, tile_valid,
                            moe_w_gate[i].astype(BF16), moe_w_up[i].astype(BF16),
                            moe_w_down[i].astype(BF16), tm)
            h2 = (h2 + gates[:, 0:1] * jnp.take(y_sorted, tok_slots[:, 0], axis=0)
                  + gates[:, 1:2] * jnp.take(y_sorted, tok_slots[:, 1], axis=0))
        h = h2.reshape(bsz, lp, d)
    return h[:, ROW0:]
```

```python
import functools
import math

import jax
import jax.numpy as jnp
from jax import lax
from jax.experimental import pallas as pl
from jax.experimental.pallas import tpu as pltpu

F32 = jnp.float32
BF16 = jnp.bfloat16

N_META = 16
N_PAD = 112
ROW0 = N_PAD + N_META
DN_HEADS = 4
DN_D = 128
DN_CONV = 5
DN_CHUNK = 64
DIFF_HEADS = 4
DIFF_D = 64
MLA_HEADS = 4
MLA_NOPE = 128
MLA_ROPE = 64
MLA_Q_RANK = 256
MLA_KV_RANK = 128
ROPE_THETA = 10000.0
T5_BUCKETS = 32
T5_MAX_DIST = 128
N_EXPERTS = 8
EPS = 1e-6
NEG_BIG = -1e30
LOG2E = 1.4426950408889634
MASK_LANE = 32
LANES = 128
VMEM_LIMIT = 56 * 1024 * 1024

MAIN_COLS = 4096
GATE_COLS = 128


def _dot(a, b):
    return jnp.dot(a, b, preferred_element_type=F32)


def _dot_nt(a, b):
    return lax.dot_general(a, b, (((1,), (1,)), ((), ())), preferred_element_type=F32)


def _ones_bf16(n):
    return jnp.ones((n, n), BF16)


def _row_tile(lp):
    return 384 if lp % 384 == 0 else 128


def _cparams(sem):
    return pltpu.CompilerParams(dimension_semantics=sem, vmem_limit_bytes=VMEM_LIMIT)


def _inproj_kernel(h_ref, nw_ref, w_ref, wvt_ref, main_ref, gate_ref, vt_ref):
    x = h_ref[0]
    ms = jnp.mean(x * x, axis=-1, keepdims=True)
    u = (x * lax.rsqrt(ms + EPS) * nw_ref[...]).astype(BF16)
    for c0 in range(0, MAIN_COLS, 512):
        main_ref[0, :, c0:c0 + 512] = _dot(u, w_ref[:, c0:c0 + 512]).astype(BF16)
    gate_ref[0] = _dot(u, w_ref[:, MAIN_COLS:MAIN_COLS + GATE_COLS])
    vt_ref[0] = _dot_nt(wvt_ref[...], u).astype(BF16)


def _inproj(h, norm_w, w, wvt):
    b, lp, d = h.shape
    tm = _row_tile(lp)
    ncol = MAIN_COLS + GATE_COLS
    nv = wvt.shape[0]
    return pl.pallas_call(
        _inproj_kernel,
        grid=(b, lp // tm),
        in_specs=[
            pl.BlockSpec((1, tm, d), lambda i, j: (i, j, 0)),
            pl.BlockSpec((1, d), lambda i, j: (0, 0)),
            pl.BlockSpec((d, ncol), lambda i, j: (0, 0)),
            pl.BlockSpec((nv, d), lambda i, j: (0, 0)),
        ],
        out_specs=[
            pl.BlockSpec((1, tm, MAIN_COLS), lambda i, j: (i, j, 0)),
            pl.BlockSpec((1, tm, GATE_COLS), lambda i, j: (i, j, 0)),
            pl.BlockSpec((1, nv, tm), lambda i, j: (i, 0, j)),
        ],
        out_shape=[
            jax.ShapeDtypeStruct((b, lp, MAIN_COLS), BF16),
            jax.ShapeDtypeStruct((b, lp, GATE_COLS), F32),
            jax.ShapeDtypeStruct((b, nv, lp), BF16),
        ],
        compiler_params=_cparams(("parallel", "parallel")),
        name="inproj",
    )(h, norm_w, w, wvt)


def _dnprep_kernel(x_ref, cw_ref, o_ref, xs_ref):
    j = pl.program_id(1)
    lp = x_ref.shape[1]
    halo = 8
    xs_ref[0:halo + N_PAD, :] = jnp.zeros((halo + N_PAD, LANES), F32)
    xs_ref[halo + N_PAD:halo + lp, :] = x_ref[0, N_PAD:, :].astype(F32)
    xs_ref[halo + lp:2 * halo + lp, :] = jnp.zeros((halo, LANES), F32)
    q_scale = jnp.where(j < DN_HEADS, DN_D ** -0.5, 1.0).astype(F32)
    is_qk = j < 2 * DN_HEADS
    rc = 128
    for c in range(lp // rc):
        base = c * rc
        acc = cw_ref[0:1, :] * xs_ref[base + halo - 2:base + halo - 2 + rc, :]
        for t in range(1, DN_CONV):
            off = base + halo - 2 + t
            acc = acc + cw_ref[t:t + 1, :] * xs_ref[off:off + rc, :]
        y = acc * jax.nn.sigmoid(acc)
        ss = jnp.sum(y * y, axis=-1, keepdims=True)
        r = jnp.where(is_qk, lax.rsqrt(ss + EPS) * q_scale, 1.0)
        o_ref[0, base:base + rc, :] = (y * r).astype(BF16)


def _dnprep(main, conv_w):
    b, lp, _ = main.shape
    ncb = 3 * DN_HEADS
    return pl.pallas_call(
        _dnprep_kernel,
        grid=(b, ncb),
        in_specs=[
            pl.BlockSpec((1, lp, LANES), lambda i, j: (i, 0, j)),
            pl.BlockSpec((DN_CONV, LANES), lambda i, j: (0, j)),
        ],
        out_specs=pl.BlockSpec((1, lp, LANES), lambda i, j: (i, 0, j)),
        out_shape=jax.ShapeDtypeStruct((b, lp, ncb * LANES), BF16),
        scratch_shapes=[pltpu.VMEM((lp + 16, LANES), F32)],
        compiler_params=_cparams(("parallel", "parallel")),
        name="dnprep",
    )(main, conv_w)


def _each(fn, *lists):
    return [fn(*args) for args in zip(*lists)]


def _unit_tri_inverse(a_list, eye, blk16, blk32):
    bf = lambda t: t.astype(BF16)
    a0 = _each(lambda a: jnp.where(blk16, a, 0.0), a_list)
    a0b = _each(bf, a0)
    a2b = _each(lambda x: bf(_dot(x, x)), a0b)
    a4b = _each(lambda x: bf(_dot(x, x)), a2b)
    a8b = _each(lambda x: bf(_dot(x, x)), a4b)
    p = _each(lambda x: eye - x, a0)
    for pw in (a2b, a4b, a8b):
        p = _each(lambda x, y: x + _dot(bf(x), y), p, pw)
    off1 = _each(lambda a: bf(jnp.where(blk32 & (~blk16), a, 0.0)), a_list)
    off2 = _each(lambda a: bf(jnp.where(blk32, 0.0, a)), a_list)
    for off in (off1, off2):
        pb = _each(bf, p)
        mid = _each(lambda x, y: bf(_dot(x, y)), pb, off)
        p = _each(lambda x, m, xb: x - _dot(m, xb), p, mid, pb)
    return p


def _dn_kernel(qf_ref, kf_ref, vf_ref, qb_ref, kb_ref, vb_ref, gcf_ref, gcb_ref,
               grf_ref, grb_ref, pc_ref, pr_ref, of_ref, ob_ref, st_ref):
    s = pl.program_id(1)
    c = DN_CHUNK
    nb = qf_ref.shape[0]

    @pl.when(s == 0)
    def _():
        st_ref[...] = jnp.zeros(st_ref.shape, F32)

    row = lax.broadcasted_iota(jnp.int32, (c, c), 0)
    col = lax.broadcasted_iota(jnp.int32, (c, c), 1)
    eye = (row == col).astype(F32)
    blk16 = (row // 16) == (col // 16)
    blk32 = (row // 32) == (col // 32)
    negA_c, dtb_c = pc_ref[0:1, :], pc_ref[1:2, :]
    negA_r, dtb_r = pr_ref[:, 0:c], pr_ref[:, c:2 * c]
    bf = lambda t: t.astype(BF16)

    per_dir = []
    for d, (gc_ref, gr_ref) in enumerate(((gcf_ref, grf_ref), (gcb_ref, grb_ref))):
        if d == 0:
            causal = s >= 0
            chunk = s
        else:
            causal = s <= 1
            chunk = jnp.where(s <= 1, s, (pl.num_programs(1) + 1) - s)
        sgn = jnp.where(causal, 1, -1)
        m_in = (row - col) * sgn >= 0
        m_strict = (row - col) * sgn > 0
        m_f = m_in.astype(F32)
        m_t = ((col - row) * sgn >= 0).astype(F32)
        pos_c = chunk * c + lax.broadcasted_iota(jnp.int32, (c, LANES), 0)
        live_c = pos_c >= N_PAD
        pos_r = chunk * c + lax.broadcasted_iota(jnp.int32, (16, c), 1)
        gates = []
        for bi in range(nb):
            gcol = gc_ref[bi]
            beta_c = jnp.where(live_c, jax.nn.sigmoid(gcol), 0.0)
            g_c = jnp.where(live_c, negA_c * jax.nn.softplus(gcol + dtb_c), 0.0)
            grow = gr_ref[bi, 0]
            g_r = jnp.where(pos_r >= N_PAD, negA_r * jax.nn.softplus(grow + dtb_r), 0.0)
            gc_c = jnp.dot(m_f, g_c, preferred_element_type=F32, precision=lax.Precision.HIGHEST)
            gc_r = jnp.dot(g_r, m_t, preferred_element_type=F32, precision=lax.Precision.HIGHEST)
            gtot_c = jnp.sum(g_c, axis=0, keepdims=True)
            gates.append(dict(beta_c=beta_c, gc_c=gc_c, gc_r=gc_r, eg_c=jnp.exp(gc_c),
                              ew_c=jnp.exp(gtot_c - gc_c), et_c=jnp.exp(gtot_c)))
        per_dir.append(dict(m_in=m_in, m_strict=m_strict, gates=gates))

    refs = ((qf_ref, kf_ref, vf_ref, of_ref), (qb_ref, kb_ref, vb_ref, ob_ref))
    chains = [(bi, d, h) for bi in range(nb) for d in range(2) for h in range(DN_HEADS)]
    sl = lambda h: slice(h * DN_D, (h + 1) * DN_D)
    bcol = lambda d, h: d * DN_HEADS + h
    acol = lambda d, h: 2 * DN_HEADS + d * DN_HEADS + h
    sidx = lambda bi, d, h: (bi * 2 + d) * DN_HEADS + h
    gate = lambda bi, d, name: per_dir[d]["gates"][bi][name]

    q = [refs[d][0][bi, :, sl(h)] for bi, d, h in chains]
    k = [refs[d][1][bi, :, sl(h)] for bi, d, h in chains]
    v = [refs[d][2][bi, :, sl(h)] for bi, d, h in chains]
    kf = _each(lambda t: t.astype(F32), k)
    beta = [gate(bi, d, "beta_c")[:, bcol(d, h):bcol(d, h) + 1] for bi, d, h in chains]
    eg = [gate(bi, d, "eg_c")[:, acol(d, h):acol(d, h) + 1] for bi, d, h in chains]
    kb = _each(lambda x, b: x * b, kf, beta)
    vb = _each(lambda x, b: x.astype(F32) * b, v, beta)

    def decay_of(bi, d, h):
        m_in = per_dir[d]["m_in"]
        diff = gate(bi, d, "gc_c")[:, acol(d, h):acol(d, h) + 1] - gate(bi, d, "gc_r")[acol(d, h):acol(d, h) + 1, :]
        return jnp.where(m_in, jnp.exp(jnp.where(m_in, diff, 0.0)), 0.0)

    decay = [decay_of(*ch) for ch in chains]
    kq = _each(lambda x, y, z: _dot_nt(jnp.concatenate([bf(x), y], axis=0), z), kb, q, k)
    a = [jnp.where(per_dir[d]["m_strict"], kq_i[0:c] * dec, 0.0)
         for (bi, d, h), kq_i, dec in zip(chains, kq, decay)]
    qk = _each(lambda x, dec: bf(x[c:2 * c] * dec), kq, decay)
    t = _unit_tri_inverse(a, eye, blk16, blk32)
    x = _each(lambda x1, x2, e: bf(jnp.concatenate([x1, x2 * e], axis=1)), vb, kb, eg)
    tx = _each(lambda ti, xi: _dot(bf(ti), xi), t, x)
    st = [st_ref[sidx(*ch)] for ch in chains]
    lhs = _each(lambda txi, qi, e: bf(jnp.concatenate([txi[:, DN_D:2 * DN_D], qi.astype(F32) * e], axis=0)),
                tx, q, eg)
    r = _each(lambda l, si: _dot(l, bf(si)), lhs, st)
    v_new = _each(lambda txi, ri: bf(txi[:, 0:DN_D] - ri[0:c]), tx, r)
    out = _each(lambda ri, qki, vn: ri[c:2 * c] + _dot(qki, vn), r, qk, v_new)
    kw = [bf((kfi * gate(bi, d, "ew_c")[:, acol(d, h):acol(d, h) + 1]).T) for (bi, d, h), kfi in zip(chains, kf)]
    upd = _each(_dot, kw, v_new)
    for (bi, d, h), o_i, st_i, u_i in zip(chains, out, st, upd):
        refs[d][3][bi, :, sl(h)] = o_i.astype(BF16)
        st_ref[sidx(bi, d, h)] = st_i * gate(bi, d, "et_c")[:, acol(d, h):acol(d, h) + 1] + u_i


DN_BATCH = 4


def _deltanet(qkv, gate_col, gate_row, pc, pr):
    b, lp, _ = qkv.shape
    n = lp // DN_CHUNK
    hw = DN_HEADS * DN_D
    nb = DN_BATCH if b % DN_BATCH == 0 else 1

    def fwd_chunk(s):
        return s

    def bwd_chunk(s):
        return jnp.where(s <= 1, s, n + 1 - s)

    def spec3(cb, chunk):
        return pl.BlockSpec((nb, DN_CHUNK, hw), lambda i, s: (i, chunk(s), cb))

    in_specs = [spec3(0, fwd_chunk), spec3(1, fwd_chunk), spec3(2, fwd_chunk),
                spec3(0, bwd_chunk), spec3(1, bwd_chunk), spec3(2, bwd_chunk),
                pl.BlockSpec((nb, DN_CHUNK, GATE_COLS), lambda i, s: (i, fwd_chunk(s), 0)),
                pl.BlockSpec((nb, DN_CHUNK, GATE_COLS), lambda i, s: (i, bwd_chunk(s), 0)),
                pl.BlockSpec((nb, 1, 16, DN_CHUNK), lambda i, s: (i, fwd_chunk(s), 0, 0)),
                pl.BlockSpec((nb, 1, 16, DN_CHUNK), lambda i, s: (i, bwd_chunk(s), 0, 0)),
                pl.BlockSpec((2, GATE_COLS), lambda i, s: (0, 0)),
                pl.BlockSpec((16, 2 * DN_CHUNK), lambda i, s: (0, 0))]
    out_specs = [pl.BlockSpec((nb, DN_CHUNK, hw), lambda i, s: (i, fwd_chunk(s), 0)),
                 pl.BlockSpec((nb, DN_CHUNK, hw), lambda i, s: (i, bwd_chunk(s), 0))]
    return pl.pallas_call(
        _dn_kernel,
        grid=(b // nb, n),
        in_specs=in_specs,
        out_specs=out_specs,
        out_shape=[jax.ShapeDtypeStruct((b, lp, hw), BF16)] * 2,
        scratch_shapes=[pltpu.VMEM((nb * 2 * DN_HEADS, DN_D, DN_D), F32)],
        compiler_params=_cparams(("parallel", "arbitrary")),
        name="deltanet",
    )(qkv, qkv, qkv, qkv, qkv, qkv, gate_col, gate_col, gate_row, gate_row, pc, pr)


def _attnprep_kernel(x_ref, dqw_ref, dkw_ref, qlw_ref, wuq_ref, klw_ref, wukv_ref,
                     mqw_ref, mkw_ref, cos_ref, sin_ref,
                     dq_ref, dk_ref, mq_ref, mk_ref, mv_ref):
    ones = _ones_bf16(LANES)
    row = lax.broadcasted_iota(jnp.int32, (LANES, LANES), 0)
    col = lax.broadcasted_iota(jnp.int32, (LANES, LANES), 1)
    half_ones = ((row // DIFF_D) == (col // DIFF_D)).astype(BF16)
    cos_t, sin_t = cos_ref[...], sin_ref[...]
    tm = x_ref.shape[1]
    lane_t = lax.broadcasted_iota(jnp.int32, (tm, LANES), 1)
    row_t = pl.program_id(1) * tm + lax.broadcasted_iota(jnp.int32, (tm, LANES), 0)
    q_flag = jnp.where(lane_t == MASK_LANE, 1.0, 0.0)
    k_flag = jnp.where((lane_t == MASK_LANE) & (row_t < N_PAD), NEG_BIG, 0.0)

    def rope(t):
        return t * cos_t + pltpu.roll(t, 64, 1) * sin_t

    for src, w_ref, dst in ((0, dqw_ref, dq_ref), (512, dkw_ref, dk_ref)):
        for h in range(DIFF_HEADS):
            y = x_ref[0, :, src + h * LANES:src + (h + 1) * LANES].astype(F32)
            ms = _dot((y * y).astype(BF16), half_ones) * (1.0 / DIFF_D)
            dst[0, :, h * LANES:(h + 1) * LANES] = (y * lax.rsqrt(ms + EPS) * w_ref[...]).astype(BF16)

    cq0 = x_ref[0, :, 1536:1664].astype(F32)
    cq1 = x_ref[0, :, 1664:1792].astype(F32)
    ms = _dot((cq0 * cq0 + cq1 * cq1).astype(BF16), ones) * (1.0 / MLA_Q_RANK)
    r = lax.rsqrt(ms + EPS)
    cqn = jnp.concatenate([cq0 * r * qlw_ref[:, 0:LANES], cq1 * r * qlw_ref[:, LANES:2 * LANES]],
                          axis=1).astype(BF16)
    q = _dot(cqn, wuq_ref[...])
    inv_d = 1.0 / (MLA_NOPE + MLA_ROPE)
    for h in range(MLA_HEADS):
        q0 = q[:, 256 * h:256 * h + LANES]
        q1 = q[:, 256 * h + LANES:256 * (h + 1)]
        ms = _dot((q0 * q0 + q1 * q1).astype(BF16), ones) * inv_d
        r = lax.rsqrt(ms + EPS)
        mq_ref[0, :, 256 * h:256 * h + LANES] = (q0 * r * mqw_ref[:, 0:LANES]).astype(BF16)
        mq_ref[0, :, 256 * h + LANES:256 * (h + 1)] = (
            rope(q1 * r * mqw_ref[:, LANES:2 * LANES]) + q_flag).astype(BF16)

    ckv = x_ref[0, :, 1792:1920].astype(F32)
    ms = _dot((ckv * ckv).astype(BF16), ones) * (1.0 / MLA_KV_RANK)
    ckvn = (ckv * lax.rsqrt(ms + EPS) * klw_ref[...]).astype(BF16)
    kv = _dot(ckvn, wukv_ref[...])
    mv_ref[0] = kv[:, 512:1024].astype(BF16)
    kr = x_ref[0, :, 1920:2048].astype(F32)
    kr2 = kr * kr
    for h in range(MLA_HEADS):
        k0 = kv[:, LANES * h:LANES * (h + 1)]
        ms = _dot((k0 * k0 + kr2).astype(BF16), ones) * inv_d
        r = lax.rsqrt(ms + EPS)
        mk_ref[0, :, 256 * h:256 * h + LANES] = (k0 * r * mkw_ref[:, 0:LANES]).astype(BF16)
        mk_ref[0, :, 256 * h + LANES:256 * (h + 1)] = (
            rope(kr * r * mkw_ref[:, LANES:2 * LANES]) + k_flag).astype(BF16)


def _attnprep(main, dqw, dkw, qlw, wuq, klw, wukv, mqw, mkw, cos_t, sin_t):
    b, lp, _ = main.shape
    tm = _row_tile(lp)

    def full(a):
        return pl.BlockSpec(a.shape, lambda i, j: (0,) * a.ndim)

    def rows(width):
        return pl.BlockSpec((1, tm, width), lambda i, j: (i, j, 0))

    return pl.pallas_call(
        _attnprep_kernel,
        grid=(b, lp // tm),
        in_specs=[pl.BlockSpec((1, tm, 2048), lambda i, j: (i, j, 1)),
                  full(dqw), full(dkw), full(qlw), full(wuq), full(klw), full(wukv),
                  full(mqw), full(mkw),
                  pl.BlockSpec((tm, LANES), lambda i, j: (j, 0)),
                  pl.BlockSpec((tm, LANES), lambda i, j: (j, 0))],
        out_specs=[rows(512), rows(512), rows(1024), rows(1024), rows(512)],
        out_shape=[jax.ShapeDtypeStruct((b, lp, 512), BF16),
                   jax.ShapeDtypeStruct((b, lp, 512), BF16),
                   jax.ShapeDtypeStruct((b, lp, 1024), BF16),
                   jax.ShapeDtypeStruct((b, lp, 1024), BF16),
                   jax.ShapeDtypeStruct((b, lp, 512), BF16)],
        compiler_params=_cparams(("parallel", "parallel")),
        name="attnprep",
    )(main, dqw, dkw, qlw, wuq, klw, wukv, mqw, mkw, cos_t, sin_t)


ONES_ROWS = 16


KEY_PARTS = 6


def _key_splits(lp, parts=None):
    nblk = lp // LANES
    nparts = min(parts or KEY_PARTS, nblk)
    edges = [(nblk * i // nparts) * LANES for i in range(nparts + 1)]
    return tuple(zip(edges[:-1], edges[1:]))


def _attend_t(score_fns, vx_ref, lp):
    splits = _key_splits(lp)
    pieces = [(fn, k0, k1) for fn in score_fns for k0, k1 in splits]
    s = [fn(k0, k1) for fn, k0, k1 in pieces]
    m = [x.max(axis=0, keepdims=True) for x in s]
    p = [jnp.exp2((x - mi).astype(BF16)) for x, mi in zip(s, m)]
    o = [_dot(vx_ref[:, k0:k1], pi) for (_, k0, k1), pi in zip(pieces, p)]
    outs = []
    n = len(splits)
    for t in range(len(score_fns)):
        mt, ot = m[t * n:(t + 1) * n], o[t * n:(t + 1) * n]
        m_all = functools.reduce(jnp.maximum, mt)
        acc = functools.reduce(lambda a, b: a + b, [oi * jnp.exp2(mi - m_all) for oi, mi in zip(ot, mt)])
        outs.append(acc[0:LANES] / acc[LANES:LANES + 1])
    return outs


def _diffattn_kernel(q_ref, k_ref, vt_ref, bandt_ref, far_ref, lam_ref, sw_ref, o_ref, kx_ref, vx_ref):
    qi = pl.program_id(2)
    tq = LANES
    lp = k_ref.shape[1]
    nt = kx_ref.shape[0]
    nblk = lp // LANES
    c_neg = jnp.broadcast_to(far_ref[0, 0:1, :], (LANES, LANES)).astype(BF16)

    @pl.when(qi == 0)
    def _():
        c_pos = jnp.broadcast_to(far_ref[0, 1:2, :], (lp, LANES)).astype(BF16)
        for t in range(nt):
            kx_ref[t, :, 0:LANES] = k_ref[0]
            kx_ref[t, :, LANES:2 * LANES] = c_pos
        vx_ref[0:LANES, :] = vt_ref[0]
        vx_ref[LANES:LANES + ONES_ROWS, :] = jnp.ones((ONES_ROWS, lp), BF16)

    lane = lax.broadcasted_iota(jnp.int32, (tq, LANES), 1)
    rowi = lax.broadcasted_iota(jnp.int32, (tq, LANES), 0)
    eye = jnp.where(lane == rowi, 1.0, 0.0).astype(BF16)
    score_fns = []
    for t in range(nt):
        tile = qi * nt + t

        def put(kb, val, t=t):
            @pl.when((kb >= 0) & (kb < nblk))
            def _():
                kx_ref[t, pl.ds(pl.multiple_of(kb * LANES, LANES), LANES), LANES:2 * LANES] = val

        for back in range(nt):
            put(tile - 2 - back, c_neg)
        for dd in range(3):
            put(tile - 1 + dd, bandt_ref[0, dd])

        @pl.when(tile <= nt + 1)
        def _(t=t):
            kx_ref[t, 0:N_PAD, LANES:2 * LANES] = jnp.full((N_PAD, LANES), NEG_BIG, BF16)

        q = q_ref[0, t * tq:(t + 1) * tq, :]
        zero = jnp.zeros_like(q)
        lhs = jnp.concatenate([jnp.concatenate([jnp.where(lane < DIFF_D, q, zero), eye], axis=1),
                               jnp.concatenate([jnp.where(lane >= DIFF_D, q, zero), eye], axis=1)], axis=0)
        score_fns.append(lambda k0, k1, t=t, lhs=lhs: _dot_nt(kx_ref[t, k0:k1, :], lhs))
    for t, o2 in enumerate(_attend_t(score_fns, vx_ref, lp)):
        o = (o2[:, 0:tq] - lam_ref[0:1, 0:1] * o2[:, tq:2 * tq]).T
        ms = jnp.mean(o * o, axis=-1, keepdims=True)
        o_ref[0, t * tq:(t + 1) * tq, :] = (o * lax.rsqrt(ms + EPS) * sw_ref[...]).astype(BF16)


DIFF_TILES = 3


def _diffattn(dq, dk, dvt, bandt, far, lam, sw):
    b, lp, _ = dq.shape
    nblk = lp // LANES
    nt = DIFF_TILES if nblk % DIFF_TILES == 0 else 1
    rows = nt * LANES
    return pl.pallas_call(
        _diffattn_kernel,
        grid=(b, DIFF_HEADS, nblk // nt),
        in_specs=[pl.BlockSpec((1, rows, LANES), lambda i, h, j: (i, j, h)),
                  pl.BlockSpec((1, lp, LANES), lambda i, h, j: (i, 0, h)),
                  pl.BlockSpec((1, LANES, lp), lambda i, h, j: (i, h, 0)),
                  pl.BlockSpec((1, 3, LANES, LANES), lambda i, h, j: (h, 0, 0, 0)),
                  pl.BlockSpec((1, 2, LANES), lambda i, h, j: (h, 0, 0)),
                  pl.BlockSpec((1, LANES), lambda i, h, j: (0, 0)),
                  pl.BlockSpec((1, LANES), lambda i, h, j: (0, 0))],
        out_specs=pl.BlockSpec((1, rows, LANES), lambda i, h, j: (i, j, h)),
        out_shape=jax.ShapeDtypeStruct((b, lp, DIFF_HEADS * LANES), BF16),
        scratch_shapes=[pltpu.VMEM((nt, lp, 2 * LANES), BF16), pltpu.VMEM((LANES + ONES_ROWS, lp), BF16)],
        compiler_params=_cparams(("parallel", "parallel", "arbitrary")),
        name="diffattn",
    )(dq, dk, dvt, bandt, far, lam, sw)


def _mlaattn_kernel(q_ref, k_ref, v_ref, o_ref, vx_ref):
    lp = k_ref.shape[1]

    @pl.when(pl.program_id(2) == 0)
    def _():
        vx_ref[:, 0:LANES] = v_ref[0]
        vx_ref[:, LANES:2 * LANES] = jnp.ones((lp, LANES), BF16)

    q = q_ref[0]
    splits = _key_splits(lp, MLA_KEY_PARTS)
    s = [_dot_nt(q, k_ref[0, k0:k1, :]) for k0, k1 in splits]
    m = [x.max(axis=1, keepdims=True) for x in s]
    p = [jnp.exp2((x - mi).astype(BF16)) for x, mi in zip(s, m)]
    o = [_dot(pi, vx_ref[k0:k1, :]) for (k0, k1), pi in zip(splits, p)]
    m_all = functools.reduce(jnp.maximum, m)
    acc = functools.reduce(lambda a, b: a + b, [oi * jnp.exp2(mi - m_all) for oi, mi in zip(o, m)])
    o_ref[0] = (acc[:, 0:LANES] / acc[:, LANES:LANES + 1]).astype(BF16)


MLA_KEY_PARTS = 4
MLA_Q_STEPS = 4


def _mlaattn(mq, mk, mv):
    b, lp, _ = mq.shape
    tq = lp // MLA_Q_STEPS if lp % (16 * MLA_Q_STEPS) == 0 else LANES
    return pl.pallas_call(
        _mlaattn_kernel,
        grid=(b, MLA_HEADS, lp // tq),
        in_specs=[pl.BlockSpec((1, tq, 256), lambda i, h, j: (i, j, h)),
                  pl.BlockSpec((1, lp, 256), lambda i, h, j: (i, 0, h)),
                  pl.BlockSpec((1, lp, LANES), lambda i, h, j: (i, 0, h))],
        out_specs=pl.BlockSpec((1, tq, LANES), lambda i, h, j: (i, j, h)),
        out_shape=jax.ShapeDtypeStruct((b, lp, MLA_HEADS * LANES), BF16),
        scratch_shapes=[pltpu.VMEM((lp, 2 * LANES), BF16)],
        compiler_params=_cparams(("parallel", "parallel", "arbitrary")),
        name="mlaattn",
    )(mq, mk, mv)


def _outproj_kernel(of_ref, ob_ref, z_ref, df_ref, ml_ref, h_ref, dnw_ref, w_ref, o_ref):
    ones = _ones_bf16(LANES)
    parts = []
    for h in range(DN_HEADS):
        sl = slice(h * DN_D, (h + 1) * DN_D)
        o = of_ref[0, :, sl].astype(F32) + ob_ref[0, :, sl].astype(F32)
        ms = _dot((o * o).astype(BF16), ones) * (1.0 / DN_D)
        z = z_ref[0, :, sl].astype(F32)
        parts.append((o * lax.rsqrt(ms + EPS) * dnw_ref[...] * (z * jax.nn.sigmoid(z))).astype(BF16))
    ydn = jnp.concatenate(parts, axis=1)
    acc = h_ref[0] + _dot(ydn, w_ref[0:512, :])
    acc = acc + _dot(df_ref[0], w_ref[512:1024, :])
    acc = acc + _dot(ml_ref[0], w_ref[1024:1536, :])
    o_ref[0] = acc


def _outproj(o_f, o_b, main, ydf, yml, h, dnw, w):
    b, lp, d = h.shape
    tm = _row_tile(lp)

    def rows(width, cb=0):
        return pl.BlockSpec((1, tm, width), lambda i, j: (i, j, cb))

    return pl.pallas_call(
        _outproj_kernel,
        grid=(b, lp // tm),
        in_specs=[rows(512), rows(512), rows(512, 3), rows(512), rows(512), rows(d),
                  pl.BlockSpec((1, LANES), lambda i, j: (0, 0)),
                  pl.BlockSpec(w.shape, lambda i, j: (0, 0))],
        out_specs=rows(d),
        out_shape=jax.ShapeDtypeStruct((b, lp, d), F32),
        compiler_params=_cparams(("parallel", "parallel")),
        name="outproj",
    )(o_f, o_b, main, ydf, yml, h, dnw, w)


def _ffn_kernel(h_ref, nw_ref, wg_ref, wu_ref, wd_ref, o_ref, a_ref):
    x = h_ref[...]
    ms = jnp.mean(x * x, axis=-1, keepdims=True)
    u = (x * lax.rsqrt(ms + EPS) * nw_ref[...]).astype(BF16)
    f = wg_ref.shape[1]
    fc = 256
    for c0 in range(0, f, fc):
        g = _dot(u, wg_ref[:, c0:c0 + fc])
        up = _dot(u, wu_ref[:, c0:c0 + fc])
        a_ref[:, c0:c0 + fc] = (g * jax.nn.sigmoid(g) * up).astype(BF16)
    o_ref[...] = x + _dot(a_ref[...], wd_ref[...])


def _ffn(h2, nw, wg, wu, wd):
    n, d = h2.shape
    f = wg.shape[1]
    tm = 512 if n % 512 == 0 else 128

    def const(a):
        return pl.BlockSpec(a.shape, lambda i: (0, 0), pipeline_mode=pl.Buffered(1))

    return pl.pallas_call(
        _ffn_kernel,
        grid=(n // tm,),
        in_specs=[pl.BlockSpec((tm, d), lambda i: (i, 0)),
                  pl.BlockSpec((1, d), lambda i: (0, 0)),
                  const(wg), const(wu), const(wd)],
        out_specs=pl.BlockSpec((tm, d), lambda i: (i, 0)),
        out_shape=jax.ShapeDtypeStruct((n, d), F32),
        scratch_shapes=[pltpu.VMEM((tm, f), BF16)],
        compiler_params=_cparams(("parallel",)),
        name="ffn",
    )(h2, nw, wg, wu, wd)


def _router_kernel(h_ref, nw_ref, rw_ref, u_ref, idx_ref, gate_ref):
    x = h_ref[...]
    ms = jnp.mean(x * x, axis=-1, keepdims=True)
    u = x * lax.rsqrt(ms + EPS) * nw_ref[...]
    u_ref[...] = u.astype(BF16)
    logits = jnp.dot(u, rw_ref[...], preferred_element_type=F32, precision=lax.Precision.HIGHEST)
    lane = lax.broadcasted_iota(jnp.int32, logits.shape, 1)
    logits = jnp.where(lane < N_EXPERTS, logits, -jnp.inf)
    m1 = jnp.max(logits, axis=-1, keepdims=True)
    i1 = jnp.min(jnp.where(logits == m1, lane, LANES), axis=-1, keepdims=True)
    rest = jnp.where(lane == i1, -jnp.inf, logits)
    m2 = jnp.max(rest, axis=-1, keepdims=True)
    i2 = jnp.min(jnp.where(rest == m2, lane, LANES), axis=-1, keepdims=True)
    e2 = jnp.exp(m2 - m1)
    g1 = 1.0 / (1.0 + e2)
    g2 = e2 / (1.0 + e2)
    idx_ref[...] = jnp.where(lane == 0, i1, jnp.where(lane == 1, i2, 0))
    gate_ref[...] = jnp.where(lane == 0, g1, jnp.where(lane == 1, g2, 0.0))


def _router(h2, nw, rw):
    n, d = h2.shape
    tm = 512 if n % 512 == 0 else 128
    return pl.pallas_call(
        _router_kernel,
        grid=(n // tm,),
        in_specs=[pl.BlockSpec((tm, d), lambda i: (i, 0)),
                  pl.BlockSpec((1, d), lambda i: (0, 0)),
                  pl.BlockSpec((d, LANES), lambda i: (0, 0))],
        out_specs=[pl.BlockSpec((tm, d), lambda i: (i, 0)),
                   pl.BlockSpec((tm, LANES), lambda i: (i, 0)),
                   pl.BlockSpec((tm, LANES), lambda i: (i, 0))],
        out_shape=[jax.ShapeDtypeStruct((n, d), BF16),
                   jax.ShapeDtypeStruct((n, LANES), jnp.int32),
                   jax.ShapeDtypeStruct((n, LANES), F32)],
        compiler_params=_cparams(("parallel",)),
        name="router",
    )(h2, nw, rw)


def _moe_kernel(te_ref, tv_ref, x_ref, wg_ref, wu_ref, wd_ref, o_ref):
    i = pl.program_id(0)
    f = pl.program_id(1)

    @pl.when(f == 0)
    def _():
        o_ref[...] = jnp.zeros(o_ref.shape, F32)

    @pl.when(tv_ref[i] > 0)
    def _():
        x = x_ref[...]
        g = _dot(x, wg_ref[0])
        up = _dot(x, wu_ref[0])
        a = (g * jax.nn.sigmoid(g) * up).astype(BF16)
        o_ref[...] += _dot(a, wd_ref[0])


def _moe(x_sorted, tile_expert, tile_valid, wg, wu, wd, tm):
    p, d = x_sorted.shape
    f = wg.shape[2]
    nf = 2
    fh = f // nf
    grid_spec = pltpu.PrefetchScalarGridSpec(
        num_scalar_prefetch=2,
        grid=(p // tm, nf),
        in_specs=[pl.BlockSpec((tm, d), lambda i, j, te, tv: (i, 0)),
                  pl.BlockSpec((1, d, fh), lambda i, j, te, tv: (te[i], 0, j)),
                  pl.BlockSpec((1, d, fh), lambda i, j, te, tv: (te[i], 0, j)),
                  pl.BlockSpec((1, fh, d), lambda i, j, te, tv: (te[i], j, 0))],
        out_specs=pl.BlockSpec((tm, d), lambda i, j, te, tv: (i, 0)),
    )
    return pl.pallas_call(
        _moe_kernel,
        grid_spec=grid_spec,
        out_shape=jax.ShapeDtypeStruct((p, d), F32),
        compiler_params=_cparams(("arbitrary", "arbitrary")),
        name="moe",
    )(tile_expert, tile_valid, x_sorted, wg, wu, wd)


def _spread_rope_cols(w):
    z = jnp.zeros(w.shape[:-1] + (32,), w.dtype)
    return jnp.concatenate([w[..., 0:32], z, w[..., 32:64], z], axis=-1)


def _relayout_w_in(w):
    dnw = 4 * DN_HEADS * DN_D
    n_gate = 4 * DN_HEADS
    rest = w[:, dnw + n_gate:]
    body = rest[:, :rest.shape[1] - MLA_ROPE]
    kr = _spread_rope_cols(rest[:, rest.shape[1] - MLA_ROPE:])
    gates = jnp.concatenate([w[:, dnw:dnw + n_gate],
                             jnp.zeros((w.shape[0], GATE_COLS - n_gate), w.dtype)], axis=1)
    out = jnp.concatenate([w[:, :dnw], body, kr, gates], axis=1)
    assert out.shape[1] == MAIN_COLS + GATE_COLS
    return out.astype(BF16)


def _relayout_w_uq(w):
    per = MLA_NOPE + MLA_ROPE
    heads = [jnp.concatenate([w[:, h * per:h * per + MLA_NOPE],
                              _spread_rope_cols(w[:, h * per + MLA_NOPE:(h + 1) * per])], axis=1)
             for h in range(MLA_HEADS)]
    return jnp.concatenate(heads, axis=1).astype(BF16)


def _relayout_w_ukv(w):
    per = MLA_NOPE + LANES
    ks = [w[:, h * per:h * per + MLA_NOPE] for h in range(MLA_HEADS)]
    vs = [w[:, h * per + MLA_NOPE:(h + 1) * per] for h in range(MLA_HEADS)]
    return jnp.concatenate(ks + vs, axis=1).astype(BF16)


def _diff_v_weight_t(w):
    c0 = 4 * DN_HEADS * DN_D + 4 * DN_HEADS + 4 * DIFF_HEADS * DIFF_D
    return w[:, c0:c0 + DIFF_HEADS * LANES].T.astype(BF16)


def _qk_norm_vec(w, scale):
    return (jnp.concatenate([w[:MLA_NOPE], _spread_rope_cols(w[MLA_NOPE:])]) * scale)[None, :]


def _t5_bucket(rel):
    half = T5_BUCKETS // 2
    exact = half // 2
    n = jnp.abs(rel)
    large = exact + (jnp.log(jnp.maximum(n, exact).astype(F32) / exact)
                     / math.log(T5_MAX_DIST / exact) * (half - exact)).astype(jnp.int32)
    large = jnp.minimum(large, half - 1)
    return jnp.where(rel > 0, half, 0) + jnp.where(n < exact, n, large)


def _t5_tables(table, tq):
    i = jnp.arange(tq)[None, :]
    j = jnp.arange(LANES)[:, None]
    rels = jnp.stack([j - i + (d * LANES) for d in (-1, 0, 1)])
    table = table.astype(F32) * LOG2E
    bandt = jnp.transpose(table[_t5_bucket(rels)], (3, 0, 1, 2)).astype(BF16)
    far_rel = jnp.array([-T5_MAX_DIST, T5_MAX_DIST], jnp.int32)
    far = jnp.transpose(table[_t5_bucket(far_rel)], (1, 0))
    far = jnp.broadcast_to(far[:, :, None], far.shape + (LANES,))
    return bandt, far


def _rope_tables(lp):
    half = MLA_ROPE // 2
    inv = 1.0 / (ROPE_THETA ** (jnp.arange(half, dtype=F32) / half))
    pos = (jnp.arange(lp, dtype=jnp.int32) - N_PAD).astype(F32)
    ang = pos[:, None] * inv[None, :]
    c, s, z = jnp.cos(ang), jnp.sin(ang), jnp.zeros((lp, half), F32)
    return jnp.concatenate([c, z, c, z], axis=1), jnp.concatenate([-s, z, s, z], axis=1)


def _moe_plan(idx, n, tm):
    e_flat = idx.reshape(-1)
    onehot = (e_flat[:, None] == jnp.arange(N_EXPERTS)[None, :]).astype(jnp.int32)
    counts = jnp.sum(onehot, axis=0)
    rank = jnp.sum((jnp.cumsum(onehot, axis=0) - onehot) * onehot, axis=1)
    tiles_per = (counts + tm - 1) // tm
    tile_end = jnp.cumsum(tiles_per)
    start = (tile_end - tiles_per) * tm
    dest = start[e_flat] + rank
    n_tiles = (2 * n) // tm + N_EXPERTS
    p = n_tiles * tm
    slot_tok = jnp.zeros((p,), jnp.int32).at[dest].set(jnp.arange(2 * n, dtype=jnp.int32) // 2)
    t = jnp.arange(n_tiles, dtype=jnp.int32)
    tile_expert = jnp.minimum(jnp.sum((t[:, None] >= tile_end[None, :]).astype(jnp.int32), axis=1),
                              N_EXPERTS - 1).astype(jnp.int32)
    tile_valid = (t < tile_end[-1]).astype(jnp.int32)
    return slot_tok, dest.reshape(n, 2), tile_expert, tile_valid


def kernel(x, meta_tokens, rel_bias_table, attn_norm_w, w_in, dn_conv_w, dn_a_log, dn_dt_bias, dn_norm_w, diff_q_norm_w, diff_k_norm_w, diff_lambda, diff_subln_w, mla_q_lat_norm_w, mla_w_uq, mla_kv_lat_norm_w, mla_w_ukv, mla_q_norm_w, mla_k_norm_w, w_out, ffn_norm_w, ffn_w_gate, ffn_w_up, ffn_w_down, router_w, moe_w_gate, moe_w_up, moe_w_down):
    bsz, seq, d = x.shape
    depth = w_in.shape[0]
    lp = ROW0 + seq
    n = bsz * lp
    assert lp % LANES == 0 and meta_tokens.shape[0] == N_META

    meta = jnp.broadcast_to(meta_tokens[None].astype(x.dtype), (bsz, N_META, d))
    h = jnp.concatenate([jnp.zeros((bsz, N_PAD, d), x.dtype), meta, x], axis=1)

    cos_t, sin_t = _rope_tables(lp)
    bandt, far = _t5_tables(rel_bias_table, LANES)
    n_chunks = lp // DN_CHUNK

    for l in range(depth):
        main, gate, dvt = _inproj(h, attn_norm_w[l][None, :], _relayout_w_in(w_in[l]),
                                  _diff_v_weight_t(w_in[l]))

        qkv = _dnprep(main, dn_conv_w[l])
        gate_row = jnp.transpose(gate[:, :, 0:16].reshape(bsz, n_chunks, DN_CHUNK, 16), (0, 1, 3, 2))
        neg_a = -jnp.exp(dn_a_log[l].astype(F32)).reshape(-1)
        dtb = dn_dt_bias[l].astype(F32).reshape(-1)
        z8 = jnp.zeros((8,), F32)
        pc = jnp.stack([jnp.concatenate([z8, neg_a, jnp.zeros((GATE_COLS - 16,), F32)]),
                        jnp.concatenate([z8, dtb, jnp.zeros((GATE_COLS - 16,), F32)])])
        pr = jnp.concatenate([jnp.broadcast_to(jnp.concatenate([z8, neg_a])[:, None], (16, DN_CHUNK)),
                              jnp.broadcast_to(jnp.concatenate([z8, dtb])[:, None], (16, DN_CHUNK))], axis=1)
        o_f, o_b = _deltanet(qkv, gate, gate_row, pc, pr)

        dscale = DIFF_D ** -0.5 * LOG2E
        mscale = (MLA_NOPE + MLA_ROPE) ** -0.5 * LOG2E
        dq, dk, mq, mk, mv = _attnprep(
            main,
            (jnp.tile(diff_q_norm_w[l], 2) * dscale)[None, :],
            jnp.tile(diff_k_norm_w[l], 2)[None, :],
            mla_q_lat_norm_w[l][None, :], _relayout_w_uq(mla_w_uq[l]),
            mla_kv_lat_norm_w[l][None, :], _relayout_w_ukv(mla_w_ukv[l]),
            _qk_norm_vec(mla_q_norm_w[l], mscale), _qk_norm_vec(mla_k_norm_w[l], 1.0),
            cos_t, sin_t)
        lam_init = 0.8 - 0.6 * math.exp(-0.3 * l)
        lpar = diff_lambda[l].astype(F32)
        lam = jnp.exp(jnp.sum(lpar[0] * lpar[1])) - jnp.exp(jnp.sum(lpar[2] * lpar[3])) + lam_init
        ydf = _diffattn(dq, dk, dvt, bandt, far, jnp.full((1, LANES), lam, F32),
                        (diff_subln_w[l] * (1.0 - lam_init))[None, :])
        yml = _mlaattn(mq, mk, mv)

        h = _outproj(o_f, o_b, main, ydf, yml, h, dn_norm_w[l][None, :], w_out[l].astype(BF16))

        h2 = h.reshape(n, d)
        if l % 2 == 0:
            i = l // 2
            h2 = _ffn(h2, ffn_norm_w[l][None, :], ffn_w_gate[i].astype(BF16),
                      ffn_w_up[i].astype(BF16), ffn_w_down[i].astype(BF16))
        else:
            i = l // 2
            rw = jnp.concatenate([router_w[i], jnp.zeros((d, LANES - N_EXPERTS), F32)], axis=1)
            u, idx, gates = _router(h2, ffn_norm_w[l][None, :], rw)
            tm = 512 if (2 * n) % 512 == 0 else 128
            slot_tok, tok_slots, tile_expert, tile_valid = _moe_plan(idx[:, 0:2], n, tm)
            x_sorted = jnp.take(u, slot_tok, axis=0)
            y_sorted = _moe(x_sorted, tile_expert, tile_valid,
                            moe_w_gate[i].astype(BF16), moe_w_up[i].astype(BF16),
                            moe_w_down[i].astype(BF16), tm)
            if l == depth - 1:
                real = lambda t: t.reshape((bsz, lp) + t.shape[1:])[:, ROW0:].reshape((bsz * seq,) + t.shape[1:])
                h2, gates, tok_slots = real(h2), real(gates), real(tok_slots)
            h2 = (h2 + gates[:, 0:1] * jnp.take(y_sorted, tok_slots[:, 0], axis=0)
                  + gates[:, 1:2] * jnp.take(y_sorted, tok_slots[:, 1], axis=0))
            if l == depth - 1:
                return h2.reshape(bsz, seq, d)
        h = h2.reshape(bsz, lp, d)
    return h[:, ROW0:]
```

```python
import functools
import math

import jax
import jax.numpy as jnp
from jax import lax
from jax.experimental import pallas as pl
from jax.experimental.pallas import tpu as pltpu

F32 = jnp.float32
BF16 = jnp.bfloat16

N_META = 16
N_PAD = 112
ROW0 = N_PAD + N_META
DN_HEADS = 4
DN_D = 128
DN_CONV = 5
DN_CHUNK = 64
DIFF_HEADS = 4
DIFF_D = 64
MLA_HEADS = 4
MLA_NOPE = 128
MLA_ROPE = 64
MLA_Q_RANK = 256
MLA_KV_RANK = 128
ROPE_THETA = 10000.0
T5_BUCKETS = 32
T5_MAX_DIST = 128
N_EXPERTS = 8
EPS = 1e-6
NEG_BIG = -1e30
LOG2E = 1.4426950408889634
MASK_LANE = 32
LANES = 128
VMEM_LIMIT = 56 * 1024 * 1024

MAIN_COLS = 4096
GATE_COLS = 128


def _dot(a, b):
    return jnp.dot(a, b, preferred_element_type=F32)


def _dot_nt(a, b):
    return lax.dot_general(a, b, (((1,), (1,)), ((), ())), preferred_element_type=F32)


def _ones_bf16(n):
    return jnp.ones((n, n), BF16)


def _row_tile(lp):
    return 384 if lp % 384 == 0 else 128


def _cparams(sem):
    return pltpu.CompilerParams(dimension_semantics=sem, vmem_limit_bytes=VMEM_LIMIT)


CAST_BLOCK_ELEMS = 1 << 20


def _cast_kernel(x_ref, o_ref):
    o_ref[...] = x_ref[...].astype(o_ref.dtype)


def _to_bf16(w):
    c = w.shape[-1]
    w2 = w.reshape(-1, c)
    r = w2.shape[0]
    tr = 1 << max(4, (CAST_BLOCK_ELEMS // c).bit_length() - 1)
    while r % tr:
        tr //= 2
    assert tr >= 16
    out = pl.pallas_call(
        _cast_kernel,
        grid=(r // tr,),
        in_specs=[pl.BlockSpec((tr, c), lambda i: (i, 0))],
        out_specs=pl.BlockSpec((tr, c), lambda i: (i, 0)),
        out_shape=jax.ShapeDtypeStruct((r, c), BF16),
        compiler_params=_cparams(("parallel",)),
        name="cast_bf16",
    )(w2)
    return out.reshape(w.shape)


def _inproj_kernel(h_ref, nw_ref, w_ref, wvt_ref, main_ref, gate_ref, vt_ref):
    x = h_ref[0]
    ms = jnp.mean(x * x, axis=-1, keepdims=True)
    u = (x * lax.rsqrt(ms + EPS) * nw_ref[...]).astype(BF16)
    for c0 in range(0, MAIN_COLS, 512):
        main_ref[0, :, c0:c0 + 512] = _dot(u, w_ref[:, c0:c0 + 512]).astype(BF16)
    gate_ref[0] = _dot(u, w_ref[:, MAIN_COLS:MAIN_COLS + GATE_COLS])
    vt_ref[0] = _dot_nt(wvt_ref[...], u).astype(BF16)


def _inproj(h, norm_w, w, wvt):
    b, lp, d = h.shape
    tm = _row_tile(lp)
    ncol = MAIN_COLS + GATE_COLS
    nv = wvt.shape[0]
    return pl.pallas_call(
        _inproj_kernel,
        grid=(b, lp // tm),
        in_specs=[
            pl.BlockSpec((1, tm, d), lambda i, j: (i, j, 0)),
            pl.BlockSpec((1, d), lambda i, j: (0, 0)),
            pl.BlockSpec((d, ncol), lambda i, j: (0, 0)),
            pl.BlockSpec((nv, d), lambda i, j: (0, 0)),
        ],
        out_specs=[
            pl.BlockSpec((1, tm, MAIN_COLS), lambda i, j: (i, j, 0)),
            pl.BlockSpec((1, tm, GATE_COLS), lambda i, j: (i, j, 0)),
            pl.BlockSpec((1, nv, tm), lambda i, j: (i, 0, j)),
        ],
        out_shape=[
            jax.ShapeDtypeStruct((b, lp, MAIN_COLS), BF16),
            jax.ShapeDtypeStruct((b, lp, GATE_COLS), F32),
            jax.ShapeDtypeStruct((b, nv, lp), BF16),
        ],
        compiler_params=_cparams(("parallel", "parallel")),
        name="inproj",
    )(h, norm_w, w, wvt)


def _dnprep_kernel(x_ref, cw_ref, o_ref, xs_ref):
    j = pl.program_id(1)
    lp = x_ref.shape[1]
    halo = 8
    xs_ref[0:halo + N_PAD, :] = jnp.zeros((halo + N_PAD, LANES), F32)
    xs_ref[halo + N_PAD:halo + lp, :] = x_ref[0, N_PAD:, :].astype(F32)
    xs_ref[halo + lp:2 * halo + lp, :] = jnp.zeros((halo, LANES), F32)
    q_scale = jnp.where(j < DN_HEADS, DN_D ** -0.5, 1.0).astype(F32)
    is_qk = j < 2 * DN_HEADS
    rc = 128
    for c in range(lp // rc):
        base = c * rc
        acc = cw_ref[0:1, :] * xs_ref[base + halo - 2:base + halo - 2 + rc, :]
        for t in range(1, DN_CONV):
            off = base + halo - 2 + t
            acc = acc + cw_ref[t:t + 1, :] * xs_ref[off:off + rc, :]
        y = acc * jax.nn.sigmoid(acc)
        ss = jnp.sum(y * y, axis=-1, keepdims=True)
        r = jnp.where(is_qk, lax.rsqrt(ss + EPS) * q_scale, 1.0)
        o_ref[0, base:base + rc, :] = (y * r).astype(BF16)


def _dnprep(main, conv_w):
    b, lp, _ = main.shape
    ncb = 3 * DN_HEADS
    return pl.pallas_call(
        _dnprep_kernel,
        grid=(b, ncb),
        in_specs=[
            pl.BlockSpec((1, lp, LANES), lambda i, j: (i, 0, j)),
            pl.BlockSpec((DN_CONV, LANES), lambda i, j: (0, j)),
        ],
        out_specs=pl.BlockSpec((1, lp, LANES), lambda i, j: (i, 0, j)),
        out_shape=jax.ShapeDtypeStruct((b, lp, ncb * LANES), BF16),
        scratch_shapes=[pltpu.VMEM((lp + 16, LANES), F32)],
        compiler_params=_cparams(("parallel", "parallel")),
        name="dnprep",
    )(main, conv_w)


def _each(fn, *lists):
    return [fn(*args) for args in zip(*lists)]


def _unit_tri_inverse(a_list, eye, blk16, blk32):
    bf = lambda t: t.astype(BF16)
    a0 = _each(lambda a: jnp.where(blk16, a, 0.0), a_list)
    a0b = _each(bf, a0)
    a2b = _each(lambda x: bf(_dot(x, x)), a0b)
    a4b = _each(lambda x: bf(_dot(x, x)), a2b)
    a8b = _each(lambda x: bf(_dot(x, x)), a4b)
    p = _each(lambda x: eye - x, a0)
    for pw in (a2b, a4b, a8b):
        p = _each(lambda x, y: x + _dot(bf(x), y), p, pw)
    off1 = _each(lambda a: bf(jnp.where(blk32 & (~blk16), a, 0.0)), a_list)
    off2 = _each(lambda a: bf(jnp.where(blk32, 0.0, a)), a_list)
    for off in (off1, off2):
        pb = _each(bf, p)
        mid = _each(lambda x, y: bf(_dot(x, y)), pb, off)
        p = _each(lambda x, m, xb: x - _dot(m, xb), p, mid, pb)
    return p


def _dn_kernel(qf_ref, kf_ref, vf_ref, qb_ref, kb_ref, vb_ref, gcf_ref, gcb_ref,
               grf_ref, grb_ref, pc_ref, pr_ref, of_ref, ob_ref, st_ref):
    s = pl.program_id(1)
    c = DN_CHUNK
    nb = qf_ref.shape[0]

    @pl.when(s == 0)
    def _():
        st_ref[...] = jnp.zeros(st_ref.shape, F32)

    row = lax.broadcasted_iota(jnp.int32, (c, c), 0)
    col = lax.broadcasted_iota(jnp.int32, (c, c), 1)
    eye = (row == col).astype(F32)
    blk16 = (row // 16) == (col // 16)
    blk32 = (row // 32) == (col // 32)
    negA_c, dtb_c = pc_ref[0:1, :], pc_ref[1:2, :]
    negA_r, dtb_r = pr_ref[:, 0:c], pr_ref[:, c:2 * c]
    bf = lambda t: t.astype(BF16)

    per_dir = []
    for d, (gc_ref, gr_ref) in enumerate(((gcf_ref, grf_ref), (gcb_ref, grb_ref))):
        if d == 0:
            causal = s >= 0
            chunk = s
        else:
            causal = s <= 1
            chunk = jnp.where(s <= 1, s, (pl.num_programs(1) + 1) - s)
        sgn = jnp.where(causal, 1, -1)
        m_in = (row - col) * sgn >= 0
        m_strict = (row - col) * sgn > 0
        m_f = m_in.astype(F32)
        m_t = ((col - row) * sgn >= 0).astype(F32)
        pos_c = chunk * c + lax.broadcasted_iota(jnp.int32, (c, LANES), 0)
        live_c = pos_c >= N_PAD
        pos_r = chunk * c + lax.broadcasted_iota(jnp.int32, (16, c), 1)
        gates = []
        for bi in range(nb):
            gcol = gc_ref[bi]
            beta_c = jnp.where(live_c, jax.nn.sigmoid(gcol), 0.0)
            g_c = jnp.where(live_c, negA_c * jax.nn.softplus(gcol + dtb_c), 0.0)
            grow = gr_ref[bi, 0]
            g_r = jnp.where(pos_r >= N_PAD, negA_r * jax.nn.softplus(grow + dtb_r), 0.0)
            gc_c = jnp.dot(m_f, g_c, preferred_element_type=F32, precision=lax.Precision.HIGHEST)
            gc_r = jnp.dot(g_r, m_t, preferred_element_type=F32, precision=lax.Precision.HIGHEST)
            gtot_c = jnp.sum(g_c, axis=0, keepdims=True)
            gates.append(dict(beta_c=beta_c, gc_c=gc_c, gc_r=gc_r, eg_c=jnp.exp(gc_c),
                              ew_c=jnp.exp(gtot_c - gc_c), et_c=jnp.exp(gtot_c)))
        per_dir.append(dict(m_in=m_in, m_strict=m_strict, gates=gates))

    refs = ((qf_ref, kf_ref, vf_ref, of_ref), (qb_ref, kb_ref, vb_ref, ob_ref))
    chains = [(bi, d, h) for bi in range(nb) for d in range(2) for h in range(DN_HEADS)]
    sl = lambda h: slice(h * DN_D, (h + 1) * DN_D)
    bcol = lambda d, h: d * DN_HEADS + h
    acol = lambda d, h: 2 * DN_HEADS + d * DN_HEADS + h
    sidx = lambda bi, d, h: (bi * 2 + d) * DN_HEADS + h
    gate = lambda bi, d, name: per_dir[d]["gates"][bi][name]

    q = [refs[d][0][bi, :, sl(h)] for bi, d, h in chains]
    k = [refs[d][1][bi, :, sl(h)] for bi, d, h in chains]
    v = [refs[d][2][bi, :, sl(h)] for bi, d, h in chains]
    kf = _each(lambda t: t.astype(F32), k)
    beta = [gate(bi, d, "beta_c")[:, bcol(d, h):bcol(d, h) + 1] for bi, d, h in chains]
    eg = [gate(bi, d, "eg_c")[:, acol(d, h):acol(d, h) + 1] for bi, d, h in chains]
    kb = _each(lambda x, b: x * b, kf, beta)
    vb = _each(lambda x, b: x.astype(F32) * b, v, beta)

    def decay_of(bi, d, h):
        m_in = per_dir[d]["m_in"]
        diff = gate(bi, d, "gc_c")[:, acol(d, h):acol(d, h) + 1] - gate(bi, d, "gc_r")[acol(d, h):acol(d, h) + 1, :]
        return jnp.where(m_in, jnp.exp(jnp.where(m_in, diff, 0.0)), 0.0)

    decay = [decay_of(*ch) for ch in chains]
    kq = _each(lambda x, y, z: _dot_nt(jnp.concatenate([bf(x), y], axis=0), z), kb, q, k)
    a = [jnp.where(per_dir[d]["m_strict"], kq_i[0:c] * dec, 0.0)
         for (bi, d, h), kq_i, dec in zip(chains, kq, decay)]
    qk = _each(lambda x, dec: bf(x[c:2 * c] * dec), kq, decay)
    t = _unit_tri_inverse(a, eye, blk16, blk32)
    x = _each(lambda x1, x2, e: bf(jnp.concatenate([x1, x2 * e], axis=1)), vb, kb, eg)
    tx = _each(lambda ti, xi: _dot(bf(ti), xi), t, x)
    st = [st_ref[sidx(*ch)] for ch in chains]
    lhs = _each(lambda txi, qi, e: bf(jnp.concatenate([txi[:, DN_D:2 * DN_D], qi.astype(F32) * e], axis=0)),
                tx, q, eg)
    r = _each(lambda l, si: _dot(l, bf(si)), lhs, st)
    v_new = _each(lambda txi, ri: bf(txi[:, 0:DN_D] - ri[0:c]), tx, r)
    out = _each(lambda ri, qki, vn: ri[c:2 * c] + _dot(qki, vn), r, qk, v_new)
    kw = [bf((kfi * gate(bi, d, "ew_c")[:, acol(d, h):acol(d, h) + 1]).T) for (bi, d, h), kfi in zip(chains, kf)]
    upd = _each(_dot, kw, v_new)
    for (bi, d, h), o_i, st_i, u_i in zip(chains, out, st, upd):
        refs[d][3][bi, :, sl(h)] = o_i.astype(BF16)
        st_ref[sidx(bi, d, h)] = st_i * gate(bi, d, "et_c")[:, acol(d, h):acol(d, h) + 1] + u_i


DN_BATCH = 4


def _deltanet(qkv, gate_col, gate_row, pc, pr):
    b, lp, _ = qkv.shape
    n = lp // DN_CHUNK
    hw = DN_HEADS * DN_D
    nb = DN_BATCH if b % DN_BATCH == 0 else 1

    def fwd_chunk(s):
        return s

    def bwd_chunk(s):
        return jnp.where(s <= 1, s, n + 1 - s)

    def spec3(cb, chunk):
        return pl.BlockSpec((nb, DN_CHUNK, hw), lambda i, s: (i, chunk(s), cb))

    in_specs = [spec3(0, fwd_chunk), spec3(1, fwd_chunk), spec3(2, fwd_chunk),
                spec3(0, bwd_chunk), spec3(1, bwd_chunk), spec3(2, bwd_chunk),
                pl.BlockSpec((nb, DN_CHUNK, GATE_COLS), lambda i, s: (i, fwd_chunk(s), 0)),
                pl.BlockSpec((nb, DN_CHUNK, GATE_COLS), lambda i, s: (i, bwd_chunk(s), 0)),
                pl.BlockSpec((nb, 1, 16, DN_CHUNK), lambda i, s: (i, fwd_chunk(s), 0, 0)),
                pl.BlockSpec((nb, 1, 16, DN_CHUNK), lambda i, s: (i, bwd_chunk(s), 0, 0)),
                pl.BlockSpec((2, GATE_COLS), lambda i, s: (0, 0)),
                pl.BlockSpec((16, 2 * DN_CHUNK), lambda i, s: (0, 0))]
    out_specs = [pl.BlockSpec((nb, DN_CHUNK, hw), lambda i, s: (i, fwd_chunk(s), 0)),
                 pl.BlockSpec((nb, DN_CHUNK, hw), lambda i, s: (i, bwd_chunk(s), 0))]
    return pl.pallas_call(
        _dn_kernel,
        grid=(b // nb, n),
        in_specs=in_specs,
        out_specs=out_specs,
        out_shape=[jax.ShapeDtypeStruct((b, lp, hw), BF16)] * 2,
        scratch_shapes=[pltpu.VMEM((nb * 2 * DN_HEADS, DN_D, DN_D), F32)],
        compiler_params=_cparams(("parallel", "arbitrary")),
        name="deltanet",
    )(qkv, qkv, qkv, qkv, qkv, qkv, gate_col, gate_col, gate_row, gate_row, pc, pr)


def _attnprep_kernel(x_ref, dqw_ref, dkw_ref, qlw_ref, wuq_ref, klw_ref, wukv_ref,
                     mqw_ref, mkw_ref, cos_ref, sin_ref,
                     dq_ref, dk_ref, mq_ref, mk_ref, mv_ref):
    ones = _ones_bf16(LANES)
    row = lax.broadcasted_iota(jnp.int32, (LANES, LANES), 0)
    col = lax.broadcasted_iota(jnp.int32, (LANES, LANES), 1)
    half_ones = ((row // DIFF_D) == (col // DIFF_D)).astype(BF16)
    cos_t, sin_t = cos_ref[...], sin_ref[...]
    tm = x_ref.shape[1]
    lane_t = lax.broadcasted_iota(jnp.int32, (tm, LANES), 1)
    row_t = pl.program_id(1) * tm + lax.broadcasted_iota(jnp.int32, (tm, LANES), 0)
    q_flag = jnp.where(lane_t == MASK_LANE, 1.0, 0.0)
    k_flag = jnp.where((lane_t == MASK_LANE) & (row_t < N_PAD), NEG_BIG, 0.0)

    def rope(t):
        return t * cos_t + pltpu.roll(t, 64, 1) * sin_t

    for src, w_ref, dst in ((0, dqw_ref, dq_ref), (512, dkw_ref, dk_ref)):
        for h in range(DIFF_HEADS):
            y = x_ref[0, :, src + h * LANES:src + (h + 1) * LANES].astype(F32)
            ms = _dot((y * y).astype(BF16), half_ones) * (1.0 / DIFF_D)
            dst[0, :, h * LANES:(h + 1) * LANES] = (y * lax.rsqrt(ms + EPS) * w_ref[...]).astype(BF16)

    cq0 = x_ref[0, :, 1536:1664].astype(F32)
    cq1 = x_ref[0, :, 1664:1792].astype(F32)
    ms = _dot((cq0 * cq0 + cq1 * cq1).astype(BF16), ones) * (1.0 / MLA_Q_RANK)
    r = lax.rsqrt(ms + EPS)
    cqn = jnp.concatenate([cq0 * r * qlw_ref[:, 0:LANES], cq1 * r * qlw_ref[:, LANES:2 * LANES]],
                          axis=1).astype(BF16)
    q = _dot(cqn, wuq_ref[...])
    inv_d = 1.0 / (MLA_NOPE + MLA_ROPE)
    for h in range(MLA_HEADS):
        q0 = q[:, 256 * h:256 * h + LANES]
        q1 = q[:, 256 * h + LANES:256 * (h + 1)]
        ms = _dot((q0 * q0 + q1 * q1).astype(BF16), ones) * inv_d
        r = lax.rsqrt(ms + EPS)
        mq_ref[0, :, 256 * h:256 * h + LANES] = (q0 * r * mqw_ref[:, 0:LANES]).astype(BF16)
        mq_ref[0, :, 256 * h + LANES:256 * (h + 1)] = (
            rope(q1 * r * mqw_ref[:, LANES:2 * LANES]) + q_flag).astype(BF16)

    ckv = x_ref[0, :, 1792:1920].astype(F32)
    ms = _dot((ckv * ckv).astype(BF16), ones) * (1.0 / MLA_KV_RANK)
    ckvn = (ckv * lax.rsqrt(ms + EPS) * klw_ref[...]).astype(BF16)
    kv = _dot(ckvn, wukv_ref[...])
    mv_ref[0] = kv[:, 512:1024].astype(BF16)
    kr = x_ref[0, :, 1920:2048].astype(F32)
    kr2 = kr * kr
    for h in range(MLA_HEADS):
        k0 = kv[:, LANES * h:LANES * (h + 1)]
        ms = _dot((k0 * k0 + kr2).astype(BF16), ones) * inv_d
        r = lax.rsqrt(ms + EPS)
        mk_ref[0, :, 256 * h:256 * h + LANES] = (k0 * r * mkw_ref[:, 0:LANES]).astype(BF16)
        mk_ref[0, :, 256 * h + LANES:256 * (h + 1)] = (
            rope(kr * r * mkw_ref[:, LANES:2 * LANES]) + k_flag).astype(BF16)


def _attnprep(main, dqw, dkw, qlw, wuq, klw, wukv, mqw, mkw, cos_t, sin_t):
    b, lp, _ = main.shape
    tm = _row_tile(lp)

    def full(a):
        return pl.BlockSpec(a.shape, lambda i, j: (0,) * a.ndim)

    def rows(width):
        return pl.BlockSpec((1, tm, width), lambda i, j: (i, j, 0))

    return pl.pallas_call(
        _attnprep_kernel,
        grid=(b, lp // tm),
        in_specs=[pl.BlockSpec((1, tm, 2048), lambda i, j: (i, j, 1)),
                  full(dqw), full(dkw), full(qlw), full(wuq), full(klw), full(wukv),
                  full(mqw), full(mkw),
                  pl.BlockSpec((tm, LANES), lambda i, j: (j, 0)),
                  pl.BlockSpec((tm, LANES), lambda i, j: (j, 0))],
        out_specs=[rows(512), rows(512), rows(1024), rows(1024), rows(512)],
        out_shape=[jax.ShapeDtypeStruct((b, lp, 512), BF16),
                   jax.ShapeDtypeStruct((b, lp, 512), BF16),
                   jax.ShapeDtypeStruct((b, lp, 1024), BF16),
                   jax.ShapeDtypeStruct((b, lp, 1024), BF16),
                   jax.ShapeDtypeStruct((b, lp, 512), BF16)],
        compiler_params=_cparams(("parallel", "parallel")),
        name="attnprep",
    )(main, dqw, dkw, qlw, wuq, klw, wukv, mqw, mkw, cos_t, sin_t)


KEY_PARTS = 3


def _key_splits(lp, parts=None):
    nblk = lp // LANES
    nparts = min(parts or KEY_PARTS, nblk)
    edges = [(nblk * i // nparts) * LANES for i in range(nparts + 1)]
    return tuple(zip(edges[:-1], edges[1:]))


def _diff_attend_t(score_fns, vt_ref, lam, lp):
    splits = _key_splits(lp)
    tq = LANES
    add = lambda x, y: x + y
    s = [[fn(k0, k1) for k0, k1 in splits] for fn in score_fns]
    m = [functools.reduce(jnp.maximum, [x.max(axis=0, keepdims=True) for x in st]) for st in s]
    p = [[jnp.exp2(x - mt) for x in st] for st, mt in zip(s, m)]
    l = [functools.reduce(add, [x.sum(axis=0, keepdims=True) for x in pt]) for pt in p]
    a, inv_l1 = [], []
    for pt, lt in zip(p, l):
        l1, l2 = lt[:, 0:tq], lt[:, tq:2 * tq]
        r = lam * l1 / l2
        inv_l1.append(1.0 / l1)
        a.append([(x[:, 0:tq] - x[:, tq:2 * tq] * r).astype(BF16) for x in pt])
    outs = []
    for g0 in range(0, len(a), 2):
        grp = a[g0:g0 + 2]
        acc = None
        for i, (k0, k1) in enumerate(splits):
            w = grp[0][i] if len(grp) == 1 else jnp.concatenate([grp[0][i], grp[1][i]], axis=1)
            part = _dot(vt_ref[0, :, k0:k1], w)
            acc = part if acc is None else acc + part
        outs.extend(acc[:, j * tq:(j + 1) * tq] * inv_l1[g0 + j] for j in range(len(grp)))
    return outs


def _diffattn_kernel(q_ref, k_ref, vt_ref, bandt_ref, far_ref, lam_ref, sw_ref, o_ref, kx_ref):
    qi = pl.program_id(2)
    tq = LANES
    lp = k_ref.shape[1]
    nt = kx_ref.shape[0]
    nblk = lp // LANES
    c_neg = jnp.broadcast_to(far_ref[0, 0:1, :], (LANES, LANES)).astype(BF16)

    @pl.when(qi == 0)
    def _():
        c_pos = jnp.broadcast_to(far_ref[0, 1:2, :], (lp, LANES)).astype(BF16)
        for t in range(nt):
            kx_ref[t, :, 0:LANES] = k_ref[0]
            kx_ref[t, :, LANES:2 * LANES] = c_pos

    lane = lax.broadcasted_iota(jnp.int32, (tq, LANES), 1)
    rowi = lax.broadcasted_iota(jnp.int32, (tq, LANES), 0)
    eye = jnp.where(lane == rowi, 1.0, 0.0).astype(BF16)
    score_fns = []
    for t in range(nt):
        tile = qi * nt + t

        def put(kb, val, t=t):
            @pl.when((kb >= 0) & (kb < nblk))
            def _():
                kx_ref[t, pl.ds(pl.multiple_of(kb * LANES, LANES), LANES), LANES:2 * LANES] = val

        for back in range(nt):
            put(tile - 2 - back, c_neg)
        for dd in range(3):
            put(tile - 1 + dd, bandt_ref[0, dd])

        @pl.when(tile <= nt + 1)
        def _(t=t):
            kx_ref[t, 0:N_PAD, LANES:2 * LANES] = jnp.full((N_PAD, LANES), NEG_BIG, BF16)

        q = q_ref[0, t * tq:(t + 1) * tq, :]
        zero = jnp.zeros_like(q)
        lhs = jnp.concatenate([jnp.concatenate([jnp.where(lane < DIFF_D, q, zero), eye], axis=1),
                               jnp.concatenate([jnp.where(lane >= DIFF_D, q, zero), eye], axis=1)], axis=0)
        score_fns.append(lambda k0, k1, t=t, lhs=lhs: _dot_nt(kx_ref[t, k0:k1, :], lhs))
    for t, ot in enumerate(_diff_attend_t(score_fns, vt_ref, lam_ref[0:1, 0:1], lp)):
        o = ot.T
        ms = jnp.mean(o * o, axis=-1, keepdims=True)
        o_ref[0, t * tq:(t + 1) * tq, :] = (o * lax.rsqrt(ms + EPS) * sw_ref[...]).astype(BF16)


DIFF_TILES = 3


def _diffattn(dq, dk, dvt, bandt, far, lam, sw):
    b, lp, _ = dq.shape
    nblk = lp // LANES
    nt = DIFF_TILES
    rows = nt * LANES
    return pl.pallas_call(
        _diffattn_kernel,
        grid=(b, DIFF_HEADS, pl.cdiv(nblk, nt)),
        in_specs=[pl.BlockSpec((1, rows, LANES), lambda i, h, j: (i, j, h)),
                  pl.BlockSpec((1, lp, LANES), lambda i, h, j: (i, 0, h)),
                  pl.BlockSpec((1, LANES, lp), lambda i, h, j: (i, h, 0)),
                  pl.BlockSpec((1, 3, LANES, LANES), lambda i, h, j: (h, 0, 0, 0)),
                  pl.BlockSpec((1, 2, LANES), lambda i, h, j: (h, 0, 0)),
                  pl.BlockSpec((1, LANES), lambda i, h, j: (0, 0)),
                  pl.BlockSpec((1, LANES), lambda i, h, j: (0, 0))],
        out_specs=pl.BlockSpec((1, rows, LANES), lambda i, h, j: (i, j, h)),
        out_shape=jax.ShapeDtypeStruct((b, lp, DIFF_HEADS * LANES), BF16),
        scratch_shapes=[pltpu.VMEM((nt, lp, 2 * LANES), BF16)],
        compiler_params=_cparams(("parallel", "parallel", "arbitrary")),
        name="diffattn",
    )(dq, dk, dvt, bandt, far, lam, sw)


def _mlaattn_kernel(q_ref, k_ref, v_ref, o_ref, vx_ref):
    lp = k_ref.shape[1]

    @pl.when(pl.program_id(2) == 0)
    def _():
        vx_ref[:, 0:LANES] = v_ref[0]
        vx_ref[:, LANES:2 * LANES] = jnp.ones((lp, LANES), BF16)

    q = q_ref[0]
    splits = _key_splits(lp, MLA_KEY_PARTS)
    s = [_dot_nt(q, k_ref[0, k0:k1, :]) for k0, k1 in splits]
    m = [x.max(axis=1, keepdims=True) for x in s]
    p = [jnp.exp2((x - mi).astype(BF16)) for x, mi in zip(s, m)]
    o = [_dot(pi, vx_ref[k0:k1, :]) for (k0, k1), pi in zip(splits, p)]
    m_all = functools.reduce(jnp.maximum, m)
    acc = functools.reduce(lambda a, b: a + b, [oi * jnp.exp2(mi - m_all) for oi, mi in zip(o, m)])
    o_ref[0] = (acc[:, 0:LANES] / acc[:, LANES:LANES + 1]).astype(BF16)


MLA_KEY_PARTS = 4
MLA_Q_STEPS = 4


def _mlaattn(mq, mk, mv):
    b, lp, _ = mq.shape
    tq = lp // MLA_Q_STEPS if lp % (16 * MLA_Q_STEPS) == 0 else LANES
    return pl.pallas_call(
        _mlaattn_kernel,
        grid=(b, MLA_HEADS, lp // tq),
        in_specs=[pl.BlockSpec((1, tq, 256), lambda i, h, j: (i, j, h)),
                  pl.BlockSpec((1, lp, 256), lambda i, h, j: (i, 0, h)),
                  pl.BlockSpec((1, lp, LANES), lambda i, h, j: (i, 0, h))],
        out_specs=pl.BlockSpec((1, tq, LANES), lambda i, h, j: (i, j, h)),
        out_shape=jax.ShapeDtypeStruct((b, lp, MLA_HEADS * LANES), BF16),
        scratch_shapes=[pltpu.VMEM((lp, 2 * LANES), BF16)],
        compiler_params=_cparams(("parallel", "parallel", "arbitrary")),
        name="mlaattn",
    )(mq, mk, mv)


def _outproj_kernel(of_ref, ob_ref, z_ref, df_ref, ml_ref, h_ref, dnw_ref, w_ref, o_ref):
    ones = _ones_bf16(LANES)
    parts = []
    for h in range(DN_HEADS):
        sl = slice(h * DN_D, (h + 1) * DN_D)
        o = of_ref[0, :, sl].astype(F32) + ob_ref[0, :, sl].astype(F32)
        ms = _dot((o * o).astype(BF16), ones) * (1.0 / DN_D)
        z = z_ref[0, :, sl].astype(F32)
        parts.append((o * lax.rsqrt(ms + EPS) * dnw_ref[...] * (z * jax.nn.sigmoid(z))).astype(BF16))
    ydn = jnp.concatenate(parts, axis=1)
    acc = h_ref[0] + _dot(ydn, w_ref[0:512, :])
    acc = acc + _dot(df_ref[0], w_ref[512:1024, :])
    acc = acc + _dot(ml_ref[0], w_ref[1024:1536, :])
    o_ref[0] = acc


def _outproj(o_f, o_b, main, ydf, yml, h, dnw, w):
    b, lp, d = h.shape
    tm = _row_tile(lp)

    def rows(width, cb=0):
        return pl.BlockSpec((1, tm, width), lambda i, j: (i, j, cb))

    return pl.pallas_call(
        _outproj_kernel,
        grid=(b, lp // tm),
        in_specs=[rows(512), rows(512), rows(512, 3), rows(512), rows(512), rows(d),
                  pl.BlockSpec((1, LANES), lambda i, j: (0, 0)),
                  pl.BlockSpec(w.shape, lambda i, j: (0, 0))],
        out_specs=rows(d),
        out_shape=jax.ShapeDtypeStruct((b, lp, d), F32),
        compiler_params=_cparams(("parallel", "parallel")),
        name="outproj",
    )(o_f, o_b, main, ydf, yml, h, dnw, w)


def _ffn_kernel(h_ref, nw_ref, wg_ref, wu_ref, wd_ref, o_ref, a_ref):
    x = h_ref[...]
    ms = jnp.mean(x * x, axis=-1, keepdims=True)
    u = (x * lax.rsqrt(ms + EPS) * nw_ref[...]).astype(BF16)
    f = wg_ref.shape[1]
    fc = 256
    for c0 in range(0, f, fc):
        g = _dot(u, wg_ref[:, c0:c0 + fc])
        up = _dot(u, wu_ref[:, c0:c0 + fc])
        a_ref[:, c0:c0 + fc] = (g * jax.nn.sigmoid(g) * up).astype(BF16)
    o_ref[...] = x + _dot(a_ref[...], wd_ref[...])


def _ffn(h2, nw, wg, wu, wd):
    n, d = h2.shape
    f = wg.shape[1]
    tm = 512 if n % 512 == 0 else 128

    def const(a):
        return pl.BlockSpec(a.shape, lambda i: (0, 0), pipeline_mode=pl.Buffered(1))

    return pl.pallas_call(
        _ffn_kernel,
        grid=(n // tm,),
        in_specs=[pl.BlockSpec((tm, d), lambda i: (i, 0)),
                  pl.BlockSpec((1, d), lambda i: (0, 0)),
                  const(wg), const(wu), const(wd)],
        out_specs=pl.BlockSpec((tm, d), lambda i: (i, 0)),
        out_shape=jax.ShapeDtypeStruct((n, d), F32),
        scratch_shapes=[pltpu.VMEM((tm, f), BF16)],
        compiler_params=_cparams(("parallel",)),
        name="ffn",
    )(h2, nw, wg, wu, wd)


def _router_kernel(h_ref, nw_ref, rw_ref, u_ref, idx_ref, gate_ref):
    x = h_ref[...]
    ms = jnp.mean(x * x, axis=-1, keepdims=True)
    u = x * lax.rsqrt(ms + EPS) * nw_ref[...]
    u_ref[...] = u.astype(BF16)
    logits = jnp.dot(u, rw_ref[...], preferred_element_type=F32, precision=lax.Precision.HIGHEST)
    lane = lax.broadcasted_iota(jnp.int32, logits.shape, 1)
    logits = jnp.where(lane < N_EXPERTS, logits, -jnp.inf)
    m1 = jnp.max(logits, axis=-1, keepdims=True)
    i1 = jnp.min(jnp.where(logits == m1, lane, LANES), axis=-1, keepdims=True)
    rest = jnp.where(lane == i1, -jnp.inf, logits)
    m2 = jnp.max(rest, axis=-1, keepdims=True)
    i2 = jnp.min(jnp.where(rest == m2, lane, LANES), axis=-1, keepdims=True)
    e2 = jnp.exp(m2 - m1)
    g1 = 1.0 / (1.0 + e2)
    g2 = e2 / (1.0 + e2)
    idx_ref[...] = jnp.where(lane == 0, i1, jnp.where(lane == 1, i2, 0))
    gate_ref[...] = jnp.where(lane == 0, g1, jnp.where(lane == 1, g2, 0.0))


def _router(h2, nw, rw):
    n, d = h2.shape
    tm = 512 if n % 512 == 0 else 128
    return pl.pallas_call(
        _router_kernel,
        grid=(n // tm,),
        in_specs=[pl.BlockSpec((tm, d), lambda i: (i, 0)),
                  pl.BlockSpec((1, d), lambda i: (0, 0)),
                  pl.BlockSpec((d, LANES), lambda i: (0, 0))],
        out_specs=[pl.BlockSpec((tm, d), lambda i: (i, 0)),
                   pl.BlockSpec((tm, LANES), lambda i: (i, 0)),
                   pl.BlockSpec((tm, LANES), lambda i: (i, 0))],
        out_shape=[jax.ShapeDtypeStruct((n, d), BF16),
                   jax.ShapeDtypeStruct((n, LANES), jnp.int32),
                   jax.ShapeDtypeStruct((n, LANES), F32)],
        compiler_params=_cparams(("parallel",)),
        name="router",
    )(h2, nw, rw)


def _moe_kernel(te_ref, tv_ref, x_ref, wg_ref, wu_ref, wd_ref, o_ref, acc_ref):
    i = pl.program_id(0)
    f = pl.program_id(1)

    @pl.when(f == 0)
    def _():
        acc_ref[...] = jnp.zeros(acc_ref.shape, F32)

    @pl.when(tv_ref[i] > 0)
    def _():
        x = x_ref[...]
        g = _dot(x, wg_ref[0])
        up = _dot(x, wu_ref[0])
        a = (g * jax.nn.sigmoid(g) * up).astype(BF16)
        acc_ref[...] += _dot(a, wd_ref[0])

    @pl.when(f == pl.num_programs(1) - 1)
    def _():
        o_ref[...] = acc_ref[...].astype(o_ref.dtype)


def _moe(x_sorted, tile_expert, tile_valid, wg, wu, wd, tm):
    p, d = x_sorted.shape
    f = wg.shape[2]
    nf = 2
    fh = f // nf
    grid_spec = pltpu.PrefetchScalarGridSpec(
        num_scalar_prefetch=2,
        grid=(p // tm, nf),
        in_specs=[pl.BlockSpec((tm, d), lambda i, j, te, tv: (i, 0)),
                  pl.BlockSpec((1, d, fh), lambda i, j, te, tv: (te[i], 0, j)),
                  pl.BlockSpec((1, d, fh), lambda i, j, te, tv: (te[i], 0, j)),
                  pl.BlockSpec((1, fh, d), lambda i, j, te, tv: (te[i], j, 0))],
        out_specs=pl.BlockSpec((tm, d), lambda i, j, te, tv: (i, 0)),
        scratch_shapes=[pltpu.VMEM((tm, d), F32)],
    )
    return pl.pallas_call(
        _moe_kernel,
        grid_spec=grid_spec,
        out_shape=jax.ShapeDtypeStruct((p, d), BF16),
        compiler_params=_cparams(("arbitrary", "arbitrary")),
        name="moe",
    )(tile_expert, tile_valid, x_sorted, wg, wu, wd)


def _spread_rope_cols(w):
    z = jnp.zeros(w.shape[:-1] + (32,), w.dtype)
    return jnp.concatenate([w[..., 0:32], z, w[..., 32:64], z], axis=-1)


def _relayout_w_in(w):
    dnw = 4 * DN_HEADS * DN_D
    n_gate = 4 * DN_HEADS
    rest = w[:, dnw + n_gate:]
    body = rest[:, :rest.shape[1] - MLA_ROPE]
    kr = _spread_rope_cols(rest[:, rest.shape[1] - MLA_ROPE:])
    gates = jnp.concatenate([w[:, dnw:dnw + n_gate],
                             jnp.zeros((w.shape[0], GATE_COLS - n_gate), w.dtype)], axis=1)
    out = jnp.concatenate([w[:, :dnw], body, kr, gates], axis=1)
    assert out.shape[1] == MAIN_COLS + GATE_COLS
    return out.astype(BF16)


def _relayout_w_uq(w):
    per = MLA_NOPE + MLA_ROPE
    heads = [jnp.concatenate([w[:, h * per:h * per + MLA_NOPE],
                              _spread_rope_cols(w[:, h * per + MLA_NOPE:(h + 1) * per])], axis=1)
             for h in range(MLA_HEADS)]
    return jnp.concatenate(heads, axis=1).astype(BF16)


def _relayout_w_ukv(w):
    per = MLA_NOPE + LANES
    ks = [w[:, h * per:h * per + MLA_NOPE] for h in range(MLA_HEADS)]
    vs = [w[:, h * per + MLA_NOPE:(h + 1) * per] for h in range(MLA_HEADS)]
    return jnp.concatenate(ks + vs, axis=1).astype(BF16)


def _diff_v_weight_t(w):
    c0 = 4 * DN_HEADS * DN_D + 4 * DN_HEADS + 4 * DIFF_HEADS * DIFF_D
    return w[:, c0:c0 + DIFF_HEADS * LANES].T.astype(BF16)


def _qk_norm_vec(w, scale):
    return (jnp.concatenate([w[:MLA_NOPE], _spread_rope_cols(w[MLA_NOPE:])]) * scale)[None, :]


def _t5_bucket(rel):
    half = T5_BUCKETS // 2
    exact = half // 2
    n = jnp.abs(rel)
    large = exact + (jnp.log(jnp.maximum(n, exact).astype(F32) / exact)
                     / math.log(T5_MAX_DIST / exact) * (half - exact)).astype(jnp.int32)
    large = jnp.minimum(large, half - 1)
    return jnp.where(rel > 0, half, 0) + jnp.where(n < exact, n, large)


def _t5_tables(table, tq):
    i = jnp.arange(tq)[None, :]
    j = jnp.arange(LANES)[:, None]
    rels = jnp.stack([j - i + (d * LANES) for d in (-1, 0, 1)])
    table = table.astype(F32) * LOG2E
    bandt = jnp.transpose(table[_t5_bucket(rels)], (3, 0, 1, 2)).astype(BF16)
    far_rel = jnp.array([-T5_MAX_DIST, T5_MAX_DIST], jnp.int32)
    far = jnp.transpose(table[_t5_bucket(far_rel)], (1, 0))
    far = jnp.broadcast_to(far[:, :, None], far.shape + (LANES,))
    return bandt, far


def _rope_tables(lp):
    half = MLA_ROPE // 2
    inv = 1.0 / (ROPE_THETA ** (jnp.arange(half, dtype=F32) / half))
    pos = (jnp.arange(lp, dtype=jnp.int32) - N_PAD).astype(F32)
    ang = pos[:, None] * inv[None, :]
    c, s, z = jnp.cos(ang), jnp.sin(ang), jnp.zeros((lp, half), F32)
    return jnp.concatenate([c, z, c, z], axis=1), jnp.concatenate([-s, z, s, z], axis=1)


def _moe_plan(idx, n, tm):
    e_flat = idx.reshape(-1)
    onehot = (e_flat[:, None] == jnp.arange(N_EXPERTS)[None, :]).astype(jnp.int32)
    counts = jnp.sum(onehot, axis=0)
    rank = jnp.sum((jnp.cumsum(onehot, axis=0) - onehot) * onehot, axis=1)
    tiles_per = (counts + tm - 1) // tm
    tile_end = jnp.cumsum(tiles_per)
    start = (tile_end - tiles_per) * tm
    dest = start[e_flat] + rank
    n_tiles = (2 * n) // tm + N_EXPERTS
    p = n_tiles * tm
    slot_tok = jnp.zeros((p,), jnp.int32).at[dest].set(jnp.arange(2 * n, dtype=jnp.int32) // 2)
    t = jnp.arange(n_tiles, dtype=jnp.int32)
    tile_expert = jnp.minimum(jnp.sum((t[:, None] >= tile_end[None, :]).astype(jnp.int32), axis=1),
                              N_EXPERTS - 1).astype(jnp.int32)
    tile_valid = (t < tile_end[-1]).astype(jnp.int32)
    return slot_tok, dest.reshape(n, 2), tile_expert, tile_valid


def kernel(x, meta_tokens, rel_bias_table, attn_norm_w, w_in, dn_conv_w, dn_a_log, dn_dt_bias, dn_norm_w, diff_q_norm_w, diff_k_norm_w, diff_lambda, diff_subln_w, mla_q_lat_norm_w, mla_w_uq, mla_kv_lat_norm_w, mla_w_ukv, mla_q_norm_w, mla_k_norm_w, w_out, ffn_norm_w, ffn_w_gate, ffn_w_up, ffn_w_down, router_w, moe_w_gate, moe_w_up, moe_w_down):
    bsz, seq, d = x.shape
    depth = w_in.shape[0]
    lp = ROW0 + seq
    n = bsz * lp
    assert lp % LANES == 0 and meta_tokens.shape[0] == N_META

    meta = jnp.broadcast_to(meta_tokens[None].astype(x.dtype), (bsz, N_META, d))
    h = jnp.concatenate([jnp.zeros((bsz, N_PAD, d), x.dtype), meta, x], axis=1)

    cos_t, sin_t = _rope_tables(lp)
    bandt, far = _t5_tables(rel_bias_table, LANES)
    n_chunks = lp // DN_CHUNK

    for l in range(depth):
        main, gate, dvt = _inproj(h, attn_norm_w[l][None, :], _relayout_w_in(w_in[l]),
                                  _diff_v_weight_t(w_in[l]))

        qkv = _dnprep(main, dn_conv_w[l])
        gate_row = jnp.transpose(gate[:, :, 0:16].reshape(bsz, n_chunks, DN_CHUNK, 16), (0, 1, 3, 2))
        neg_a = -jnp.exp(dn_a_log[l].astype(F32)).reshape(-1)
        dtb = dn_dt_bias[l].astype(F32).reshape(-1)
        z8 = jnp.zeros((8,), F32)
        pc = jnp.stack([jnp.concatenate([z8, neg_a, jnp.zeros((GATE_COLS - 16,), F32)]),
                        jnp.concatenate([z8, dtb, jnp.zeros((GATE_COLS - 16,), F32)])])
        pr = jnp.concatenate([jnp.broadcast_to(jnp.concatenate([z8, neg_a])[:, None], (16, DN_CHUNK)),
                              jnp.broadcast_to(jnp.concatenate([z8, dtb])[:, None], (16, DN_CHUNK))], axis=1)
        o_f, o_b = _deltanet(qkv, gate, gate_row, pc, pr)

        dscale = DIFF_D ** -0.5 * LOG2E
        mscale = (MLA_NOPE + MLA_ROPE) ** -0.5 * LOG2E
        dq, dk, mq, mk, mv = _attnprep(
            main,
            (jnp.tile(diff_q_norm_w[l], 2) * dscale)[None, :],
            jnp.tile(diff_k_norm_w[l], 2)[None, :],
            mla_q_lat_norm_w[l][None, :], _relayout_w_uq(mla_w_uq[l]),
            mla_kv_lat_norm_w[l][None, :], _relayout_w_ukv(mla_w_ukv[l]),
            _qk_norm_vec(mla_q_norm_w[l], mscale), _qk_norm_vec(mla_k_norm_w[l], 1.0),
            cos_t, sin_t)
        lam_init = 0.8 - 0.6 * math.exp(-0.3 * l)
        lpar = diff_lambda[l].astype(F32)
        lam = jnp.exp(jnp.sum(lpar[0] * lpar[1])) - jnp.exp(jnp.sum(lpar[2] * lpar[3])) + lam_init
        ydf = _diffattn(dq, dk, dvt, bandt, far, jnp.full((1, LANES), lam, F32),
                        (diff_subln_w[l] * (1.0 - lam_init))[None, :])
        yml = _mlaattn(mq, mk, mv)

        h = _outproj(o_f, o_b, main, ydf, yml, h, dn_norm_w[l][None, :], w_out[l].astype(BF16))

        h2 = h.reshape(n, d)
        if l % 2 == 0:
            i = l // 2
            h2 = _ffn(h2, ffn_norm_w[l][None, :], _to_bf16(ffn_w_gate[i]),
                      _to_bf16(ffn_w_up[i]), _to_bf16(ffn_w_down[i]))
        else:
            i = l // 2
            rw = jnp.concatenate([router_w[i], jnp.zeros((d, LANES - N_EXPERTS), F32)], axis=1)
            u, idx, gates = _router(h2, ffn_norm_w[l][None, :], rw)
            tm = 512 if (2 * n) % 512 == 0 else 128
            slot_tok, tok_slots, tile_expert, tile_valid = _moe_plan(idx[:, 0:2], n, tm)
            x_sorted = jnp.take(u, slot_tok, axis=0)
            y_sorted = _moe(x_sorted, tile_expert, tile_valid,
                            _to_bf16(moe_w_gate[i]), _to_bf16(moe_w_up[i]),
                            _to_bf16(moe_w_down[i]), tm)
            if l == depth - 1:
                real = lambda t: t.reshape((bsz, lp) + t.shape[1:])[:, ROW0:].reshape((bsz * seq,) + t.shape[1:])
                h2, gates, tok_slots = real(h2), real(gates), real(tok_slots)
            h2 = (h2 + gates[:, 0:1] * jnp.take(y_sorted, tok_slots[:, 0], axis=0)
                  + gates[:, 1:2] * jnp.take(y_sorted, tok_slots[:, 1], axis=0))
            if l == depth - 1:
                return h2.reshape(bsz, seq, d)
        h = h2.reshape(bsz, lp, d)
    return h[:, ROW0:]
```

```python
import functools
import math

import jax
import jax.numpy as jnp
from jax import lax
from jax.experimental import pallas as pl
from jax.experimental.pallas import tpu as pltpu

F32 = jnp.float32
BF16 = jnp.bfloat16

N_META = 16
N_PAD = 112
ROW0 = N_PAD + N_META
DN_HEADS = 4
DN_D = 128
DN_CONV = 5
DN_CHUNK = 64
DIFF_HEADS = 4
DIFF_D = 64
MLA_HEADS = 4
MLA_NOPE = 128
MLA_ROPE = 64
MLA_Q_RANK = 256
MLA_KV_RANK = 128
ROPE_THETA = 10000.0
T5_BUCKETS = 32
T5_MAX_DIST = 128
N_EXPERTS = 8
EPS = 1e-6
NEG_BIG = -1e30
LOG2E = 1.4426950408889634
MASK_LANE = 32
LANES = 128
VMEM_LIMIT = 56 * 1024 * 1024

ATTN_COLS = 1536
DN_COL0 = ATTN_COLS
MAIN_COLS = 3584
GATE_COLS = 128


def _dot(a, b):
    return jnp.dot(a, b, preferred_element_type=F32)


def _dot_nt(a, b):
    return lax.dot_general(a, b, (((1,), (1,)), ((), ())), preferred_element_type=F32)


def _ones_bf16(n):
    return jnp.ones((n, n), BF16)


def _row_tile(lp):
    return 384 if lp % 384 == 0 else 128


def _cparams(sem):
    return pltpu.CompilerParams(dimension_semantics=sem, vmem_limit_bytes=VMEM_LIMIT)


def _inproj_kernel(h_ref, nw_ref, w_ref, wvt_ref, main_ref, gate_ref, vt_ref):
    x = h_ref[0]
    ms = jnp.mean(x * x, axis=-1, keepdims=True)
    u = (x * lax.rsqrt(ms + EPS) * nw_ref[...]).astype(BF16)
    for c0 in range(0, MAIN_COLS, 512):
        main_ref[0, :, c0:c0 + 512] = _dot(u, w_ref[:, c0:c0 + 512]).astype(BF16)
    gate_ref[0] = _dot(u, w_ref[:, MAIN_COLS:MAIN_COLS + GATE_COLS])
    vt_ref[0] = _dot_nt(wvt_ref[...], u).astype(BF16)


def _inproj(h, norm_w, w, wvt):
    b, lp, d = h.shape
    tm = _row_tile(lp)
    ncol = MAIN_COLS + GATE_COLS
    nv = wvt.shape[0]
    return pl.pallas_call(
        _inproj_kernel,
        grid=(b, lp // tm),
        in_specs=[
            pl.BlockSpec((1, tm, d), lambda i, j: (i, j, 0)),
            pl.BlockSpec((1, d), lambda i, j: (0, 0)),
            pl.BlockSpec((d, ncol), lambda i, j: (0, 0)),
            pl.BlockSpec((nv, d), lambda i, j: (0, 0)),
        ],
        out_specs=[
            pl.BlockSpec((1, tm, MAIN_COLS), lambda i, j: (i, j, 0)),
            pl.BlockSpec((1, tm, GATE_COLS), lambda i, j: (i, j, 0)),
            pl.BlockSpec((1, nv, tm), lambda i, j: (i, 0, j)),
        ],
        out_shape=[
            jax.ShapeDtypeStruct((b, lp, MAIN_COLS), BF16),
            jax.ShapeDtypeStruct((b, lp, GATE_COLS), F32),
            jax.ShapeDtypeStruct((b, nv, lp), BF16),
        ],
        compiler_params=_cparams(("parallel", "parallel")),
        name="inproj",
    )(h, norm_w, w, wvt)


def _dnprep_kernel(x_ref, cw_ref, o_ref, xs_ref):
    j = pl.program_id(1)
    lp = x_ref.shape[1]
    halo = 8
    xs_ref[0:halo + N_PAD, :] = jnp.zeros((halo + N_PAD, LANES), F32)
    xs_ref[halo + N_PAD:halo + lp, :] = x_ref[0, N_PAD:, :].astype(F32)
    xs_ref[halo + lp:2 * halo + lp, :] = jnp.zeros((halo, LANES), F32)
    q_scale = jnp.where(j < DN_HEADS, DN_D ** -0.5, 1.0).astype(F32)
    is_qk = j < 2 * DN_HEADS
    rc = 128
    for c in range(lp // rc):
        base = c * rc
        acc = cw_ref[0:1, :] * xs_ref[base + halo - 2:base + halo - 2 + rc, :]
        for t in range(1, DN_CONV):
            off = base + halo - 2 + t
            acc = acc + cw_ref[t:t + 1, :] * xs_ref[off:off + rc, :]
        y = acc * jax.nn.sigmoid(acc)
        ss = jnp.sum(y * y, axis=-1, keepdims=True)
        r = jnp.where(is_qk, lax.rsqrt(ss + EPS) * q_scale, 1.0)
        o_ref[0, base:base + rc, :] = (y * r).astype(BF16)


def _dnprep(main, conv_w):
    b, lp, _ = main.shape
    ncb = 3 * DN_HEADS
    return pl.pallas_call(
        _dnprep_kernel,
        grid=(b, ncb),
        in_specs=[
            pl.BlockSpec((1, lp, LANES), lambda i, j: (i, 0, DN_COL0 // LANES + j)),
            pl.BlockSpec((DN_CONV, LANES), lambda i, j: (0, j)),
        ],
        out_specs=pl.BlockSpec((1, lp, LANES), lambda i, j: (i, 0, j)),
        out_shape=jax.ShapeDtypeStruct((b, lp, ncb * LANES), BF16),
        scratch_shapes=[pltpu.VMEM((lp + 16, LANES), F32)],
        compiler_params=_cparams(("parallel", "parallel")),
        name="dnprep",
    )(main, conv_w)


def _each(fn, *lists):
    return [fn(*args) for args in zip(*lists)]


def _unit_tri_inverse(a_list, eye, blk16, blk32):
    bf = lambda t: t.astype(BF16)
    a0 = _each(lambda a: jnp.where(blk16, a, 0.0), a_list)
    a0b = _each(bf, a0)
    a2b = _each(lambda x: bf(_dot(x, x)), a0b)
    a4b = _each(lambda x: bf(_dot(x, x)), a2b)
    a8b = _each(lambda x: bf(_dot(x, x)), a4b)
    p = _each(lambda x: eye - x, a0)
    for pw in (a2b, a4b, a8b):
        p = _each(lambda x, y: x + _dot(bf(x), y), p, pw)
    off1 = _each(lambda a: bf(jnp.where(blk32 & (~blk16), a, 0.0)), a_list)
    off2 = _each(lambda a: bf(jnp.where(blk32, 0.0, a)), a_list)
    for off in (off1, off2):
        pb = _each(bf, p)
        mid = _each(lambda x, y: bf(_dot(x, y)), pb, off)
        p = _each(lambda x, m, xb: x - _dot(m, xb), p, mid, pb)
    return p


def _dn_kernel(qf_ref, kf_ref, vf_ref, qb_ref, kb_ref, vb_ref, gcf_ref, gcb_ref,
               grf_ref, grb_ref, pc_ref, pr_ref, of_ref, ob_ref, st_ref):
    s = pl.program_id(1)
    c = DN_CHUNK
    nb = qf_ref.shape[0]

    @pl.when(s == 0)
    def _():
        st_ref[...] = jnp.zeros(st_ref.shape, F32)

    row = lax.broadcasted_iota(jnp.int32, (c, c), 0)
    col = lax.broadcasted_iota(jnp.int32, (c, c), 1)
    eye = (row == col).astype(F32)
    blk16 = (row // 16) == (col // 16)
    blk32 = (row // 32) == (col // 32)
    negA_c, dtb_c = pc_ref[0:1, :], pc_ref[1:2, :]
    negA_r, dtb_r = pr_ref[:, 0:c], pr_ref[:, c:2 * c]
    bf = lambda t: t.astype(BF16)

    per_dir = []
    for d, (gc_ref, gr_ref) in enumerate(((gcf_ref, grf_ref), (gcb_ref, grb_ref))):
        if d == 0:
            causal = s >= 0
            chunk = s
        else:
            causal = s <= 1
            chunk = jnp.where(s <= 1, s, (pl.num_programs(1) + 1) - s)
        sgn = jnp.where(causal, 1, -1)
        m_in = (row - col) * sgn >= 0
        m_strict = (row - col) * sgn > 0
        m_f = m_in.astype(F32)
        m_t = ((col - row) * sgn >= 0).astype(F32)
        pos_c = chunk * c + lax.broadcasted_iota(jnp.int32, (c, LANES), 0)
        live_c = pos_c >= N_PAD
        pos_r = chunk * c + lax.broadcasted_iota(jnp.int32, (16, c), 1)
        gates = []
        for bi in range(nb):
            gcol = gc_ref[bi]
            beta_c = jnp.where(live_c, jax.nn.sigmoid(gcol), 0.0)
            g_c = jnp.where(live_c, negA_c * jax.nn.softplus(gcol + dtb_c), 0.0)
            grow = gr_ref[bi, 0]
            g_r = jnp.where(pos_r >= N_PAD, negA_r * jax.nn.softplus(grow + dtb_r), 0.0)
            gc_c = jnp.dot(m_f, g_c, preferred_element_type=F32, precision=lax.Precision.HIGHEST)
            gc_r = jnp.dot(g_r, m_t, preferred_element_type=F32, precision=lax.Precision.HIGHEST)
            gtot_c = jnp.sum(g_c, axis=0, keepdims=True)
            gates.append(dict(beta_c=beta_c, gc_c=gc_c, gc_r=gc_r, eg_c=jnp.exp(gc_c),
                              ew_c=jnp.exp(gtot_c - gc_c), et_c=jnp.exp(gtot_c)))
        per_dir.append(dict(m_in=m_in, m_strict=m_strict, gates=gates))

    refs = ((qf_ref, kf_ref, vf_ref, of_ref), (qb_ref, kb_ref, vb_ref, ob_ref))
    chains = [(bi, d, h) for bi in range(nb) for d in range(2) for h in range(DN_HEADS)]
    sl = lambda h: slice(h * DN_D, (h + 1) * DN_D)
    bcol = lambda d, h: d * DN_HEADS + h
    acol = lambda d, h: 2 * DN_HEADS + d * DN_HEADS + h
    sidx = lambda bi, d, h: (bi * 2 + d) * DN_HEADS + h
    gate = lambda bi, d, name: per_dir[d]["gates"][bi][name]

    q = [refs[d][0][bi, :, sl(h)] for bi, d, h in chains]
    k = [refs[d][1][bi, :, sl(h)] for bi, d, h in chains]
    v = [refs[d][2][bi, :, sl(h)] for bi, d, h in chains]
    kf = _each(lambda t: t.astype(F32), k)
    beta = [gate(bi, d, "beta_c")[:, bcol(d, h):bcol(d, h) + 1] for bi, d, h in chains]
    eg = [gate(bi, d, "eg_c")[:, acol(d, h):acol(d, h) + 1] for bi, d, h in chains]
    kb = _each(lambda x, b: x * b, kf, beta)
    vb = _each(lambda x, b: x.astype(F32) * b, v, beta)

    def decay_of(bi, d, h):
        m_in = per_dir[d]["m_in"]
        diff = gate(bi, d, "gc_c")[:, acol(d, h):acol(d, h) + 1] - gate(bi, d, "gc_r")[acol(d, h):acol(d, h) + 1, :]
        return jnp.where(m_in, jnp.exp(jnp.where(m_in, diff, 0.0)), 0.0)

    decay = [decay_of(*ch) for ch in chains]
    kq = _each(lambda x, y, z: _dot_nt(jnp.concatenate([bf(x), y], axis=0), z), kb, q, k)
    a = [jnp.where(per_dir[d]["m_strict"], kq_i[0:c] * dec, 0.0)
         for (bi, d, h), kq_i, dec in zip(chains, kq, decay)]
    qk = _each(lambda x, dec: bf(x[c:2 * c] * dec), kq, decay)
    t = _unit_tri_inverse(a, eye, blk16, blk32)
    x = _each(lambda x1, x2, e: bf(jnp.concatenate([x1, x2 * e], axis=1)), vb, kb, eg)
    tx = _each(lambda ti, xi: _dot(bf(ti), xi), t, x)
    st = [st_ref[sidx(*ch)] for ch in chains]
    lhs = _each(lambda txi, qi, e: bf(jnp.concatenate([txi[:, DN_D:2 * DN_D], qi.astype(F32) * e], axis=0)),
                tx, q, eg)
    r = _each(lambda l, si: _dot(l, bf(si)), lhs, st)
    v_new = _each(lambda txi, ri: bf(txi[:, 0:DN_D] - ri[0:c]), tx, r)
    out = _each(lambda ri, qki, vn: ri[c:2 * c] + _dot(qki, vn), r, qk, v_new)
    kw = [bf((kfi * gate(bi, d, "ew_c")[:, acol(d, h):acol(d, h) + 1]).T) for (bi, d, h), kfi in zip(chains, kf)]
    upd = _each(_dot, kw, v_new)
    for (bi, d, h), o_i, st_i, u_i in zip(chains, out, st, upd):
        refs[d][3][bi, :, sl(h)] = o_i.astype(BF16)
        st_ref[sidx(bi, d, h)] = st_i * gate(bi, d, "et_c")[:, acol(d, h):acol(d, h) + 1] + u_i


DN_BATCH = 4


def _deltanet(qkv, gate_col, gate_row, pc, pr):
    b, lp, _ = qkv.shape
    n = lp // DN_CHUNK
    hw = DN_HEADS * DN_D
    nb = DN_BATCH if b % DN_BATCH == 0 else 1

    def fwd_chunk(s):
        return s

    def bwd_chunk(s):
        return jnp.where(s <= 1, s, n + 1 - s)

    def spec3(cb, chunk):
        return pl.BlockSpec((nb, DN_CHUNK, hw), lambda i, s: (i, chunk(s), cb))

    in_specs = [spec3(0, fwd_chunk), spec3(1, fwd_chunk), spec3(2, fwd_chunk),
                spec3(0, bwd_chunk), spec3(1, bwd_chunk), spec3(2, bwd_chunk),
                pl.BlockSpec((nb, DN_CHUNK, GATE_COLS), lambda i, s: (i, fwd_chunk(s), 0)),
                pl.BlockSpec((nb, DN_CHUNK, GATE_COLS), lambda i, s: (i, bwd_chunk(s), 0)),
                pl.BlockSpec((nb, 1, 16, DN_CHUNK), lambda i, s: (i, fwd_chunk(s), 0, 0)),
                pl.BlockSpec((nb, 1, 16, DN_CHUNK), lambda i, s: (i, bwd_chunk(s), 0, 0)),
                pl.BlockSpec((2, GATE_COLS), lambda i, s: (0, 0)),
                pl.BlockSpec((16, 2 * DN_CHUNK), lambda i, s: (0, 0))]
    out_specs = [pl.BlockSpec((nb, DN_CHUNK, hw), lambda i, s: (i, fwd_chunk(s), 0)),
                 pl.BlockSpec((nb, DN_CHUNK, hw), lambda i, s: (i, bwd_chunk(s), 0))]
    return pl.pallas_call(
        _dn_kernel,
        grid=(b // nb, n),
        in_specs=in_specs,
        out_specs=out_specs,
        out_shape=[jax.ShapeDtypeStruct((b, lp, hw), BF16)] * 2,
        scratch_shapes=[pltpu.VMEM((nb * 2 * DN_HEADS, DN_D, DN_D), F32)],
        compiler_params=_cparams(("parallel", "arbitrary")),
        name="deltanet",
    )(qkv, qkv, qkv, qkv, qkv, qkv, gate_col, gate_col, gate_row, gate_row, pc, pr)


def _attnprep_kernel(x_ref, dqw_ref, dkw_ref, qlw_ref, wuq_ref, klw_ref, wukv_ref,
                     mqw_ref, mkw_ref, cos_ref, sin_ref,
                     dq_ref, dk_ref, mq_ref, mk_ref, mv_ref):
    ones = _ones_bf16(LANES)
    row = lax.broadcasted_iota(jnp.int32, (LANES, LANES), 0)
    col = lax.broadcasted_iota(jnp.int32, (LANES, LANES), 1)
    half_ones = ((row // DIFF_D) == (col // DIFF_D)).astype(BF16)
    cos_t, sin_t = cos_ref[...], sin_ref[...]
    tm = x_ref.shape[1]
    lane_t = lax.broadcasted_iota(jnp.int32, (tm, LANES), 1)
    row_t = pl.program_id(1) * tm + lax.broadcasted_iota(jnp.int32, (tm, LANES), 0)
    q_flag = jnp.where(lane_t == MASK_LANE, 1.0, 0.0)
    k_flag = jnp.where((lane_t == MASK_LANE) & (row_t < N_PAD), NEG_BIG, 0.0)

    def rope(t):
        return t * cos_t + pltpu.roll(t, 64, 1) * sin_t

    for src, w_ref, dst in ((0, dqw_ref, dq_ref), (512, dkw_ref, dk_ref)):
        for h in range(DIFF_HEADS):
            y = x_ref[0, :, src + h * LANES:src + (h + 1) * LANES].astype(F32)
            ms = _dot((y * y).astype(BF16), half_ones) * (1.0 / DIFF_D)
            dst[0, :, h * LANES:(h + 1) * LANES] = (y * lax.rsqrt(ms + EPS) * w_ref[...]).astype(BF16)

    cq0 = x_ref[0, :, 1024:1152].astype(F32)
    cq1 = x_ref[0, :, 1152:1280].astype(F32)
    ms = _dot((cq0 * cq0 + cq1 * cq1).astype(BF16), ones) * (1.0 / MLA_Q_RANK)
    r = lax.rsqrt(ms + EPS)
    cqn = jnp.concatenate([cq0 * r * qlw_ref[:, 0:LANES], cq1 * r * qlw_ref[:, LANES:2 * LANES]],
                          axis=1).astype(BF16)
    q = _dot(cqn, wuq_ref[...])
    inv_d = 1.0 / (MLA_NOPE + MLA_ROPE)
    for h in range(MLA_HEADS):
        q0 = q[:, 256 * h:256 * h + LANES]
        q1 = q[:, 256 * h + LANES:256 * (h + 1)]
        ms = _dot((q0 * q0 + q1 * q1).astype(BF16), ones) * inv_d
        r = lax.rsqrt(ms + EPS)
        mq_ref[0, :, 256 * h:256 * h + LANES] = (q0 * r * mqw_ref[:, 0:LANES]).astype(BF16)
        mq_ref[0, :, 256 * h + LANES:256 * (h + 1)] = (
            rope(q1 * r * mqw_ref[:, LANES:2 * LANES]) + q_flag).astype(BF16)

    ckv = x_ref[0, :, 1280:1408].astype(F32)
    ms = _dot((ckv * ckv).astype(BF16), ones) * (1.0 / MLA_KV_RANK)
    ckvn = (ckv * lax.rsqrt(ms + EPS) * klw_ref[...]).astype(BF16)
    kv = _dot(ckvn, wukv_ref[...])
    mv_ref[0] = kv[:, 512:1024].astype(BF16)
    kr = x_ref[0, :, 1408:1536].astype(F32)
    kr2 = kr * kr
    for h in range(MLA_HEADS):
        k0 = kv[:, LANES * h:LANES * (h + 1)]
        ms = _dot((k0 * k0 + kr2).astype(BF16), ones) * inv_d
        r = lax.rsqrt(ms + EPS)
        mk_ref[0, :, 256 * h:256 * h + LANES] = (k0 * r * mkw_ref[:, 0:LANES]).astype(BF16)
        mk_ref[0, :, 256 * h + LANES:256 * (h + 1)] = (
            rope(kr * r * mkw_ref[:, LANES:2 * LANES]) + k_flag).astype(BF16)


def _attnprep(main, dqw, dkw, qlw, wuq, klw, wukv, mqw, mkw, cos_t, sin_t):
    b, lp, _ = main.shape
    tm = _row_tile(lp)

    def full(a):
        return pl.BlockSpec(a.shape, lambda i, j: (0,) * a.ndim)

    def rows(width):
        return pl.BlockSpec((1, tm, width), lambda i, j: (i, j, 0))

    return pl.pallas_call(
        _attnprep_kernel,
        grid=(b, lp // tm),
        in_specs=[pl.BlockSpec((1, tm, ATTN_COLS), lambda i, j: (i, j, 0)),
                  full(dqw), full(dkw), full(qlw), full(wuq), full(klw), full(wukv),
                  full(mqw), full(mkw),
                  pl.BlockSpec((tm, LANES), lambda i, j: (j, 0)),
                  pl.BlockSpec((tm, LANES), lambda i, j: (j, 0))],
        out_specs=[rows(512), rows(512), rows(1024), rows(1024), rows(512)],
        out_shape=[jax.ShapeDtypeStruct((b, lp, 512), BF16),
                   jax.ShapeDtypeStruct((b, lp, 512), BF16),
                   jax.ShapeDtypeStruct((b, lp, 1024), BF16),
                   jax.ShapeDtypeStruct((b, lp, 1024), BF16),
                   jax.ShapeDtypeStruct((b, lp, 512), BF16)],
        compiler_params=_cparams(("parallel", "parallel")),
        name="attnprep",
    )(main, dqw, dkw, qlw, wuq, klw, wukv, mqw, mkw, cos_t, sin_t)


KEY_PARTS = 3


def _key_splits(lp, parts=None):
    nblk = lp // LANES
    nparts = min(parts or KEY_PARTS, nblk)
    edges = [(nblk * i // nparts) * LANES for i in range(nparts + 1)]
    return tuple(zip(edges[:-1], edges[1:]))


def _diff_attend_t(score_fns, vt_ref, lam, lp):
    splits = _key_splits(lp)
    tq = LANES
    nt = len(score_fns)
    add = lambda x, y: x + y

    def scores(t):
        return [score_fns[t](k0, k1) for k0, k1 in splits]

    def softmax(st):
        m = functools.reduce(jnp.maximum, [x.max(axis=0, keepdims=True) for x in st])
        pt = [jnp.exp2(x - m) for x in st]
        lt = functools.reduce(add, [x.sum(axis=0, keepdims=True) for x in pt])
        l1, l2 = lt[:, 0:tq], lt[:, tq:2 * tq]
        r = lam * l1 / l2
        return [(x[:, 0:tq] - x[:, tq:2 * tq] * r).astype(BF16) for x in pt], 1.0 / l1

    def values(grp):
        acc = None
        for i, (k0, k1) in enumerate(splits):
            w = grp[0][0][i] if len(grp) == 1 else jnp.concatenate([grp[0][0][i], grp[1][0][i]], axis=1)
            part = _dot(vt_ref[0, :, k0:k1], w)
            acc = part if acc is None else acc + part
        return [acc[:, j * tq:(j + 1) * tq] * grp[j][1] for j in range(len(grp))]

    s = [scores(0)]
    sm, outs = [], []
    for t in range(nt):
        if t + 1 < nt:
            s.append(scores(t + 1))
        sm.append(softmax(s[t]))
        if t % 2 == 1:
            outs.extend(values(sm[t - 1:t + 1]))
    if nt % 2 == 1:
        outs.extend(values(sm[nt - 1:nt]))
    return outs


def _diffattn_kernel(q_ref, k_ref, vt_ref, bandt_ref, far_ref, lam_ref, sw_ref, o_ref, kx_ref):
    qi = pl.program_id(2)
    tq = LANES
    lp = k_ref.shape[1]
    nt = kx_ref.shape[0]
    nblk = lp // LANES
    c_neg = jnp.broadcast_to(far_ref[0, 0:1, :], (LANES, LANES)).astype(BF16)

    @pl.when(qi == 0)
    def _():
        c_pos = jnp.broadcast_to(far_ref[0, 1:2, :], (lp, LANES)).astype(BF16)
        for t in range(nt):
            kx_ref[t, :, 0:LANES] = k_ref[0]
            kx_ref[t, :, LANES:2 * LANES] = c_pos

    lane = lax.broadcasted_iota(jnp.int32, (tq, LANES), 1)
    rowi = lax.broadcasted_iota(jnp.int32, (tq, LANES), 0)
    eye = jnp.where(lane == rowi, 1.0, 0.0).astype(BF16)
    score_fns = []
    for t in range(nt):
        tile = qi * nt + t

        def put(kb, val, t=t):
            @pl.when((kb >= 0) & (kb < nblk))
            def _():
                kx_ref[t, pl.ds(pl.multiple_of(kb * LANES, LANES), LANES), LANES:2 * LANES] = val

        for back in range(nt):
            put(tile - 2 - back, c_neg)
        for dd in range(3):
            put(tile - 1 + dd, bandt_ref[0, dd])

        @pl.when(tile <= nt + 1)
        def _(t=t):
            kx_ref[t, 0:N_PAD, LANES:2 * LANES] = jnp.full((N_PAD, LANES), NEG_BIG, BF16)

        q = q_ref[0, t * tq:(t + 1) * tq, :]
        zero = jnp.zeros_like(q)
        lhs = jnp.concatenate([jnp.concatenate([jnp.where(lane < DIFF_D, q, zero), eye], axis=1),
                               jnp.concatenate([jnp.where(lane >= DIFF_D, q, zero), eye], axis=1)], axis=0)
        score_fns.append(lambda k0, k1, t=t, lhs=lhs: _dot_nt(kx_ref[t, k0:k1, :], lhs))
    for t, ot in enumerate(_diff_attend_t(score_fns, vt_ref, lam_ref[0:1, 0:1], lp)):
        o = ot.T
        ms = jnp.mean(o * o, axis=-1, keepdims=True)
        o_ref[0, t * tq:(t + 1) * tq, :] = (o * lax.rsqrt(ms + EPS) * sw_ref[...]).astype(BF16)


DIFF_TILES = 3


def _diffattn(dq, dk, dvt, bandt, far, lam, sw):
    b, lp, _ = dq.shape
    nblk = lp // LANES
    nt = DIFF_TILES
    rows = nt * LANES
    return pl.pallas_call(
        _diffattn_kernel,
        grid=(b, DIFF_HEADS, pl.cdiv(nblk, nt)),
        in_specs=[pl.BlockSpec((1, rows, LANES), lambda i, h, j: (i, j, h)),
                  pl.BlockSpec((1, lp, LANES), lambda i, h, j: (i, 0, h)),
                  pl.BlockSpec((1, LANES, lp), lambda i, h, j: (i, h, 0)),
                  pl.BlockSpec((1, 3, LANES, LANES), lambda i, h, j: (h, 0, 0, 0)),
                  pl.BlockSpec((1, 2, LANES), lambda i, h, j: (h, 0, 0)),
                  pl.BlockSpec((1, LANES), lambda i, h, j: (0, 0)),
                  pl.BlockSpec((1, LANES), lambda i, h, j: (0, 0))],
        out_specs=pl.BlockSpec((1, rows, LANES), lambda i, h, j: (i, j, h)),
        out_shape=jax.ShapeDtypeStruct((b, lp, DIFF_HEADS * LANES), BF16),
        scratch_shapes=[pltpu.VMEM((nt, lp, 2 * LANES), BF16)],
        compiler_params=_cparams(("parallel", "parallel", "arbitrary")),
        name="diffattn",
    )(dq, dk, dvt, bandt, far, lam, sw)


def _mlaattn_kernel(q_ref, k_ref, v_ref, o_ref, vx_ref):
    lp = k_ref.shape[1]

    @pl.when(pl.program_id(2) == 0)
    def _():
        vx_ref[:, 0:LANES] = v_ref[0]
        vx_ref[:, LANES:2 * LANES] = jnp.ones((lp, LANES), BF16)

    q = q_ref[0]
    splits = _key_splits(lp, MLA_KEY_PARTS)
    s = [_dot_nt(q, k_ref[0, k0:k1, :]) for k0, k1 in splits]
    m = [x.max(axis=1, keepdims=True) for x in s]
    p = [jnp.exp2((x - mi).astype(BF16)) for x, mi in zip(s, m)]
    o = [_dot(pi, vx_ref[k0:k1, :]) for (k0, k1), pi in zip(splits, p)]
    m_all = functools.reduce(jnp.maximum, m)
    acc = functools.reduce(lambda a, b: a + b, [oi * jnp.exp2(mi - m_all) for oi, mi in zip(o, m)])
    o_ref[0] = (acc[:, 0:LANES] / acc[:, LANES:LANES + 1]).astype(BF16)


MLA_KEY_PARTS = 4
MLA_Q_STEPS = 4


def _mlaattn(mq, mk, mv):
    b, lp, _ = mq.shape
    tq = lp // MLA_Q_STEPS if lp % (16 * MLA_Q_STEPS) == 0 else LANES
    return pl.pallas_call(
        _mlaattn_kernel,
        grid=(b, MLA_HEADS, lp // tq),
        in_specs=[pl.BlockSpec((1, tq, 256), lambda i, h, j: (i, j, h)),
                  pl.BlockSpec((1, lp, 256), lambda i, h, j: (i, 0, h)),
                  pl.BlockSpec((1, lp, LANES), lambda i, h, j: (i, 0, h))],
        out_specs=pl.BlockSpec((1, tq, LANES), lambda i, h, j: (i, j, h)),
        out_shape=jax.ShapeDtypeStruct((b, lp, MLA_HEADS * LANES), BF16),
        scratch_shapes=[pltpu.VMEM((lp, 2 * LANES), BF16)],
        compiler_params=_cparams(("parallel", "parallel", "arbitrary")),
        name="mlaattn",
    )(mq, mk, mv)


def _outproj_kernel(of_ref, ob_ref, z_ref, df_ref, ml_ref, h_ref, dnw_ref, w_ref, o_ref):
    ones = _ones_bf16(LANES)
    parts = []
    for h in range(DN_HEADS):
        sl = slice(h * DN_D, (h + 1) * DN_D)
        o = of_ref[0, :, sl].astype(F32) + ob_ref[0, :, sl].astype(F32)
        ms = _dot((o * o).astype(BF16), ones) * (1.0 / DN_D)
        z = z_ref[0, :, sl].astype(F32)
        parts.append((o * lax.rsqrt(ms + EPS) * dnw_ref[...] * (z * jax.nn.sigmoid(z))).astype(BF16))
    ydn = jnp.concatenate(parts, axis=1)
    acc = h_ref[0] + _dot(ydn, w_ref[0:512, :])
    acc = acc + _dot(df_ref[0], w_ref[512:1024, :])
    acc = acc + _dot(ml_ref[0], w_ref[1024:1536, :])
    o_ref[0] = acc


def _outproj(o_f, o_b, main, ydf, yml, h, dnw, w):
    b, lp, d = h.shape
    tm = _row_tile(lp)

    def rows(width, cb=0):
        return pl.BlockSpec((1, tm, width), lambda i, j: (i, j, cb))

    return pl.pallas_call(
        _outproj_kernel,
        grid=(b, lp // tm),
        in_specs=[rows(512), rows(512), rows(512, (DN_COL0 + 1536) // 512), rows(512), rows(512), rows(d),
                  pl.BlockSpec((1, LANES), lambda i, j: (0, 0)),
                  pl.BlockSpec(w.shape, lambda i, j: (0, 0))],
        out_specs=rows(d),
        out_shape=jax.ShapeDtypeStruct((b, lp, d), F32),
        compiler_params=_cparams(("parallel", "parallel")),
        name="outproj",
    )(o_f, o_b, main, ydf, yml, h, dnw, w)


def _ffn_kernel(h_ref, nw_ref, wg_ref, wu_ref, wd_ref, o_ref, a_ref):
    x = h_ref[...]
    ms = jnp.mean(x * x, axis=-1, keepdims=True)
    u = (x * lax.rsqrt(ms + EPS) * nw_ref[...]).astype(BF16)
    f = wg_ref.shape[1]
    fc = 256
    for c0 in range(0, f, fc):
        g = _dot(u, wg_ref[:, c0:c0 + fc])
        up = _dot(u, wu_ref[:, c0:c0 + fc])
        a_ref[:, c0:c0 + fc] = (g * jax.nn.sigmoid(g) * up).astype(BF16)
    o_ref[...] = x + _dot(a_ref[...], wd_ref[...])


def _ffn(h2, nw, wg, wu, wd):
    n, d = h2.shape
    f = wg.shape[1]
    tm = 512 if n % 512 == 0 else 128

    def const(a):
        return pl.BlockSpec(a.shape, lambda i: (0, 0), pipeline_mode=pl.Buffered(1))

    return pl.pallas_call(
        _ffn_kernel,
        grid=(n // tm,),
        in_specs=[pl.BlockSpec((tm, d), lambda i: (i, 0)),
                  pl.BlockSpec((1, d), lambda i: (0, 0)),
                  const(wg), const(wu), const(wd)],
        out_specs=pl.BlockSpec((tm, d), lambda i: (i, 0)),
        out_shape=jax.ShapeDtypeStruct((n, d), F32),
        scratch_shapes=[pltpu.VMEM((tm, f), BF16)],
        compiler_params=_cparams(("parallel",)),
        name="ffn",
    )(h2, nw, wg, wu, wd)


def _router_kernel(h_ref, nw_ref, rw_ref, u_ref, idx_ref, gate_ref):
    x = h_ref[...]
    ms = jnp.mean(x * x, axis=-1, keepdims=True)
    u = x * lax.rsqrt(ms + EPS) * nw_ref[...]
    u_ref[...] = u.astype(BF16)
    logits = jnp.dot(u, rw_ref[...], preferred_element_type=F32, precision=lax.Precision.HIGHEST)
    lane = lax.broadcasted_iota(jnp.int32, logits.shape, 1)
    logits = jnp.where(lane < N_EXPERTS, logits, -jnp.inf)
    m1 = jnp.max(logits, axis=-1, keepdims=True)
    i1 = jnp.min(jnp.where(logits == m1, lane, LANES), axis=-1, keepdims=True)
    rest = jnp.where(lane == i1, -jnp.inf, logits)
    m2 = jnp.max(rest, axis=-1, keepdims=True)
    i2 = jnp.min(jnp.where(rest == m2, lane, LANES), axis=-1, keepdims=True)
    e2 = jnp.exp(m2 - m1)
    g1 = 1.0 / (1.0 + e2)
    g2 = e2 / (1.0 + e2)
    idx_ref[...] = jnp.where(lane == 0, i1, jnp.where(lane == 1, i2, 0))
    gate_ref[...] = jnp.where(lane == 0, g1, jnp.where(lane == 1, g2, 0.0))


def _router(h2, nw, rw):
    n, d = h2.shape
    tm = 512 if n % 512 == 0 else 128
    return pl.pallas_call(
        _router_kernel,
        grid=(n // tm,),
        in_specs=[pl.BlockSpec((tm, d), lambda i: (i, 0)),
                  pl.BlockSpec((1, d), lambda i: (0, 0)),
                  pl.BlockSpec((d, LANES), lambda i: (0, 0))],
        out_specs=[pl.BlockSpec((tm, d), lambda i: (i, 0)),
                   pl.BlockSpec((tm, LANES), lambda i: (i, 0)),
                   pl.BlockSpec((tm, LANES), lambda i: (i, 0))],
        out_shape=[jax.ShapeDtypeStruct((n, d), BF16),
                   jax.ShapeDtypeStruct((n, LANES), jnp.int32),
                   jax.ShapeDtypeStruct((n, LANES), F32)],
        compiler_params=_cparams(("parallel",)),
        name="router",
    )(h2, nw, rw)


def _moe_kernel(te_ref, tv_ref, x_ref, wg_ref, wu_ref, wd_ref, o_ref, acc_ref):
    i = pl.program_id(0)
    f = pl.program_id(1)

    @pl.when(f == 0)
    def _():
        acc_ref[...] = jnp.zeros(acc_ref.shape, F32)

    @pl.when(tv_ref[i] > 0)
    def _():
        x = x_ref[...]
        g = _dot(x, wg_ref[0])
        up = _dot(x, wu_ref[0])
        a = (g * jax.nn.sigmoid(g) * up).astype(BF16)
        acc_ref[...] += _dot(a, wd_ref[0])

    @pl.when(f == pl.num_programs(1) - 1)
    def _():
        o_ref[...] = acc_ref[...].astype(o_ref.dtype)


def _moe(x_sorted, tile_expert, tile_valid, wg, wu, wd, tm):
    p, d = x_sorted.shape
    f = wg.shape[2]
    nf = 2
    fh = f // nf
    grid_spec = pltpu.PrefetchScalarGridSpec(
        num_scalar_prefetch=2,
        grid=(p // tm, nf),
        in_specs=[pl.BlockSpec((tm, d), lambda i, j, te, tv: (i, 0)),
                  pl.BlockSpec((1, d, fh), lambda i, j, te, tv: (te[i], 0, j)),
                  pl.BlockSpec((1, d, fh), lambda i, j, te, tv: (te[i], 0, j)),
                  pl.BlockSpec((1, fh, d), lambda i, j, te, tv: (te[i], j, 0))],
        out_specs=pl.BlockSpec((tm, d), lambda i, j, te, tv: (i, 0)),
        scratch_shapes=[pltpu.VMEM((tm, d), F32)],
    )
    return pl.pallas_call(
        _moe_kernel,
        grid_spec=grid_spec,
        out_shape=jax.ShapeDtypeStruct((p, d), BF16),
        compiler_params=_cparams(("arbitrary", "arbitrary")),
        name="moe",
    )(tile_expert, tile_valid, x_sorted, wg, wu, wd)


def _combine_kernel(h_ref, y0_ref, y1_ref, g_ref, o_ref):
    g = g_ref[...]
    o_ref[...] = (h_ref[...] + g[:, 0:1] * y0_ref[...].astype(F32)
                  + g[:, 1:2] * y1_ref[...].astype(F32))


def _combine(h2, y0, y1, gates):
    n, d = h2.shape
    tm = 1024 if n % 1024 == 0 else 128
    row = lambda w: pl.BlockSpec((tm, w), lambda i: (i, 0))
    return pl.pallas_call(
        _combine_kernel,
        grid=(n // tm,),
        in_specs=[row(d), row(d), row(d), row(LANES)],
        out_specs=row(d),
        out_shape=jax.ShapeDtypeStruct((n, d), F32),
        compiler_params=_cparams(("parallel",)),
        name="combine",
    )(h2, y0, y1, gates)


def _spread_rope_cols(w):
    z = jnp.zeros(w.shape[:-1] + (32,), w.dtype)
    return jnp.concatenate([w[..., 0:32], z, w[..., 32:64], z], axis=-1)


def _relayout_w_in(w):
    dnw = 4 * DN_HEADS * DN_D
    n_gate = 4 * DN_HEADS
    rest = w[:, dnw + n_gate:]
    nqk = 4 * DIFF_HEADS * DIFF_D
    nv = DIFF_HEADS * LANES
    diff_qk = rest[:, :nqk]
    lat = rest[:, nqk + nv:rest.shape[1] - MLA_ROPE]
    kr = _spread_rope_cols(rest[:, rest.shape[1] - MLA_ROPE:])
    gates = jnp.concatenate([w[:, dnw:dnw + n_gate],
                             jnp.zeros((w.shape[0], GATE_COLS - n_gate), w.dtype)], axis=1)
    out = jnp.concatenate([diff_qk, lat, kr, w[:, :dnw], gates], axis=1)
    assert out.shape[1] == MAIN_COLS + GATE_COLS
    return out.astype(BF16)


def _relayout_w_uq(w):
    per = MLA_NOPE + MLA_ROPE
    heads = [jnp.concatenate([w[:, h * per:h * per + MLA_NOPE],
                              _spread_rope_cols(w[:, h * per + MLA_NOPE:(h + 1) * per])], axis=1)
             for h in range(MLA_HEADS)]
    return jnp.concatenate(heads, axis=1).astype(BF16)


def _relayout_w_ukv(w):
    per = MLA_NOPE + LANES
    ks = [w[:, h * per:h * per + MLA_NOPE] for h in range(MLA_HEADS)]
    vs = [w[:, h * per + MLA_NOPE:(h + 1) * per] for h in range(MLA_HEADS)]
    return jnp.concatenate(ks + vs, axis=1).astype(BF16)


def _diff_v_weight_t(w):
    c0 = 4 * DN_HEADS * DN_D + 4 * DN_HEADS + 4 * DIFF_HEADS * DIFF_D
    return w[:, c0:c0 + DIFF_HEADS * LANES].T.astype(BF16)


def _qk_norm_vec(w, scale):
    return (jnp.concatenate([w[:MLA_NOPE], _spread_rope_cols(w[MLA_NOPE:])]) * scale)[None, :]


def _t5_bucket(rel):
    half = T5_BUCKETS // 2
    exact = half // 2
    n = jnp.abs(rel)
    large = exact + (jnp.log(jnp.maximum(n, exact).astype(F32) / exact)
                     / math.log(T5_MAX_DIST / exact) * (half - exact)).astype(jnp.int32)
    large = jnp.minimum(large, half - 1)
    return jnp.where(rel > 0, half, 0) + jnp.where(n < exact, n, large)


def _t5_tables(table, tq):
    i = jnp.arange(tq)[None, :]
    j = jnp.arange(LANES)[:, None]
    rels = jnp.stack([j - i + (d * LANES) for d in (-1, 0, 1)])
    table = table.astype(F32) * LOG2E
    bandt = jnp.transpose(table[_t5_bucket(rels)], (3, 0, 1, 2)).astype(BF16)
    far_rel = jnp.array([-T5_MAX_DIST, T5_MAX_DIST], jnp.int32)
    far = jnp.transpose(table[_t5_bucket(far_rel)], (1, 0))
    far = jnp.broadcast_to(far[:, :, None], far.shape + (LANES,))
    return bandt, far


def _rope_tables(lp):
    half = MLA_ROPE // 2
    inv = 1.0 / (ROPE_THETA ** (jnp.arange(half, dtype=F32) / half))
    pos = (jnp.arange(lp, dtype=jnp.int32) - N_PAD).astype(F32)
    ang = pos[:, None] * inv[None, :]
    c, s, z = jnp.cos(ang), jnp.sin(ang), jnp.zeros((lp, half), F32)
    return jnp.concatenate([c, z, c, z], axis=1), jnp.concatenate([-s, z, s, z], axis=1)


def _moe_plan(idx, n, tm):
    e_flat = idx.reshape(-1)
    onehot = (e_flat[:, None] == jnp.arange(N_EXPERTS)[None, :]).astype(jnp.int32)
    counts = jnp.sum(onehot, axis=0)
    rank = jnp.sum((jnp.cumsum(onehot, axis=0) - onehot) * onehot, axis=1)
    tiles_per = (counts + tm - 1) // tm
    tile_end = jnp.cumsum(tiles_per)
    start = (tile_end - tiles_per) * tm
    dest = start[e_flat] + rank
    n_tiles = (2 * n) // tm + N_EXPERTS
    p = n_tiles * tm
    slot_tok = jnp.zeros((p,), jnp.int32).at[dest].set(jnp.arange(2 * n, dtype=jnp.int32) // 2)
    t = jnp.arange(n_tiles, dtype=jnp.int32)
    tile_expert = jnp.minimum(jnp.sum((t[:, None] >= tile_end[None, :]).astype(jnp.int32), axis=1),
                              N_EXPERTS - 1).astype(jnp.int32)
    tile_valid = (t < tile_end[-1]).astype(jnp.int32)
    return slot_tok, dest.reshape(n, 2), tile_expert, tile_valid


def kernel(x, meta_tokens, rel_bias_table, attn_norm_w, w_in, dn_conv_w, dn_a_log, dn_dt_bias, dn_norm_w, diff_q_norm_w, diff_k_norm_w, diff_lambda, diff_subln_w, mla_q_lat_norm_w, mla_w_uq, mla_kv_lat_norm_w, mla_w_ukv, mla_q_norm_w, mla_k_norm_w, w_out, ffn_norm_w, ffn_w_gate, ffn_w_up, ffn_w_down, router_w, moe_w_gate, moe_w_up, moe_w_down):
    bsz, seq, d = x.shape
    depth = w_in.shape[0]
    lp = ROW0 + seq
    n = bsz * lp
    assert lp % LANES == 0 and meta_tokens.shape[0] == N_META

    meta = jnp.broadcast_to(meta_tokens[None].astype(x.dtype), (bsz, N_META, d))
    h = jnp.concatenate([jnp.zeros((bsz, N_PAD, d), x.dtype), meta, x], axis=1)

    cos_t, sin_t = _rope_tables(lp)
    bandt, far = _t5_tables(rel_bias_table, LANES)
    n_chunks = lp // DN_CHUNK

    for l in range(depth):
        main, gate, dvt = _inproj(h, attn_norm_w[l][None, :], _relayout_w_in(w_in[l]),
                                  _diff_v_weight_t(w_in[l]))

        qkv = _dnprep(main, dn_conv_w[l])
        gate_row = jnp.transpose(gate[:, :, 0:16].reshape(bsz, n_chunks, DN_CHUNK, 16), (0, 1, 3, 2))
        neg_a = -jnp.exp(dn_a_log[l].astype(F32)).reshape(-1)
        dtb = dn_dt_bias[l].astype(F32).reshape(-1)
        z8 = jnp.zeros((8,), F32)
        pc = jnp.stack([jnp.concatenate([z8, neg_a, jnp.zeros((GATE_COLS - 16,), F32)]),
                        jnp.concatenate([z8, dtb, jnp.zeros((GATE_COLS - 16,), F32)])])
        pr = jnp.concatenate([jnp.broadcast_to(jnp.concatenate([z8, neg_a])[:, None], (16, DN_CHUNK)),
                              jnp.broadcast_to(jnp.concatenate([z8, dtb])[:, None], (16, DN_CHUNK))], axis=1)
        o_f, o_b = _deltanet(qkv, gate, gate_row, pc, pr)

        dscale = DIFF_D ** -0.5 * LOG2E
        mscale = (MLA_NOPE + MLA_ROPE) ** -0.5 * LOG2E
        dq, dk, mq, mk, mv = _attnprep(
            main,
            (jnp.tile(diff_q_norm_w[l], 2) * dscale)[None, :],
            jnp.tile(diff_k_norm_w[l], 2)[None, :],
            mla_q_lat_norm_w[l][None, :], _relayout_w_uq(mla_w_uq[l]),
            mla_kv_lat_norm_w[l][None, :], _relayout_w_ukv(mla_w_ukv[l]),
            _qk_norm_vec(mla_q_norm_w[l], mscale), _qk_norm_vec(mla_k_norm_w[l], 1.0),
            cos_t, sin_t)
        lam_init = 0.8 - 0.6 * math.exp(-0.3 * l)
        lpar = diff_lambda[l].astype(F32)
        lam = jnp.exp(jnp.sum(lpar[0] * lpar[1])) - jnp.exp(jnp.sum(lpar[2] * lpar[3])) + lam_init
        ydf = _diffattn(dq, dk, dvt, bandt, far, jnp.full((1, LANES), lam, F32),
                        (diff_subln_w[l] * (1.0 - lam_init))[None, :])
        yml = _mlaattn(mq, mk, mv)

        h = _outproj(o_f, o_b, main, ydf, yml, h, dn_norm_w[l][None, :], w_out[l].astype(BF16))

        h2 = h.reshape(n, d)
        if l % 2 == 0:
            i = l // 2
            h2 = _ffn(h2, ffn_norm_w[l][None, :], ffn_w_gate[i].astype(BF16),
                      ffn_w_up[i].astype(BF16), ffn_w_down[i].astype(BF16))
        else:
            i = l // 2
            rw = jnp.concatenate([router_w[i], jnp.zeros((d, LANES - N_EXPERTS), F32)], axis=1)
            u, idx, gates = _router(h2, ffn_norm_w[l][None, :], rw)
            tm = 512 if (2 * n) % 512 == 0 else 128
            slot_tok, tok_slots, tile_expert, tile_valid = _moe_plan(idx[:, 0:2], n, tm)
            x_sorted = jnp.take(u, slot_tok, axis=0)
            y_sorted = _moe(x_sorted, tile_expert, tile_valid,
                            moe_w_gate[i].astype(BF16), moe_w_up[i].astype(BF16),
                            moe_w_down[i].astype(BF16), tm)
            h2 = _combine(h2, jnp.take(y_sorted, tok_slots[:, 0], axis=0),
                          jnp.take(y_sorted, tok_slots[:, 1], axis=0), gates)
        h = h2.reshape(bsz, lp, d)
    return h[:, ROW0:]
```

```python
import functools
import math

import jax
import jax.numpy as jnp
from jax import lax
from jax.experimental import pallas as pl
from jax.experimental.pallas import tpu as pltpu

F32 = jnp.float32
BF16 = jnp.bfloat16

N_META = 16
N_PAD = 112
ROW0 = N_PAD + N_META
DN_HEADS = 4
DN_D = 128
DN_CONV = 5
DN_CHUNK = 64
DIFF_HEADS = 4
DIFF_D = 64
MLA_HEADS = 4
MLA_NOPE = 128
MLA_ROPE = 64
MLA_Q_RANK = 256
MLA_KV_RANK = 128
ROPE_THETA = 10000.0
T5_BUCKETS = 32
T5_MAX_DIST = 128
N_EXPERTS = 8
EPS = 1e-6
NEG_BIG = -1e30
LOG2E = 1.4426950408889634
MASK_LANE = 32
LANES = 128
VMEM_LIMIT = 56 * 1024 * 1024

ATTN_COLS = 1536
DN_COL0 = ATTN_COLS
MAIN_COLS = 3584
GATE_COLS = 128


def _dot(a, b):
    return jnp.dot(a, b, preferred_element_type=F32)


def _dot_nt(a, b):
    return lax.dot_general(a, b, (((1,), (1,)), ((), ())), preferred_element_type=F32)


def _ones_bf16(n):
    return jnp.ones((n, n), BF16)


def _row_tile(lp):
    return 384 if lp % 384 == 0 else 128


def _cparams(sem):
    return pltpu.CompilerParams(dimension_semantics=sem, vmem_limit_bytes=VMEM_LIMIT)


def _inproj_kernel(h_ref, nw_ref, w_ref, wvt_ref, main_ref, gate_ref, vt_ref):
    x = h_ref[0]
    ms = jnp.mean(x * x, axis=-1, keepdims=True)
    u = (x * lax.rsqrt(ms + EPS) * nw_ref[...]).astype(BF16)
    for c0 in range(0, MAIN_COLS, 512):
        main_ref[0, :, c0:c0 + 512] = _dot(u, w_ref[:, c0:c0 + 512]).astype(BF16)
    gate_ref[0] = _dot(u, w_ref[:, MAIN_COLS:MAIN_COLS + GATE_COLS])
    vt_ref[0] = _dot_nt(wvt_ref[...], u).astype(BF16)


def _inproj(h, norm_w, w, wvt):
    b, lp, d = h.shape
    tm = _row_tile(lp)
    ncol = MAIN_COLS + GATE_COLS
    nv = wvt.shape[0]
    return pl.pallas_call(
        _inproj_kernel,
        grid=(b, lp // tm),
        in_specs=[
            pl.BlockSpec((1, tm, d), lambda i, j: (i, j, 0)),
            pl.BlockSpec((1, d), lambda i, j: (0, 0)),
            pl.BlockSpec((d, ncol), lambda i, j: (0, 0)),
            pl.BlockSpec((nv, d), lambda i, j: (0, 0)),
        ],
        out_specs=[
            pl.BlockSpec((1, tm, MAIN_COLS), lambda i, j: (i, j, 0)),
            pl.BlockSpec((1, tm, GATE_COLS), lambda i, j: (i, j, 0)),
            pl.BlockSpec((1, nv, tm), lambda i, j: (i, 0, j)),
        ],
        out_shape=[
            jax.ShapeDtypeStruct((b, lp, MAIN_COLS), BF16),
            jax.ShapeDtypeStruct((b, lp, GATE_COLS), F32),
            jax.ShapeDtypeStruct((b, nv, lp), BF16),
        ],
        compiler_params=_cparams(("parallel", "parallel")),
        name="inproj",
    )(h, norm_w, w, wvt)


def _dnprep_kernel(x_ref, cw_ref, o_ref, xs_ref):
    j = pl.program_id(1)
    lp = x_ref.shape[1]
    halo = 8
    xs_ref[0:halo + N_PAD, :] = jnp.zeros((halo + N_PAD, LANES), F32)
    xs_ref[halo + N_PAD:halo + lp, :] = x_ref[0, N_PAD:, :].astype(F32)
    xs_ref[halo + lp:2 * halo + lp, :] = jnp.zeros((halo, LANES), F32)
    q_scale = jnp.where(j < DN_HEADS, DN_D ** -0.5, 1.0).astype(F32)
    is_qk = j < 2 * DN_HEADS
    rc = 128
    for c in range(lp // rc):
        base = c * rc
        acc = cw_ref[0:1, :] * xs_ref[base + halo - 2:base + halo - 2 + rc, :]
        for t in range(1, DN_CONV):
            off = base + halo - 2 + t
            acc = acc + cw_ref[t:t + 1, :] * xs_ref[off:off + rc, :]
        y = acc * jax.nn.sigmoid(acc)
        ss = jnp.sum(y * y, axis=-1, keepdims=True)
        r = jnp.where(is_qk, lax.rsqrt(ss + EPS) * q_scale, 1.0)
        o_ref[0, base:base + rc, :] = (y * r).astype(BF16)


def _dnprep(main, conv_w):
    b, lp, _ = main.shape
    ncb = 3 * DN_HEADS
    return pl.pallas_call(
        _dnprep_kernel,
        grid=(b, ncb),
        in_specs=[
            pl.BlockSpec((1, lp, LANES), lambda i, j: (i, 0, DN_COL0 // LANES + j)),
            pl.BlockSpec((DN_CONV, LANES), lambda i, j: (0, j)),
        ],
        out_specs=pl.BlockSpec((1, lp, LANES), lambda i, j: (i, 0, j)),
        out_shape=jax.ShapeDtypeStruct((b, lp, ncb * LANES), BF16),
        scratch_shapes=[pltpu.VMEM((lp + 16, LANES), F32)],
        compiler_params=_cparams(("parallel", "parallel")),
        name="dnprep",
    )(main, conv_w)


def _each(fn, *lists):
    return [fn(*args) for args in zip(*lists)]


def _unit_tri_inverse(a_list, row, col):
    c = row.shape[0]
    bf = lambda t: t.astype(BF16)
    eye = (row == col).astype(F32)
    same = lambda n: (row // n) == (col // n)
    blk16 = same(16)
    a0 = _each(lambda a: jnp.where(blk16, a, 0.0), a_list)
    a0b = _each(bf, a0)
    a2b = _each(lambda x: bf(_dot(x, x)), a0b)
    a4b = _each(lambda x: bf(_dot(x, x)), a2b)
    a8b = _each(lambda x: bf(_dot(x, x)), a4b)
    p = _each(lambda x: eye - x, a0)
    for pw in (a2b, a4b, a8b):
        p = _each(lambda x, y: x + _dot(bf(x), y), p, pw)
    size = 16
    while size < c:
        inner, outer = same(size), same(2 * size)
        off = _each(lambda a: bf(jnp.where(outer & (~inner), a, 0.0)), a_list)
        pb = _each(bf, p)
        mid = _each(lambda x, y: bf(_dot(x, y)), pb, off)
        p = _each(lambda x, m, xb: x - _dot(m, xb), p, mid, pb)
        size *= 2
    return p


def _dn_kernel(qf_ref, kf_ref, vf_ref, qb_ref, kb_ref, vb_ref, gcf_ref, gcb_ref,
               grf_ref, grb_ref, pc_ref, pr_ref, of_ref, ob_ref, st_ref):
    s = pl.program_id(1)
    c = DN_CHUNK
    nb = qf_ref.shape[0]

    @pl.when(s == 0)
    def _():
        st_ref[...] = jnp.zeros(st_ref.shape, F32)

    row = lax.broadcasted_iota(jnp.int32, (c, c), 0)
    col = lax.broadcasted_iota(jnp.int32, (c, c), 1)
    negA_c, dtb_c = pc_ref[0:1, :], pc_ref[1:2, :]
    negA_r, dtb_r = pr_ref[:, 0:c], pr_ref[:, c:2 * c]
    bf = lambda t: t.astype(BF16)
    lead = ROW0 // c

    per_dir = []
    for d, (gc_ref, gr_ref) in enumerate(((gcf_ref, grf_ref), (gcb_ref, grb_ref))):
        if d == 0:
            causal = s >= 0
            chunk = s
        else:
            causal = s < lead
            chunk = jnp.where(s < lead, s, (pl.num_programs(1) - 1 + lead) - s)
        sgn = jnp.where(causal, 1, -1)
        m_in = (row - col) * sgn >= 0
        m_strict = (row - col) * sgn > 0
        m_f = m_in.astype(F32)
        m_t = ((col - row) * sgn >= 0).astype(F32)
        pos_c = chunk * c + lax.broadcasted_iota(jnp.int32, (c, LANES), 0)
        live_c = pos_c >= N_PAD
        pos_r = chunk * c + lax.broadcasted_iota(jnp.int32, (16, c), 1)
        gates = []
        for bi in range(nb):
            gcol = gc_ref[bi]
            beta_c = jnp.where(live_c, jax.nn.sigmoid(gcol), 0.0)
            g_c = jnp.where(live_c, negA_c * jax.nn.softplus(gcol + dtb_c), 0.0)
            grow = gr_ref[bi, 0]
            g_r = jnp.where(pos_r >= N_PAD, negA_r * jax.nn.softplus(grow + dtb_r), 0.0)
            gc_c = jnp.dot(m_f, g_c, preferred_element_type=F32, precision=lax.Precision.HIGHEST)
            gc_r = jnp.dot(g_r, m_t, preferred_element_type=F32, precision=lax.Precision.HIGHEST)
            gtot_c = jnp.sum(g_c, axis=0, keepdims=True)
            gates.append(dict(beta_c=beta_c, gc_c=gc_c, gc_r=gc_r, eg_c=jnp.exp(gc_c),
                              ew_c=jnp.exp(gtot_c - gc_c), et_c=jnp.exp(gtot_c)))
        per_dir.append(dict(m_in=m_in, m_strict=m_strict, gates=gates))

    refs = ((qf_ref, kf_ref, vf_ref, of_ref), (qb_ref, kb_ref, vb_ref, ob_ref))
    chains = [(bi, d, h) for bi in range(nb) for d in range(2) for h in range(DN_HEADS)]
    sl = lambda h: slice(h * DN_D, (h + 1) * DN_D)
    bcol = lambda d, h: d * DN_HEADS + h
    acol = lambda d, h: 2 * DN_HEADS + d * DN_HEADS + h
    sidx = lambda bi, d, h: (bi * 2 + d) * DN_HEADS + h
    gate = lambda bi, d, name: per_dir[d]["gates"][bi][name]

    q = [refs[d][0][bi, :, sl(h)] for bi, d, h in chains]
    k = [refs[d][1][bi, :, sl(h)] for bi, d, h in chains]
    v = [refs[d][2][bi, :, sl(h)] for bi, d, h in chains]
    kf = _each(lambda t: t.astype(F32), k)
    beta = [gate(bi, d, "beta_c")[:, bcol(d, h):bcol(d, h) + 1] for bi, d, h in chains]
    eg = [gate(bi, d, "eg_c")[:, acol(d, h):acol(d, h) + 1] for bi, d, h in chains]
    kb = _each(lambda x, b: x * b, kf, beta)
    vb = _each(lambda x, b: x.astype(F32) * b, v, beta)

    def decay_of(bi, d, h):
        m_in = per_dir[d]["m_in"]
        diff = gate(bi, d, "gc_c")[:, acol(d, h):acol(d, h) + 1] - gate(bi, d, "gc_r")[acol(d, h):acol(d, h) + 1, :]
        return jnp.where(m_in, jnp.exp(jnp.where(m_in, diff, 0.0)), 0.0)

    decay = [decay_of(*ch) for ch in chains]
    kq = _each(lambda x, y, z: _dot_nt(jnp.concatenate([bf(x), y], axis=0), z), kb, q, k)
    a = [jnp.where(per_dir[d]["m_strict"], kq_i[0:c] * dec, 0.0)
         for (bi, d, h), kq_i, dec in zip(chains, kq, decay)]
    qk = _each(lambda x, dec: bf(x[c:2 * c] * dec), kq, decay)
    t = _unit_tri_inverse(a, row, col)
    x = _each(lambda x1, x2, e: bf(jnp.concatenate([x1, x2 * e], axis=1)), vb, kb, eg)
    tx = _each(lambda ti, xi: _dot(bf(ti), xi), t, x)
    st = [st_ref[sidx(*ch)] for ch in chains]
    lhs = _each(lambda txi, qi, e: bf(jnp.concatenate([txi[:, DN_D:2 * DN_D], qi.astype(F32) * e], axis=0)),
                tx, q, eg)
    r = _each(lambda l, si: _dot(l, bf(si)), lhs, st)
    v_new = _each(lambda txi, ri: bf(txi[:, 0:DN_D] - ri[0:c]), tx, r)
    out = _each(lambda ri, qki, vn: ri[c:2 * c] + _dot(qki, vn), r, qk, v_new)
    kw = [bf((kfi * gate(bi, d, "ew_c")[:, acol(d, h):acol(d, h) + 1]).T) for (bi, d, h), kfi in zip(chains, kf)]
    upd = _each(_dot, kw, v_new)
    for (bi, d, h), o_i, st_i, u_i in zip(chains, out, st, upd):
        refs[d][3][bi, :, sl(h)] = o_i.astype(BF16)
        st_ref[sidx(bi, d, h)] = st_i * gate(bi, d, "et_c")[:, acol(d, h):acol(d, h) + 1] + u_i


DN_BATCH = 4


def _deltanet(qkv, gate_col, gate_row, pc, pr):
    b, lp, _ = qkv.shape
    n = lp // DN_CHUNK
    hw = DN_HEADS * DN_D
    nb = DN_BATCH if b % DN_BATCH == 0 else 1

    def fwd_chunk(s):
        return s

    lead = ROW0 // DN_CHUNK

    def bwd_chunk(s):
        return jnp.where(s < lead, s, n - 1 + lead - s)

    def spec3(cb, chunk):
        return pl.BlockSpec((nb, DN_CHUNK, hw), lambda i, s: (i, chunk(s), cb))

    in_specs = [spec3(0, fwd_chunk), spec3(1, fwd_chunk), spec3(2, fwd_chunk),
                spec3(0, bwd_chunk), spec3(1, bwd_chunk), spec3(2, bwd_chunk),
                pl.BlockSpec((nb, DN_CHUNK, GATE_COLS), lambda i, s: (i, fwd_chunk(s), 0)),
                pl.BlockSpec((nb, DN_CHUNK, GATE_COLS), lambda i, s: (i, bwd_chunk(s), 0)),
                pl.BlockSpec((nb, 1, 16, DN_CHUNK), lambda i, s: (i, fwd_chunk(s), 0, 0)),
                pl.BlockSpec((nb, 1, 16, DN_CHUNK), lambda i, s: (i, bwd_chunk(s), 0, 0)),
                pl.BlockSpec((2, GATE_COLS), lambda i, s: (0, 0)),
                pl.BlockSpec((16, 2 * DN_CHUNK), lambda i, s: (0, 0))]
    out_specs = [pl.BlockSpec((nb, DN_CHUNK, hw), lambda i, s: (i, fwd_chunk(s), 0)),
                 pl.BlockSpec((nb, DN_CHUNK, hw), lambda i, s: (i, bwd_chunk(s), 0))]
    return pl.pallas_call(
        _dn_kernel,
        grid=(b // nb, n),
        in_specs=in_specs,
        out_specs=out_specs,
        out_shape=[jax.ShapeDtypeStruct((b, lp, hw), BF16)] * 2,
        scratch_shapes=[pltpu.VMEM((nb * 2 * DN_HEADS, DN_D, DN_D), F32)],
        compiler_params=_cparams(("parallel", "arbitrary")),
        name="deltanet",
    )(qkv, qkv, qkv, qkv, qkv, qkv, gate_col, gate_col, gate_row, gate_row, pc, pr)


def _attnprep_kernel(x_ref, dqw_ref, dkw_ref, qlw_ref, wuq_ref, klw_ref, wukv_ref,
                     mqw_ref, mkw_ref, cos_ref, sin_ref,
                     dq_ref, dk_ref, mq_ref, mk_ref, mv_ref):
    ones = _ones_bf16(LANES)
    row = lax.broadcasted_iota(jnp.int32, (LANES, LANES), 0)
    col = lax.broadcasted_iota(jnp.int32, (LANES, LANES), 1)
    half_ones = ((row // DIFF_D) == (col // DIFF_D)).astype(BF16)
    cos_t, sin_t = cos_ref[...], sin_ref[...]
    tm = x_ref.shape[1]
    lane_t = lax.broadcasted_iota(jnp.int32, (tm, LANES), 1)
    row_t = pl.program_id(1) * tm + lax.broadcasted_iota(jnp.int32, (tm, LANES), 0)
    q_flag = jnp.where(lane_t == MASK_LANE, 1.0, 0.0)
    k_flag = jnp.where((lane_t == MASK_LANE) & (row_t < N_PAD), NEG_BIG, 0.0)

    def rope(t):
        return t * cos_t + pltpu.roll(t, 64, 1) * sin_t

    for src, w_ref, dst in ((0, dqw_ref, dq_ref), (512, dkw_ref, dk_ref)):
        for h in range(DIFF_HEADS):
            y = x_ref[0, :, src + h * LANES:src + (h + 1) * LANES].astype(F32)
            ms = _dot((y * y).astype(BF16), half_ones) * (1.0 / DIFF_D)
            dst[0, :, h * LANES:(h + 1) * LANES] = (y * lax.rsqrt(ms + EPS) * w_ref[...]).astype(BF16)

    cq0 = x_ref[0, :, 1024:1152].astype(F32)
    cq1 = x_ref[0, :, 1152:1280].astype(F32)
    ms = _dot((cq0 * cq0 + cq1 * cq1).astype(BF16), ones) * (1.0 / MLA_Q_RANK)
    r = lax.rsqrt(ms + EPS)
    cqn = jnp.concatenate([cq0 * r * qlw_ref[:, 0:LANES], cq1 * r * qlw_ref[:, LANES:2 * LANES]],
                          axis=1).astype(BF16)
    q = _dot(cqn, wuq_ref[...])
    inv_d = 1.0 / (MLA_NOPE + MLA_ROPE)
    for h in range(MLA_HEADS):
        q0 = q[:, 256 * h:256 * h + LANES]
        q1 = q[:, 256 * h + LANES:256 * (h + 1)]
        ms = _dot((q0 * q0 + q1 * q1).astype(BF16), ones) * inv_d
        r = lax.rsqrt(ms + EPS)
        mq_ref[0, :, 256 * h:256 * h + LANES] = (q0 * r * mqw_ref[:, 0:LANES]).astype(BF16)
        mq_ref[0, :, 256 * h + LANES:256 * (h + 1)] = (
            rope(q1 * r * mqw_ref[:, LANES:2 * LANES]) + q_flag).astype(BF16)

    ckv = x_ref[0, :, 1280:1408].astype(F32)
    ms = _dot((ckv * ckv).astype(BF16), ones) * (1.0 / MLA_KV_RANK)
    ckvn = (ckv * lax.rsqrt(ms + EPS) * klw_ref[...]).astype(BF16)
    kv = _dot(ckvn, wukv_ref[...])
    mv_ref[0] = kv[:, 512:1024].astype(BF16)
    kr = x_ref[0, :, 1408:1536].astype(F32)
    kr2 = kr * kr
    for h in range(MLA_HEADS):
        k0 = kv[:, LANES * h:LANES * (h + 1)]
        ms = _dot((k0 * k0 + kr2).astype(BF16), ones) * inv_d
        r = lax.rsqrt(ms + EPS)
        mk_ref[0, :, 256 * h:256 * h + LANES] = (k0 * r * mkw_ref[:, 0:LANES]).astype(BF16)
        mk_ref[0, :, 256 * h + LANES:256 * (h + 1)] = (
            rope(kr * r * mkw_ref[:, LANES:2 * LANES]) + k_flag).astype(BF16)


def _attnprep(main, dqw, dkw, qlw, wuq, klw, wukv, mqw, mkw, cos_t, sin_t):
    b, lp, _ = main.shape
    tm = _row_tile(lp)

    def full(a):
        return pl.BlockSpec(a.shape, lambda i, j: (0,) * a.ndim)

    def rows(width):
        return pl.BlockSpec((1, tm, width), lambda i, j: (i, j, 0))

    return pl.pallas_call(
        _attnprep_kernel,
        grid=(b, lp // tm),
        in_specs=[pl.BlockSpec((1, tm, ATTN_COLS), lambda i, j: (i, j, 0)),
                  full(dqw), full(dkw), full(qlw), full(wuq), full(klw), full(wukv),
                  full(mqw), full(mkw),
                  pl.BlockSpec((tm, LANES), lambda i, j: (j, 0)),
                  pl.BlockSpec((tm, LANES), lambda i, j: (j, 0))],
        out_specs=[rows(512), rows(512), rows(1024), rows(1024), rows(512)],
        out_shape=[jax.ShapeDtypeStruct((b, lp, 512), BF16),
                   jax.ShapeDtypeStruct((b, lp, 512), BF16),
                   jax.ShapeDtypeStruct((b, lp, 1024), BF16),
                   jax.ShapeDtypeStruct((b, lp, 1024), BF16),
                   jax.ShapeDtypeStruct((b, lp, 512), BF16)],
        compiler_params=_cparams(("parallel", "parallel")),
        name="attnprep",
    )(main, dqw, dkw, qlw, wuq, klw, wukv, mqw, mkw, cos_t, sin_t)


KEY_PARTS = 3


def _key_splits(lp, parts=None):
    nblk = lp // LANES
    nparts = min(parts or KEY_PARTS, nblk)
    edges = [(nblk * i // nparts) * LANES for i in range(nparts + 1)]
    return tuple(zip(edges[:-1], edges[1:]))


def _diff_attend_t(score_fns, vt_ref, lam, lp):
    splits = _key_splits(lp)
    tq = LANES
    nt = len(score_fns)
    add = lambda x, y: x + y

    def scores(t):
        return [score_fns[t](k0, k1) for k0, k1 in splits]

    def softmax(st):
        m = functools.reduce(jnp.maximum, [x.max(axis=0, keepdims=True) for x in st])
        pt = [jnp.exp2(x - m) for x in st]
        lt = functools.reduce(add, [x.sum(axis=0, keepdims=True) for x in pt])
        l1, l2 = lt[:, 0:tq], lt[:, tq:2 * tq]
        r = lam * l1 / l2
        return [(x[:, 0:tq] - x[:, tq:2 * tq] * r).astype(BF16) for x in pt], 1.0 / l1

    def values(grp):
        acc = None
        for i, (k0, k1) in enumerate(splits):
            w = grp[0][0][i] if len(grp) == 1 else jnp.concatenate([grp[0][0][i], grp[1][0][i]], axis=1)
            part = _dot(vt_ref[0, :, k0:k1], w)
            acc = part if acc is None else acc + part
        return [acc[:, j * tq:(j + 1) * tq] * grp[j][1] for j in range(len(grp))]

    s = [scores(0)]
    sm, outs = [], []
    for t in range(nt):
        if t + 1 < nt:
            s.append(scores(t + 1))
        sm.append(softmax(s[t]))
        if t % 2 == 1:
            outs.extend(values(sm[t - 1:t + 1]))
    if nt % 2 == 1:
        outs.extend(values(sm[nt - 1:nt]))
    return outs


def _diffattn_kernel(q_ref, k_ref, vt_ref, bandt_ref, far_ref, lam_ref, sw_ref, o_ref, kx_ref):
    qi = pl.program_id(2)
    tq = LANES
    lp = k_ref.shape[1]
    nt = kx_ref.shape[0]
    nblk = lp // LANES
    c_neg = jnp.broadcast_to(far_ref[0, 0:1, :], (LANES, LANES)).astype(BF16)

    @pl.when(qi == 0)
    def _():
        c_pos = jnp.broadcast_to(far_ref[0, 1:2, :], (lp, LANES)).astype(BF16)
        for t in range(nt):
            kx_ref[t, :, 0:LANES] = k_ref[0]
            kx_ref[t, :, LANES:2 * LANES] = c_pos

    lane = lax.broadcasted_iota(jnp.int32, (tq, LANES), 1)
    rowi = lax.broadcasted_iota(jnp.int32, (tq, LANES), 0)
    eye = jnp.where(lane == rowi, 1.0, 0.0).astype(BF16)
    score_fns = []
    for t in range(nt):
        tile = qi * nt + t

        def put(kb, val, t=t):
            @pl.when((kb >= 0) & (kb < nblk))
            def _():
                kx_ref[t, pl.ds(pl.multiple_of(kb * LANES, LANES), LANES), LANES:2 * LANES] = val

        for back in range(nt):
            put(tile - 2 - back, c_neg)
        for dd in range(3):
            put(tile - 1 + dd, bandt_ref[0, dd])

        @pl.when(tile <= nt + 1)
        def _(t=t):
            kx_ref[t, 0:N_PAD, LANES:2 * LANES] = jnp.full((N_PAD, LANES), NEG_BIG, BF16)

        q = q_ref[0, t * tq:(t + 1) * tq, :]
        zero = jnp.zeros_like(q)
        lhs = jnp.concatenate([jnp.concatenate([jnp.where(lane < DIFF_D, q, zero), eye], axis=1),
                               jnp.concatenate([jnp.where(lane >= DIFF_D, q, zero), eye], axis=1)], axis=0)
        score_fns.append(lambda k0, k1, t=t, lhs=lhs: _dot_nt(kx_ref[t, k0:k1, :], lhs))
    for t, ot in enumerate(_diff_attend_t(score_fns, vt_ref, lam_ref[0:1, 0:1], lp)):
        o = ot.T
        ms = jnp.mean(o * o, axis=-1, keepdims=True)
        o_ref[0, t * tq:(t + 1) * tq, :] = (o * lax.rsqrt(ms + EPS) * sw_ref[...]).astype(BF16)


DIFF_TILES = 3


def _diffattn(dq, dk, dvt, bandt, far, lam, sw):
    b, lp, _ = dq.shape
    nblk = lp // LANES
    nt = DIFF_TILES
    rows = nt * LANES
    return pl.pallas_call(
        _diffattn_kernel,
        grid=(b, DIFF_HEADS, pl.cdiv(nblk, nt)),
        in_specs=[pl.BlockSpec((1, rows, LANES), lambda i, h, j: (i, j, h)),
                  pl.BlockSpec((1, lp, LANES), lambda i, h, j: (i, 0, h)),
                  pl.BlockSpec((1, LANES, lp), lambda i, h, j: (i, h, 0)),
                  pl.BlockSpec((1, 3, LANES, LANES), lambda i, h, j: (h, 0, 0, 0)),
                  pl.BlockSpec((1, 2, LANES), lambda i, h, j: (h, 0, 0)),
                  pl.BlockSpec((1, LANES), lambda i, h, j: (0, 0)),
                  pl.BlockSpec((1, LANES), lambda i, h, j: (0, 0))],
        out_specs=pl.BlockSpec((1, rows, LANES), lambda i, h, j: (i, j, h)),
        out_shape=jax.ShapeDtypeStruct((b, lp, DIFF_HEADS * LANES), BF16),
        scratch_shapes=[pltpu.VMEM((nt, lp, 2 * LANES), BF16)],
        compiler_params=_cparams(("parallel", "parallel", "arbitrary")),
        name="diffattn",
    )(dq, dk, dvt, bandt, far, lam, sw)


def _mlaattn_kernel(q_ref, k_ref, v_ref, o_ref, vx_ref):
    lp = k_ref.shape[1]

    @pl.when(pl.program_id(2) == 0)
    def _():
        vx_ref[:, 0:LANES] = v_ref[0]
        vx_ref[:, LANES:2 * LANES] = jnp.ones((lp, LANES), BF16)

    q = q_ref[0]
    splits = _key_splits(lp, MLA_KEY_PARTS)
    s = [_dot_nt(q, k_ref[0, k0:k1, :]) for k0, k1 in splits]
    m = [x.max(axis=1, keepdims=True) for x in s]
    p = [jnp.exp2((x - mi).astype(BF16)) for x, mi in zip(s, m)]
    o = [_dot(pi, vx_ref[k0:k1, :]) for (k0, k1), pi in zip(splits, p)]
    m_all = functools.reduce(jnp.maximum, m)
    acc = functools.reduce(lambda a, b: a + b, [oi * jnp.exp2(mi - m_all) for oi, mi in zip(o, m)])
    o_ref[0] = (acc[:, 0:LANES] / acc[:, LANES:LANES + 1]).astype(BF16)


MLA_KEY_PARTS = 4
MLA_Q_STEPS = 4


def _mlaattn(mq, mk, mv):
    b, lp, _ = mq.shape
    tq = lp // MLA_Q_STEPS if lp % (16 * MLA_Q_STEPS) == 0 else LANES
    return pl.pallas_call(
        _mlaattn_kernel,
        grid=(b, MLA_HEADS, lp // tq),
        in_specs=[pl.BlockSpec((1, tq, 256), lambda i, h, j: (i, j, h)),
                  pl.BlockSpec((1, lp, 256), lambda i, h, j: (i, 0, h)),
                  pl.BlockSpec((1, lp, LANES), lambda i, h, j: (i, 0, h))],
        out_specs=pl.BlockSpec((1, tq, LANES), lambda i, h, j: (i, j, h)),
        out_shape=jax.ShapeDtypeStruct((b, lp, MLA_HEADS * LANES), BF16),
        scratch_shapes=[pltpu.VMEM((lp, 2 * LANES), BF16)],
        compiler_params=_cparams(("parallel", "parallel", "arbitrary")),
        name="mlaattn",
    )(mq, mk, mv)


def _outproj_kernel(of_ref, ob_ref, z_ref, df_ref, ml_ref, h_ref, dnw_ref, w_ref, o_ref):
    ones = _ones_bf16(LANES)
    parts = []
    for h in range(DN_HEADS):
        sl = slice(h * DN_D, (h + 1) * DN_D)
        o = of_ref[0, :, sl].astype(F32) + ob_ref[0, :, sl].astype(F32)
        ms = _dot((o * o).astype(BF16), ones) * (1.0 / DN_D)
        z = z_ref[0, :, sl].astype(F32)
        parts.append((o * lax.rsqrt(ms + EPS) * dnw_ref[...] * (z * jax.nn.sigmoid(z))).astype(BF16))
    ydn = jnp.concatenate(parts, axis=1)
    acc = h_ref[0] + _dot(ydn, w_ref[0:512, :])
    acc = acc + _dot(df_ref[0], w_ref[512:1024, :])
    acc = acc + _dot(ml_ref[0], w_ref[1024:1536, :])
    o_ref[0] = acc


def _outproj(o_f, o_b, main, ydf, yml, h, dnw, w):
    b, lp, d = h.shape
    tm = _row_tile(lp)

    def rows(width, cb=0):
        return pl.BlockSpec((1, tm, width), lambda i, j: (i, j, cb))

    return pl.pallas_call(
        _outproj_kernel,
        grid=(b, lp // tm),
        in_specs=[rows(512), rows(512), rows(512, (DN_COL0 + 1536) // 512), rows(512), rows(512), rows(d),
                  pl.BlockSpec((1, LANES), lambda i, j: (0, 0)),
                  pl.BlockSpec(w.shape, lambda i, j: (0, 0))],
        out_specs=rows(d),
        out_shape=jax.ShapeDtypeStruct((b, lp, d), F32),
        compiler_params=_cparams(("parallel", "parallel")),
        name="outproj",
    )(o_f, o_b, main, ydf, yml, h, dnw, w)


def _ffn_kernel(h_ref, nw_ref, wg_ref, wu_ref, wd_ref, o_ref, a_ref):
    x = h_ref[...]
    ms = jnp.mean(x * x, axis=-1, keepdims=True)
    u = (x * lax.rsqrt(ms + EPS) * nw_ref[...]).astype(BF16)
    f = wg_ref.shape[1]
    fc = 256
    for c0 in range(0, f, fc):
        g = _dot(u, wg_ref[:, c0:c0 + fc])
        up = _dot(u, wu_ref[:, c0:c0 + fc])
        a_ref[:, c0:c0 + fc] = (g * jax.nn.sigmoid(g) * up).astype(BF16)
    o_ref[...] = x + _dot(a_ref[...], wd_ref[...])


def _ffn(h2, nw, wg, wu, wd):
    n, d = h2.shape
    f = wg.shape[1]
    tm = 512 if n % 512 == 0 else 128

    def const(a):
        return pl.BlockSpec(a.shape, lambda i: (0, 0), pipeline_mode=pl.Buffered(1))

    return pl.pallas_call(
        _ffn_kernel,
        grid=(n // tm,),
        in_specs=[pl.BlockSpec((tm, d), lambda i: (i, 0)),
                  pl.BlockSpec((1, d), lambda i: (0, 0)),
                  const(wg), const(wu), const(wd)],
        out_specs=pl.BlockSpec((tm, d), lambda i: (i, 0)),
        out_shape=jax.ShapeDtypeStruct((n, d), F32),
        scratch_shapes=[pltpu.VMEM((tm, f), BF16)],
        compiler_params=_cparams(("parallel",)),
        name="ffn",
    )(h2, nw, wg, wu, wd)


def _router_kernel(h_ref, nw_ref, rw_ref, u_ref, idx_ref, gate_ref):
    x = h_ref[...]
    ms = jnp.mean(x * x, axis=-1, keepdims=True)
    u = x * lax.rsqrt(ms + EPS) * nw_ref[...]
    u_ref[...] = u.astype(BF16)
    logits = jnp.dot(u, rw_ref[...], preferred_element_type=F32, precision=lax.Precision.HIGHEST)
    lane = lax.broadcasted_iota(jnp.int32, logits.shape, 1)
    logits = jnp.where(lane < N_EXPERTS, logits, -jnp.inf)
    m1 = jnp.max(logits, axis=-1, keepdims=True)
    i1 = jnp.min(jnp.where(logits == m1, lane, LANES), axis=-1, keepdims=True)
    rest = jnp.where(lane == i1, -jnp.inf, logits)
    m2 = jnp.max(rest, axis=-1, keepdims=True)
    i2 = jnp.min(jnp.where(rest == m2, lane, LANES), axis=-1, keepdims=True)
    e2 = jnp.exp(m2 - m1)
    g1 = 1.0 / (1.0 + e2)
    g2 = e2 / (1.0 + e2)
    idx_ref[...] = jnp.where(lane == 0, i1, jnp.where(lane == 1, i2, 0))
    gate_ref[...] = jnp.where(lane == 0, g1, jnp.where(lane == 1, g2, 0.0))


def _router(h2, nw, rw):
    n, d = h2.shape
    tm = 512 if n % 512 == 0 else 128
    return pl.pallas_call(
        _router_kernel,
        grid=(n // tm,),
        in_specs=[pl.BlockSpec((tm, d), lambda i: (i, 0)),
                  pl.BlockSpec((1, d), lambda i: (0, 0)),
                  pl.BlockSpec((d, LANES), lambda i: (0, 0))],
        out_specs=[pl.BlockSpec((tm, d), lambda i: (i, 0)),
                   pl.BlockSpec((tm, LANES), lambda i: (i, 0)),
                   pl.BlockSpec((tm, LANES), lambda i: (i, 0))],
        out_shape=[jax.ShapeDtypeStruct((n, d), BF16),
                   jax.ShapeDtypeStruct((n, LANES), jnp.int32),
                   jax.ShapeDtypeStruct((n, LANES), F32)],
        compiler_params=_cparams(("parallel",)),
        name="router",
    )(h2, nw, rw)


def _moe_kernel(te_ref, tv_ref, x_ref, wg_ref, wu_ref, wd_ref, o_ref, acc_ref):
    i = pl.program_id(0)
    f = pl.program_id(1)

    @pl.when(f == 0)
    def _():
        acc_ref[...] = jnp.zeros(acc_ref.shape, F32)

    @pl.when(tv_ref[i] > 0)
    def _():
        x = x_ref[...]
        g = _dot(x, wg_ref[0])
        up = _dot(x, wu_ref[0])
        a = (g * jax.nn.sigmoid(g) * up).astype(BF16)
        acc_ref[...] += _dot(a, wd_ref[0])

    @pl.when(f == pl.num_programs(1) - 1)
    def _():
        o_ref[...] = acc_ref[...].astype(o_ref.dtype)


MOE_TM = 512
MOE_F_STEPS = 2


def _moe(x_sorted, tile_expert, tile_valid, wg, wu, wd, tm):
    p, d = x_sorted.shape
    f = wg.shape[2]
    nf = MOE_F_STEPS
    fh = f // nf
    grid_spec = pltpu.PrefetchScalarGridSpec(
        num_scalar_prefetch=2,
        grid=(p // tm, nf),
        in_specs=[pl.BlockSpec((tm, d), lambda i, j, te, tv: (i, 0)),
                  pl.BlockSpec((1, d, fh), lambda i, j, te, tv: (te[i], 0, j)),
                  pl.BlockSpec((1, d, fh), lambda i, j, te, tv: (te[i], 0, j)),
                  pl.BlockSpec((1, fh, d), lambda i, j, te, tv: (te[i], j, 0))],
        out_specs=pl.BlockSpec((tm, d), lambda i, j, te, tv: (i, 0)),
        scratch_shapes=[pltpu.VMEM((tm, d), F32)],
    )
    return pl.pallas_call(
        _moe_kernel,
        grid_spec=grid_spec,
        out_shape=jax.ShapeDtypeStruct((p, d), BF16),
        compiler_params=_cparams(("arbitrary", "arbitrary")),
        name="moe",
    )(tile_expert, tile_valid, x_sorted, wg, wu, wd)


def _combine_kernel(h_ref, y0_ref, y1_ref, g_ref, o_ref):
    g = g_ref[...]
    o_ref[...] = (h_ref[...] + g[:, 0:1] * y0_ref[...].astype(F32)
                  + g[:, 1:2] * y1_ref[...].astype(F32))


def _combine(h2, y0, y1, gates):
    n, d = h2.shape
    tm = 1024 if n % 1024 == 0 else 128
    row = lambda w: pl.BlockSpec((tm, w), lambda i: (i, 0))
    return pl.pallas_call(
        _combine_kernel,
        grid=(n // tm,),
        in_specs=[row(d), row(d), row(d), row(LANES)],
        out_specs=row(d),
        out_shape=jax.ShapeDtypeStruct((n, d), F32),
        compiler_params=_cparams(("parallel",)),
        name="combine",
    )(h2, y0, y1, gates)


def _spread_rope_cols(w):
    z = jnp.zeros(w.shape[:-1] + (32,), w.dtype)
    return jnp.concatenate([w[..., 0:32], z, w[..., 32:64], z], axis=-1)


def _relayout_w_in(w):
    dnw = 4 * DN_HEADS * DN_D
    n_gate = 4 * DN_HEADS
    rest = w[:, dnw + n_gate:]
    nqk = 4 * DIFF_HEADS * DIFF_D
    nv = DIFF_HEADS * LANES
    diff_qk = rest[:, :nqk]
    lat = rest[:, nqk + nv:rest.shape[1] - MLA_ROPE]
    kr = _spread_rope_cols(rest[:, rest.shape[1] - MLA_ROPE:])
    gates = jnp.concatenate([w[:, dnw:dnw + n_gate],
                             jnp.zeros((w.shape[0], GATE_COLS - n_gate), w.dtype)], axis=1)
    out = jnp.concatenate([diff_qk, lat, kr, w[:, :dnw], gates], axis=1)
    assert out.shape[1] == MAIN_COLS + GATE_COLS
    return out.astype(BF16)


def _relayout_w_uq(w):
    per = MLA_NOPE + MLA_ROPE
    heads = [jnp.concatenate([w[:, h * per:h * per + MLA_NOPE],
                              _spread_rope_cols(w[:, h * per + MLA_NOPE:(h + 1) * per])], axis=1)
             for h in range(MLA_HEADS)]
    return jnp.concatenate(heads, axis=1).astype(BF16)


def _relayout_w_ukv(w):
    per = MLA_NOPE + LANES
    ks = [w[:, h * per:h * per + MLA_NOPE] for h in range(MLA_HEADS)]
    vs = [w[:, h * per + MLA_NOPE:(h + 1) * per] for h in range(MLA_HEADS)]
    return jnp.concatenate(ks + vs, axis=1).astype(BF16)


def _diff_v_weight_t(w):
    c0 = 4 * DN_HEADS * DN_D + 4 * DN_HEADS + 4 * DIFF_HEADS * DIFF_D
    return w[:, c0:c0 + DIFF_HEADS * LANES].T.astype(BF16)


def _qk_norm_vec(w, scale):
    return (jnp.concatenate([w[:MLA_NOPE], _spread_rope_cols(w[MLA_NOPE:])]) * scale)[None, :]


def _t5_bucket(rel):
    half = T5_BUCKETS // 2
    exact = half // 2
    n = jnp.abs(rel)
    large = exact + (jnp.log(jnp.maximum(n, exact).astype(F32) / exact)
                     / math.log(T5_MAX_DIST / exact) * (half - exact)).astype(jnp.int32)
    large = jnp.minimum(large, half - 1)
    return jnp.where(rel > 0, half, 0) + jnp.where(n < exact, n, large)


def _t5_tables(table, tq):
    i = jnp.arange(tq)[None, :]
    j = jnp.arange(LANES)[:, None]
    rels = jnp.stack([j - i + (d * LANES) for d in (-1, 0, 1)])
    table = table.astype(F32) * LOG2E
    bandt = jnp.transpose(table[_t5_bucket(rels)], (3, 0, 1, 2)).astype(BF16)
    far_rel = jnp.array([-T5_MAX_DIST, T5_MAX_DIST], jnp.int32)
    far = jnp.transpose(table[_t5_bucket(far_rel)], (1, 0))
    far = jnp.broadcast_to(far[:, :, None], far.shape + (LANES,))
    return bandt, far


def _rope_tables(lp):
    half = MLA_ROPE // 2
    inv = 1.0 / (ROPE_THETA ** (jnp.arange(half, dtype=F32) / half))
    pos = (jnp.arange(lp, dtype=jnp.int32) - N_PAD).astype(F32)
    ang = pos[:, None] * inv[None, :]
    c, s, z = jnp.cos(ang), jnp.sin(ang), jnp.zeros((lp, half), F32)
    return jnp.concatenate([c, z, c, z], axis=1), jnp.concatenate([-s, z, s, z], axis=1)


def _exclusive_cumsum(onehot):
    rows, cols = onehot.shape
    blk = 256 if rows % 256 == 0 else rows
    x = onehot.reshape(rows // blk, blk, cols)
    tri = (jnp.arange(blk)[:, None] > jnp.arange(blk)[None, :]).astype(BF16)
    within = jnp.einsum("ij,bjk->bik", tri, x.astype(BF16), preferred_element_type=F32).astype(jnp.int32)
    totals = jnp.sum(x, axis=1)
    offsets = jnp.cumsum(totals, axis=0) - totals
    return (within + offsets[:, None, :]).reshape(rows, cols)


def _moe_plan(idx, n, tm):
    e_flat = idx.reshape(-1)
    onehot = (e_flat[:, None] == jnp.arange(N_EXPERTS)[None, :]).astype(jnp.int32)
    counts = jnp.sum(onehot, axis=0)
    rank = jnp.sum(_exclusive_cumsum(onehot) * onehot, axis=1)
    tiles_per = (counts + tm - 1) // tm
    tile_end = jnp.cumsum(tiles_per)
    start = (tile_end - tiles_per) * tm
    dest = start[e_flat] + rank
    n_tiles = (2 * n) // tm + N_EXPERTS
    p = n_tiles * tm
    slot_tok = jnp.zeros((p,), jnp.int32).at[dest].set(jnp.arange(2 * n, dtype=jnp.int32) // 2)
    t = jnp.arange(n_tiles, dtype=jnp.int32)
    tile_expert = jnp.minimum(jnp.sum((t[:, None] >= tile_end[None, :]).astype(jnp.int32), axis=1),
                              N_EXPERTS - 1).astype(jnp.int32)
    tile_valid = (t < tile_end[-1]).astype(jnp.int32)
    return slot_tok, dest.reshape(n, 2), tile_expert, tile_valid


def kernel(x, meta_tokens, rel_bias_table, attn_norm_w, w_in, dn_conv_w, dn_a_log, dn_dt_bias, dn_norm_w, diff_q_norm_w, diff_k_norm_w, diff_lambda, diff_subln_w, mla_q_lat_norm_w, mla_w_uq, mla_kv_lat_norm_w, mla_w_ukv, mla_q_norm_w, mla_k_norm_w, w_out, ffn_norm_w, ffn_w_gate, ffn_w_up, ffn_w_down, router_w, moe_w_gate, moe_w_up, moe_w_down):
    bsz, seq, d = x.shape
    depth = w_in.shape[0]
    lp = ROW0 + seq
    n = bsz * lp
    assert lp % LANES == 0 and meta_tokens.shape[0] == N_META

    meta = jnp.broadcast_to(meta_tokens[None].astype(x.dtype), (bsz, N_META, d))
    h = jnp.concatenate([jnp.zeros((bsz, N_PAD, d), x.dtype), meta, x], axis=1)

    cos_t, sin_t = _rope_tables(lp)
    bandt, far = _t5_tables(rel_bias_table, LANES)
    n_chunks = lp // DN_CHUNK

    for l in range(depth):
        main, gate, dvt = _inproj(h, attn_norm_w[l][None, :], _relayout_w_in(w_in[l]),
                                  _diff_v_weight_t(w_in[l]))

        qkv = _dnprep(main, dn_conv_w[l])
        gate_row = jnp.transpose(gate[:, :, 0:16].reshape(bsz, n_chunks, DN_CHUNK, 16), (0, 1, 3, 2))
        neg_a = -jnp.exp(dn_a_log[l].astype(F32)).reshape(-1)
        dtb = dn_dt_bias[l].astype(F32).reshape(-1)
        z8 = jnp.zeros((8,), F32)
        pc = jnp.stack([jnp.concatenate([z8, neg_a, jnp.zeros((GATE_COLS - 16,), F32)]),
                        jnp.concatenate([z8, dtb, jnp.zeros((GATE_COLS - 16,), F32)])])
        pr = jnp.concatenate([jnp.broadcast_to(jnp.concatenate([z8, neg_a])[:, None], (16, DN_CHUNK)),
                              jnp.broadcast_to(jnp.concatenate([z8, dtb])[:, None], (16, DN_CHUNK))], axis=1)
        o_f, o_b = _deltanet(qkv, gate, gate_row, pc, pr)

        dscale = DIFF_D ** -0.5 * LOG2E
        mscale = (MLA_NOPE + MLA_ROPE) ** -0.5 * LOG2E
        dq, dk, mq, mk, mv = _attnprep(
            main,
            (jnp.tile(diff_q_norm_w[l], 2) * dscale)[None, :],
            jnp.tile(diff_k_norm_w[l], 2)[None, :],
            mla_q_lat_norm_w[l][None, :], _relayout_w_uq(mla_w_uq[l]),
            mla_kv_lat_norm_w[l][None, :], _relayout_w_ukv(mla_w_ukv[l]),
            _qk_norm_vec(mla_q_norm_w[l], mscale), _qk_norm_vec(mla_k_norm_w[l], 1.0),
            cos_t, sin_t)
        lam_init = 0.8 - 0.6 * math.exp(-0.3 * l)
        lpar = diff_lambda[l].astype(F32)
        lam = jnp.exp(jnp.sum(lpar[0] * lpar[1])) - jnp.exp(jnp.sum(lpar[2] * lpar[3])) + lam_init
        ydf = _diffattn(dq, dk, dvt, bandt, far, jnp.full((1, LANES), lam, F32),
                        (diff_subln_w[l] * (1.0 - lam_init))[None, :])
        yml = _mlaattn(mq, mk, mv)

        h = _outproj(o_f, o_b, main, ydf, yml, h, dn_norm_w[l][None, :], w_out[l].astype(BF16))

        h2 = h.reshape(n, d)
        if l % 2 == 0:
            i = l // 2
            h2 = _ffn(h2, ffn_norm_w[l][None, :], ffn_w_gate[i].astype(BF16),
                      ffn_w_up[i].astype(BF16), ffn_w_down[i].astype(BF16))
        else:
            i = l // 2
            rw = jnp.concatenate([router_w[i], jnp.zeros((d, LANES - N_EXPERTS), F32)], axis=1)
            u, idx, gates = _router(h2, ffn_norm_w[l][None, :], rw)
            tm = MOE_TM if (2 * n) % MOE_TM == 0 else 128
            slot_tok, tok_slots, tile_expert, tile_valid = _moe_plan(idx[:, 0:2], n, tm)
            x_sorted = jnp.take(u, slot_tok, axis=0)
            y_sorted = _moe(x_sorted, tile_expert, tile_valid, moe_w_gate[i].astype(BF16),
                            moe_w_up[i].astype(BF16), moe_w_down[i].astype(BF16), tm)
            h2 = _combine(h2, jnp.take(y_sorted, tok_slots[:, 0], axis=0),
                          jnp.take(y_sorted, tok_slots[:, 1], axis=0), gates)
        h = h2.reshape(bsz, lp, d)
    return h[:, ROW0:]
```

```python
import functools
import math

import jax
import jax.numpy as jnp
from jax import lax
from jax.experimental import pallas as pl
from jax.experimental.pallas import tpu as pltpu

F32 = jnp.float32
BF16 = jnp.bfloat16

N_META = 16
N_PAD = 112
ROW0 = N_PAD + N_META
DN_HEADS = 4
DN_D = 128
DN_CONV = 5
DN_CHUNK = 64
DIFF_HEADS = 4
DIFF_D = 64
MLA_HEADS = 4
MLA_NOPE = 128
MLA_ROPE = 64
MLA_Q_RANK = 256
MLA_KV_RANK = 128
ROPE_THETA = 10000.0
T5_BUCKETS = 32
T5_MAX_DIST = 128
N_EXPERTS = 8
EPS = 1e-6
NEG_BIG = -1e30
LOG2E = 1.4426950408889634
MASK_LANE = 32
LANES = 128
VMEM_LIMIT = 56 * 1024 * 1024

ATTN_COLS = 1536
DN_COL0 = ATTN_COLS
MAIN_COLS = 3584
GATE_COLS = 128


def _dot(a, b):
    return jnp.dot(a, b, preferred_element_type=F32)


def _dot_nt(a, b):
    return lax.dot_general(a, b, (((1,), (1,)), ((), ())), preferred_element_type=F32)


def _ones_bf16(n):
    return jnp.ones((n, n), BF16)


def _row_tile(lp):
    return 384 if lp % 384 == 0 else 128


def _cparams(sem):
    return pltpu.CompilerParams(dimension_semantics=sem, vmem_limit_bytes=VMEM_LIMIT)


def _inproj_kernel(h_ref, nw_ref, w_ref, wvt_ref, main_ref, gate_ref, vt_ref):
    x = h_ref[0]
    ms = jnp.mean(x * x, axis=-1, keepdims=True)
    u = (x * lax.rsqrt(ms + EPS) * nw_ref[...]).astype(BF16)
    for c0 in range(0, MAIN_COLS, 512):
        main_ref[0, :, c0:c0 + 512] = _dot(u, w_ref[:, c0:c0 + 512]).astype(BF16)
    gate_ref[0] = _dot(u, w_ref[:, MAIN_COLS:MAIN_COLS + GATE_COLS])
    vt_ref[0] = _dot_nt(wvt_ref[...], u).astype(BF16)


def _inproj(h, norm_w, w, wvt):
    b, lp, d = h.shape
    tm = _row_tile(lp)
    ncol = MAIN_COLS + GATE_COLS
    nv = wvt.shape[0]
    return pl.pallas_call(
        _inproj_kernel,
        grid=(b, lp // tm),
        in_specs=[
            pl.BlockSpec((1, tm, d), lambda i, j: (i, j, 0)),
            pl.BlockSpec((1, d), lambda i, j: (0, 0)),
            pl.BlockSpec((d, ncol), lambda i, j: (0, 0)),
            pl.BlockSpec((nv, d), lambda i, j: (0, 0)),
        ],
        out_specs=[
            pl.BlockSpec((1, tm, MAIN_COLS), lambda i, j: (i, j, 0)),
            pl.BlockSpec((1, tm, GATE_COLS), lambda i, j: (i, j, 0)),
            pl.BlockSpec((1, nv, tm), lambda i, j: (i, 0, j)),
        ],
        out_shape=[
            jax.ShapeDtypeStruct((b, lp, MAIN_COLS), BF16),
            jax.ShapeDtypeStruct((b, lp, GATE_COLS), F32),
            jax.ShapeDtypeStruct((b, nv, lp), BF16),
        ],
        compiler_params=_cparams(("parallel", "parallel")),
        name="inproj",
    )(h, norm_w, w, wvt)


def _dnprep_kernel(x_ref, cw_ref, o_ref, xs_ref):
    j = pl.program_id(1)
    lp = x_ref.shape[1]
    halo = 8
    xs_ref[0:halo + N_PAD, :] = jnp.zeros((halo + N_PAD, LANES), F32)
    xs_ref[halo + N_PAD:halo + lp, :] = x_ref[0, N_PAD:, :].astype(F32)
    xs_ref[halo + lp:2 * halo + lp, :] = jnp.zeros((halo, LANES), F32)
    q_scale = jnp.where(j < DN_HEADS, DN_D ** -0.5, 1.0).astype(F32)
    is_qk = j < 2 * DN_HEADS
    rc = 128
    for c in range(lp // rc):
        base = c * rc
        acc = cw_ref[0:1, :] * xs_ref[base + halo - 2:base + halo - 2 + rc, :]
        for t in range(1, DN_CONV):
            off = base + halo - 2 + t
            acc = acc + cw_ref[t:t + 1, :] * xs_ref[off:off + rc, :]
        y = acc * jax.nn.sigmoid(acc)
        ss = jnp.sum(y * y, axis=-1, keepdims=True)
        r = jnp.where(is_qk, lax.rsqrt(ss + EPS) * q_scale, 1.0)
        o_ref[0, base:base + rc, :] = (y * r).astype(BF16)


def _dnprep(main, conv_w):
    b, lp, _ = main.shape
    ncb = 3 * DN_HEADS
    return pl.pallas_call(
        _dnprep_kernel,
        grid=(b, ncb),
        in_specs=[
            pl.BlockSpec((1, lp, LANES), lambda i, j: (i, 0, DN_COL0 // LANES + j)),
            pl.BlockSpec((DN_CONV, LANES), lambda i, j: (0, j)),
        ],
        out_specs=pl.BlockSpec((1, lp, LANES), lambda i, j: (i, 0, j)),
        out_shape=jax.ShapeDtypeStruct((b, lp, ncb * LANES), BF16),
        scratch_shapes=[pltpu.VMEM((lp + 16, LANES), F32)],
        compiler_params=_cparams(("parallel", "parallel")),
        name="dnprep",
    )(main, conv_w)


def _each(fn, *lists):
    return [fn(*args) for args in zip(*lists)]


def _unit_tri_inverse(a_list, row, col, hi):
    c = row.shape[0]
    bf = lambda t: t.astype(BF16)

    def mm(x, y):
        z = jnp.zeros_like(y)
        return _dot(x, jnp.concatenate([jnp.where(hi, z, y), jnp.where(hi, y, z)], axis=0))

    eye = (row == col).astype(F32)
    same = lambda n: (row // n) == (col // n)
    blk16 = same(16)
    a0 = _each(lambda a: jnp.where(blk16, a, 0.0), a_list)
    a0b = _each(bf, a0)
    a2b = _each(lambda x: bf(mm(x, x)), a0b)
    a4b = _each(lambda x: bf(mm(x, x)), a2b)
    a8b = _each(lambda x: bf(mm(x, x)), a4b)
    p = _each(lambda x: eye - x, a0)
    for pw in (a2b, a4b, a8b):
        p = _each(lambda x, y: x + mm(bf(x), y), p, pw)
    size = 16
    while size < c:
        inner, outer = same(size), same(2 * size)
        off = _each(lambda a: bf(jnp.where(outer & (~inner), a, 0.0)), a_list)
        pb = _each(bf, p)
        mid = _each(lambda x, y: bf(mm(x, y)), pb, off)
        p = _each(lambda x, m, xb: x - mm(m, xb), p, mid, pb)
        size *= 2
    return p


def _dn_kernel(qf_ref, kf_ref, vf_ref, qb_ref, kb_ref, vb_ref, gcf_ref, gcb_ref,
               grf_ref, grb_ref, pc_ref, pr_ref, of_ref, ob_ref, st_ref):
    s = pl.program_id(1)
    c = DN_CHUNK
    nb = qf_ref.shape[0]

    @pl.when(s == 0)
    def _():
        st_ref[...] = jnp.zeros(st_ref.shape, F32)

    row = lax.broadcasted_iota(jnp.int32, (c, c), 0)
    col = lax.broadcasted_iota(jnp.int32, (c, c), 1)
    negA_c, dtb_c = pc_ref[0:1, :], pc_ref[1:2, :]
    negA_r, dtb_r = pr_ref[:, 0:c], pr_ref[:, c:2 * c]
    bf = lambda t: t.astype(BF16)
    lead = ROW0 // c

    per_dir = []
    for d, (gc_ref, gr_ref) in enumerate(((gcf_ref, grf_ref), (gcb_ref, grb_ref))):
        if d == 0:
            causal = s >= 0
            chunk = s
        else:
            causal = s < lead
            chunk = jnp.where(s < lead, s, (pl.num_programs(1) - 1 + lead) - s)
        sgn = jnp.where(causal, 1, -1)
        m_in = (row - col) * sgn >= 0
        m_strict = (row - col) * sgn > 0
        m_f = m_in.astype(F32)
        m_t = ((col - row) * sgn >= 0).astype(F32)
        pos_c = chunk * c + lax.broadcasted_iota(jnp.int32, (c, LANES), 0)
        live_c = pos_c >= N_PAD
        pos_r = chunk * c + lax.broadcasted_iota(jnp.int32, (16, c), 1)
        gates = []
        for bi in range(nb):
            gcol = gc_ref[bi]
            beta_c = jnp.where(live_c, jax.nn.sigmoid(gcol), 0.0)
            g_c = jnp.where(live_c, negA_c * jax.nn.softplus(gcol + dtb_c), 0.0)
            grow = gr_ref[bi, 0]
            g_r = jnp.where(pos_r >= N_PAD, negA_r * jax.nn.softplus(grow + dtb_r), 0.0)
            gc_c = jnp.dot(m_f, g_c, preferred_element_type=F32, precision=lax.Precision.HIGHEST)
            gc_r = jnp.dot(g_r, m_t, preferred_element_type=F32, precision=lax.Precision.HIGHEST)
            gtot_c = jnp.sum(g_c, axis=0, keepdims=True)
            gates.append(dict(beta_c=beta_c, gc_c=gc_c, gc_r=gc_r, eg_c=jnp.exp(gc_c),
                              ew_c=jnp.exp(gtot_c - gc_c), et_c=jnp.exp(gtot_c)))
        per_dir.append(dict(sgn=sgn, gates=gates))

    refs = ((qf_ref, kf_ref, vf_ref, of_ref), (qb_ref, kb_ref, vb_ref, ob_ref))
    chains = [(bi, d, h) for bi in range(nb) for d in range(2) for h in range(DN_HEADS)]
    sl = lambda h: slice(h * DN_D, (h + 1) * DN_D)
    bcol = lambda d, h: d * DN_HEADS + h
    acol = lambda d, h: 2 * DN_HEADS + d * DN_HEADS + h
    sidx = lambda bi, d, h: (bi * 2 + d) * DN_HEADS + h
    gate = lambda bi, d, name: per_dir[d]["gates"][bi][name]

    q = [refs[d][0][bi, :, sl(h)] for bi, d, h in chains]
    k = [refs[d][1][bi, :, sl(h)] for bi, d, h in chains]
    v = [refs[d][2][bi, :, sl(h)] for bi, d, h in chains]
    kf = _each(lambda t: t.astype(F32), k)
    beta = [gate(bi, d, "beta_c")[:, bcol(d, h):bcol(d, h) + 1] for bi, d, h in chains]
    eg = [gate(bi, d, "eg_c")[:, acol(d, h):acol(d, h) + 1] for bi, d, h in chains]
    kb = _each(lambda x, b: x * b, kf, beta)
    vb = _each(lambda x, b: x.astype(F32) * b, v, beta)

    assert 2 * c == LANES and DN_HEADS % 2 == 0
    pairs = [(2 * i, 2 * i + 1) for i in range(len(chains) // 2)]
    rowp = lax.broadcasted_iota(jnp.int32, (c, LANES), 0)
    lanep = lax.broadcasted_iota(jnp.int32, (c, LANES), 1)
    hi = lanep >= c
    colp = jnp.where(hi, lanep - c, lanep)
    zero_kd = jnp.zeros((c, DN_D), BF16)
    cat0 = lambda *xs: jnp.concatenate(xs, axis=0)
    cat1 = lambda *xs: jnp.concatenate(xs, axis=1)

    def decay_of(ia, ib):
        (bi, d, ha), (_, _, hb) = chains[ia], chains[ib]
        m_in = (rowp - colp) * per_dir[d]["sgn"] >= 0
        gc_c, gc_r = gate(bi, d, "gc_c"), gate(bi, d, "gc_r")
        colb = jnp.where(hi, gc_c[:, acol(d, hb):acol(d, hb) + 1], gc_c[:, acol(d, ha):acol(d, ha) + 1])
        rowb = cat1(gc_r[acol(d, ha):acol(d, ha) + 1, :], gc_r[acol(d, hb):acol(d, hb) + 1, :])
        return jnp.where(m_in, jnp.exp(jnp.where(m_in, colb - rowb, 0.0)), 0.0)

    decay = [decay_of(ia, ib) for ia, ib in pairs]
    kq = [_dot_nt(cat1(cat0(bf(kb[ia]), q[ia]), cat0(bf(kb[ib]), q[ib])),
                  cat0(cat1(k[ia], zero_kd), cat1(zero_kd, k[ib]))) for ia, ib in pairs]
    a = [jnp.where((rowp - colp) * per_dir[chains[ia][1]]["sgn"] > 0, kq_i[0:c] * dec, 0.0)
         for (ia, ib), kq_i, dec in zip(pairs, kq, decay)]
    qk = _each(lambda x, dec: bf(x[c:2 * c] * dec), kq, decay)
    t = _unit_tri_inverse(a, rowp, colp, hi)
    x = _each(lambda x1, x2, e: bf(jnp.concatenate([x1, x2 * e], axis=1)), vb, kb, eg)
    zero_x = jnp.zeros((c, 2 * DN_D), BF16)
    txp = [_dot(bf(ti), cat0(cat1(x[ia], zero_x), cat1(zero_x, x[ib]))) for (ia, ib), ti in zip(pairs, t)]
    tx = [half for r2 in txp for half in (r2[:, 0:2 * DN_D], r2[:, 2 * DN_D:4 * DN_D])]
    st = [st_ref[sidx(*ch)] for ch in chains]
    lhs = _each(lambda txi, qi, e: bf(jnp.concatenate([txi[:, DN_D:2 * DN_D], qi.astype(F32) * e], axis=0)),
                tx, q, eg)
    r = _each(lambda l, si: _dot(l, bf(si)), lhs, st)
    v_new = _each(lambda txi, ri: bf(txi[:, 0:DN_D] - ri[0:c]), tx, r)
    qkv = [_dot(qki, cat0(cat1(v_new[ia], zero_kd), cat1(zero_kd, v_new[ib]))) for (ia, ib), qki in zip(pairs, qk)]
    qkv = [half for o2 in qkv for half in (o2[:, 0:DN_D], o2[:, DN_D:2 * DN_D])]
    out = _each(lambda ri, o_i: ri[c:2 * c] + o_i, r, qkv)
    kw = [bf((kfi * gate(bi, d, "ew_c")[:, acol(d, h):acol(d, h) + 1]).T) for (bi, d, h), kfi in zip(chains, kf)]
    upd = _each(_dot, kw, v_new)
    for (bi, d, h), o_i, st_i, u_i in zip(chains, out, st, upd):
        refs[d][3][bi, :, sl(h)] = o_i.astype(BF16)
        st_ref[sidx(bi, d, h)] = st_i * gate(bi, d, "et_c")[:, acol(d, h):acol(d, h) + 1] + u_i


DN_BATCH = 4


def _deltanet(qkv, gate_col, gate_row, pc, pr):
    b, lp, _ = qkv.shape
    n = lp // DN_CHUNK
    hw = DN_HEADS * DN_D
    nb = DN_BATCH if b % DN_BATCH == 0 else 1

    def fwd_chunk(s):
        return s

    lead = ROW0 // DN_CHUNK

    def bwd_chunk(s):
        return jnp.where(s < lead, s, n - 1 + lead - s)

    def spec3(cb, chunk):
        return pl.BlockSpec((nb, DN_CHUNK, hw), lambda i, s: (i, chunk(s), cb))

    in_specs = [spec3(0, fwd_chunk), spec3(1, fwd_chunk), spec3(2, fwd_chunk),
                spec3(0, bwd_chunk), spec3(1, bwd_chunk), spec3(2, bwd_chunk),
                pl.BlockSpec((nb, DN_CHUNK, GATE_COLS), lambda i, s: (i, fwd_chunk(s), 0)),
                pl.BlockSpec((nb, DN_CHUNK, GATE_COLS), lambda i, s: (i, bwd_chunk(s), 0)),
                pl.BlockSpec((nb, 1, 16, DN_CHUNK), lambda i, s: (i, fwd_chunk(s), 0, 0)),
                pl.BlockSpec((nb, 1, 16, DN_CHUNK), lambda i, s: (i, bwd_chunk(s), 0, 0)),
                pl.BlockSpec((2, GATE_COLS), lambda i, s: (0, 0)),
                pl.BlockSpec((16, 2 * DN_CHUNK), lambda i, s: (0, 0))]
    out_specs = [pl.BlockSpec((nb, DN_CHUNK, hw), lambda i, s: (i, fwd_chunk(s), 0)),
                 pl.BlockSpec((nb, DN_CHUNK, hw), lambda i, s: (i, bwd_chunk(s), 0))]
    return pl.pallas_call(
        _dn_kernel,
        grid=(b // nb, n),
        in_specs=in_specs,
        out_specs=out_specs,
        out_shape=[jax.ShapeDtypeStruct((b, lp, hw), BF16)] * 2,
        scratch_shapes=[pltpu.VMEM((nb * 2 * DN_HEADS, DN_D, DN_D), F32)],
        compiler_params=_cparams(("parallel", "arbitrary")),
        name="deltanet",
    )(qkv, qkv, qkv, qkv, qkv, qkv, gate_col, gate_col, gate_row, gate_row, pc, pr)


def _attnprep_kernel(x_ref, dqw_ref, dkw_ref, qlw_ref, wuq_ref, klw_ref, wukv_ref,
                     mqw_ref, mkw_ref, cos_ref, sin_ref,
                     dq_ref, dk_ref, mq_ref, mk_ref, mv_ref):
    ones = _ones_bf16(LANES)
    row = lax.broadcasted_iota(jnp.int32, (LANES, LANES), 0)
    col = lax.broadcasted_iota(jnp.int32, (LANES, LANES), 1)
    half_ones = ((row // DIFF_D) == (col // DIFF_D)).astype(BF16)
    cos_t, sin_t = cos_ref[...], sin_ref[...]
    tm = x_ref.shape[1]
    lane_t = lax.broadcasted_iota(jnp.int32, (tm, LANES), 1)
    row_t = pl.program_id(1) * tm + lax.broadcasted_iota(jnp.int32, (tm, LANES), 0)
    q_flag = jnp.where(lane_t == MASK_LANE, 1.0, 0.0)
    k_flag = jnp.where((lane_t == MASK_LANE) & (row_t < N_PAD), NEG_BIG, 0.0)

    def rope(t):
        return t * cos_t + pltpu.roll(t, 64, 1) * sin_t

    for src, w_ref, dst in ((0, dqw_ref, dq_ref), (512, dkw_ref, dk_ref)):
        for h in range(DIFF_HEADS):
            y = x_ref[0, :, src + h * LANES:src + (h + 1) * LANES].astype(F32)
            ms = _dot((y * y).astype(BF16), half_ones) * (1.0 / DIFF_D)
            dst[0, :, h * LANES:(h + 1) * LANES] = (y * lax.rsqrt(ms + EPS) * w_ref[...]).astype(BF16)

    cq0 = x_ref[0, :, 1024:1152].astype(F32)
    cq1 = x_ref[0, :, 1152:1280].astype(F32)
    ms = _dot((cq0 * cq0 + cq1 * cq1).astype(BF16), ones) * (1.0 / MLA_Q_RANK)
    r = lax.rsqrt(ms + EPS)
    cqn = jnp.concatenate([cq0 * r * qlw_ref[:, 0:LANES], cq1 * r * qlw_ref[:, LANES:2 * LANES]],
                          axis=1).astype(BF16)
    q = _dot(cqn, wuq_ref[...])
    inv_d = 1.0 / (MLA_NOPE + MLA_ROPE)
    for h in range(MLA_HEADS):
        q0 = q[:, 256 * h:256 * h + LANES]
        q1 = q[:, 256 * h + LANES:256 * (h + 1)]
        ms = _dot((q0 * q0 + q1 * q1).astype(BF16), ones) * inv_d
        r = lax.rsqrt(ms + EPS)
        mq_ref[0, :, 256 * h:256 * h + LANES] = (q0 * r * mqw_ref[:, 0:LANES]).astype(BF16)
        mq_ref[0, :, 256 * h + LANES:256 * (h + 1)] = (
            rope(q1 * r * mqw_ref[:, LANES:2 * LANES]) + q_flag).astype(BF16)

    ckv = x_ref[0, :, 1280:1408].astype(F32)
    ms = _dot((ckv * ckv).astype(BF16), ones) * (1.0 / MLA_KV_RANK)
    ckvn = (ckv * lax.rsqrt(ms + EPS) * klw_ref[...]).astype(BF16)
    kv = _dot(ckvn, wukv_ref[...])
    mv_ref[0] = kv[:, 512:1024].astype(BF16)
    kr = x_ref[0, :, 1408:1536].astype(F32)
    kr2 = kr * kr
    for h in range(MLA_HEADS):
        k0 = kv[:, LANES * h:LANES * (h + 1)]
        ms = _dot((k0 * k0 + kr2).astype(BF16), ones) * inv_d
        r = lax.rsqrt(ms + EPS)
        mk_ref[0, :, 256 * h:256 * h + LANES] = (k0 * r * mkw_ref[:, 0:LANES]).astype(BF16)
        mk_ref[0, :, 256 * h + LANES:256 * (h + 1)] = (
            rope(kr * r * mkw_ref[:, LANES:2 * LANES]) + k_flag).astype(BF16)


def _attnprep(main, dqw, dkw, qlw, wuq, klw, wukv, mqw, mkw, cos_t, sin_t):
    b, lp, _ = main.shape
    tm = _row_tile(lp)

    def full(a):
        return pl.BlockSpec(a.shape, lambda i, j: (0,) * a.ndim)

    def rows(width):
        return pl.BlockSpec((1, tm, width), lambda i, j: (i, j, 0))

    return pl.pallas_call(
        _attnprep_kernel,
        grid=(b, lp // tm),
        in_specs=[pl.BlockSpec((1, tm, ATTN_COLS), lambda i, j: (i, j, 0)),
                  full(dqw), full(dkw), full(qlw), full(wuq), full(klw), full(wukv),
                  full(mqw), full(mkw),
                  pl.BlockSpec((tm, LANES), lambda i, j: (j, 0)),
                  pl.BlockSpec((tm, LANES), lambda i, j: (j, 0))],
        out_specs=[rows(512), rows(512), rows(1024), rows(1024), rows(512)],
        out_shape=[jax.ShapeDtypeStruct((b, lp, 512), BF16),
                   jax.ShapeDtypeStruct((b, lp, 512), BF16),
                   jax.ShapeDtypeStruct((b, lp, 1024), BF16),
                   jax.ShapeDtypeStruct((b, lp, 1024), BF16),
                   jax.ShapeDtypeStruct((b, lp, 512), BF16)],
        compiler_params=_cparams(("parallel", "parallel")),
        name="attnprep",
    )(main, dqw, dkw, qlw, wuq, klw, wukv, mqw, mkw, cos_t, sin_t)


KEY_PARTS = 3


def _key_splits(lp, parts=None):
    nblk = lp // LANES
    nparts = min(parts or KEY_PARTS, nblk)
    edges = [(nblk * i // nparts) * LANES for i in range(nparts + 1)]
    return tuple(zip(edges[:-1], edges[1:]))


def _diff_attend_t(score_fns, vt_ref, lam, lp):
    splits = _key_splits(lp)
    tq = LANES
    nt = len(score_fns)
    add = lambda x, y: x + y

    def scores(t):
        return [score_fns[t](k0, k1) for k0, k1 in splits]

    def softmax(st):
        m = functools.reduce(jnp.maximum, [x.max(axis=0, keepdims=True) for x in st])
        pt = [jnp.exp2(x - m) for x in st]
        lt = functools.reduce(add, [x.sum(axis=0, keepdims=True) for x in pt])
        l1, l2 = lt[:, 0:tq], lt[:, tq:2 * tq]
        r = lam * l1 / l2
        return [(x[:, 0:tq] - x[:, tq:2 * tq] * r).astype(BF16) for x in pt], 1.0 / l1

    def values(grp):
        acc = None
        for i, (k0, k1) in enumerate(splits):
            w = grp[0][0][i] if len(grp) == 1 else jnp.concatenate([grp[0][0][i], grp[1][0][i]], axis=1)
            part = _dot(vt_ref[0, :, k0:k1], w)
            acc = part if acc is None else acc + part
        return [acc[:, j * tq:(j + 1) * tq] * grp[j][1] for j in range(len(grp))]

    s = [scores(0)]
    sm, outs = [], []
    for t in range(nt):
        if t + 1 < nt:
            s.append(scores(t + 1))
        sm.append(softmax(s[t]))
        if t % 2 == 1:
            outs.extend(values(sm[t - 1:t + 1]))
    if nt % 2 == 1:
        outs.extend(values(sm[nt - 1:nt]))
    return outs


def _diffattn_kernel(q_ref, k_ref, vt_ref, bandt_ref, far_ref, lam_ref, sw_ref, o_ref, kx_ref):
    qi = pl.program_id(2)
    tq = LANES
    lp = k_ref.shape[1]
    nt = kx_ref.shape[0]
    nblk = lp // LANES
    c_neg = jnp.broadcast_to(far_ref[0, 0:1, :], (LANES, LANES)).astype(BF16)

    @pl.when(qi == 0)
    def _():
        c_pos = jnp.broadcast_to(far_ref[0, 1:2, :], (lp, LANES)).astype(BF16)
        for t in range(nt):
            kx_ref[t, :, 0:LANES] = k_ref[0]
            kx_ref[t, :, LANES:2 * LANES] = c_pos

    lane = lax.broadcasted_iota(jnp.int32, (tq, LANES), 1)
    rowi = lax.broadcasted_iota(jnp.int32, (tq, LANES), 0)
    eye = jnp.where(lane == rowi, 1.0, 0.0).astype(BF16)
    score_fns = []
    for t in range(nt):
        tile = qi * nt + t

        def put(kb, val, t=t):
            @pl.when((kb >= 0) & (kb < nblk))
            def _():
                kx_ref[t, pl.ds(pl.multiple_of(kb * LANES, LANES), LANES), LANES:2 * LANES] = val

        for back in range(nt):
            put(tile - 2 - back, c_neg)
        for dd in range(3):
            put(tile - 1 + dd, bandt_ref[0, dd])

        @pl.when(tile <= nt + 1)
        def _(t=t):
            kx_ref[t, 0:N_PAD, LANES:2 * LANES] = jnp.full((N_PAD, LANES), NEG_BIG, BF16)

        q = q_ref[0, t * tq:(t + 1) * tq, :]
        zero = jnp.zeros_like(q)
        lhs = jnp.concatenate([jnp.concatenate([jnp.where(lane < DIFF_D, q, zero), eye], axis=1),
                               jnp.concatenate([jnp.where(lane >= DIFF_D, q, zero), eye], axis=1)], axis=0)
        score_fns.append(lambda k0, k1, t=t, lhs=lhs: _dot_nt(kx_ref[t, k0:k1, :], lhs))
    for t, ot in enumerate(_diff_attend_t(score_fns, vt_ref, lam_ref[0:1, 0:1], lp)):
        o = ot.T
        ms = jnp.mean(o * o, axis=-1, keepdims=True)
        o_ref[0, t * tq:(t + 1) * tq, :] = (o * lax.rsqrt(ms + EPS) * sw_ref[...]).astype(BF16)


DIFF_TILES = 3


def _diffattn(dq, dk, dvt, bandt, far, lam, sw):
    b, lp, _ = dq.shape
    nblk = lp // LANES
    nt = DIFF_TILES
    rows = nt * LANES
    return pl.pallas_call(
        _diffattn_kernel,
        grid=(b, DIFF_HEADS, pl.cdiv(nblk, nt)),
        in_specs=[pl.BlockSpec((1, rows, LANES), lambda i, h, j: (i, j, h)),
                  pl.BlockSpec((1, lp, LANES), lambda i, h, j: (i, 0, h)),
                  pl.BlockSpec((1, LANES, lp), lambda i, h, j: (i, h, 0)),
                  pl.BlockSpec((1, 3, LANES, LANES), lambda i, h, j: (h, 0, 0, 0)),
                  pl.BlockSpec((1, 2, LANES), lambda i, h, j: (h, 0, 0)),
                  pl.BlockSpec((1, LANES), lambda i, h, j: (0, 0)),
                  pl.BlockSpec((1, LANES), lambda i, h, j: (0, 0))],
        out_specs=pl.BlockSpec((1, rows, LANES), lambda i, h, j: (i, j, h)),
        out_shape=jax.ShapeDtypeStruct((b, lp, DIFF_HEADS * LANES), BF16),
        scratch_shapes=[pltpu.VMEM((nt, lp, 2 * LANES), BF16)],
        compiler_params=_cparams(("parallel", "parallel", "arbitrary")),
        name="diffattn",
    )(dq, dk, dvt, bandt, far, lam, sw)


def _mlaattn_kernel(q_ref, k_ref, v_ref, o_ref, vx_ref):
    lp = k_ref.shape[1]

    @pl.when(pl.program_id(2) == 0)
    def _():
        vx_ref[:, 0:LANES] = v_ref[0]
        vx_ref[:, LANES:2 * LANES] = jnp.ones((lp, LANES), BF16)

    q = q_ref[0]
    splits = _key_splits(lp, MLA_KEY_PARTS)
    s = [_dot_nt(q, k_ref[0, k0:k1, :]) for k0, k1 in splits]
    m = [x.max(axis=1, keepdims=True) for x in s]
    p = [jnp.exp2((x - mi).astype(BF16)) for x, mi in zip(s, m)]
    o = [_dot(pi, vx_ref[k0:k1, :]) for (k0, k1), pi in zip(splits, p)]
    m_all = functools.reduce(jnp.maximum, m)
    acc = functools.reduce(lambda a, b: a + b, [oi * jnp.exp2(mi - m_all) for oi, mi in zip(o, m)])
    o_ref[0] = (acc[:, 0:LANES] / acc[:, LANES:LANES + 1]).astype(BF16)


MLA_KEY_PARTS = 4
MLA_Q_STEPS = 4


def _mlaattn(mq, mk, mv):
    b, lp, _ = mq.shape
    tq = lp // MLA_Q_STEPS if lp % (16 * MLA_Q_STEPS) == 0 else LANES
    return pl.pallas_call(
        _mlaattn_kernel,
        grid=(b, MLA_HEADS, lp // tq),
        in_specs=[pl.BlockSpec((1, tq, 256), lambda i, h, j: (i, j, h)),
                  pl.BlockSpec((1, lp, 256), lambda i, h, j: (i, 0, h)),
                  pl.BlockSpec((1, lp, LANES), lambda i, h, j: (i, 0, h))],
        out_specs=pl.BlockSpec((1, tq, LANES), lambda i, h, j: (i, j, h)),
        out_shape=jax.ShapeDtypeStruct((b, lp, MLA_HEADS * LANES), BF16),
        scratch_shapes=[pltpu.VMEM((lp, 2 * LANES), BF16)],
        compiler_params=_cparams(("parallel", "parallel", "arbitrary")),
        name="mlaattn",
    )(mq, mk, mv)


def _outproj_kernel(of_ref, ob_ref, z_ref, df_ref, ml_ref, h_ref, dnw_ref, w_ref, o_ref):
    ones = _ones_bf16(LANES)
    parts = []
    for h in range(DN_HEADS):
        sl = slice(h * DN_D, (h + 1) * DN_D)
        o = of_ref[0, :, sl].astype(F32) + ob_ref[0, :, sl].astype(F32)
        ms = _dot((o * o).astype(BF16), ones) * (1.0 / DN_D)
        z = z_ref[0, :, sl].astype(F32)
        parts.append((o * lax.rsqrt(ms + EPS) * dnw_ref[...] * (z * jax.nn.sigmoid(z))).astype(BF16))
    ydn = jnp.concatenate(parts, axis=1)
    acc = h_ref[0] + _dot(ydn, w_ref[0:512, :])
    acc = acc + _dot(df_ref[0], w_ref[512:1024, :])
    acc = acc + _dot(ml_ref[0], w_ref[1024:1536, :])
    o_ref[0] = acc


def _outproj(o_f, o_b, main, ydf, yml, h, dnw, w):
    b, lp, d = h.shape
    tm = _row_tile(lp)

    def rows(width, cb=0):
        return pl.BlockSpec((1, tm, width), lambda i, j: (i, j, cb))

    return pl.pallas_call(
        _outproj_kernel,
        grid=(b, lp // tm),
        in_specs=[rows(512), rows(512), rows(512, (DN_COL0 + 1536) // 512), rows(512), rows(512), rows(d),
                  pl.BlockSpec((1, LANES), lambda i, j: (0, 0)),
                  pl.BlockSpec(w.shape, lambda i, j: (0, 0))],
        out_specs=rows(d),
        out_shape=jax.ShapeDtypeStruct((b, lp, d), F32),
        compiler_params=_cparams(("parallel", "parallel")),
        name="outproj",
    )(o_f, o_b, main, ydf, yml, h, dnw, w)


def _ffn_kernel(h_ref, nw_ref, wg_ref, wu_ref, wd_ref, o_ref, a_ref):
    x = h_ref[...]
    ms = jnp.mean(x * x, axis=-1, keepdims=True)
    u = (x * lax.rsqrt(ms + EPS) * nw_ref[...]).astype(BF16)
    f = wg_ref.shape[1]
    fc = 256
    for c0 in range(0, f, fc):
        g = _dot(u, wg_ref[:, c0:c0 + fc])
        up = _dot(u, wu_ref[:, c0:c0 + fc])
        a_ref[:, c0:c0 + fc] = (g * jax.nn.sigmoid(g) * up).astype(BF16)
    o_ref[...] = x + _dot(a_ref[...], wd_ref[...])


def _ffn(h2, nw, wg, wu, wd):
    n, d = h2.shape
    f = wg.shape[1]
    tm = 512 if n % 512 == 0 else 128

    def const(a):
        return pl.BlockSpec(a.shape, lambda i: (0, 0), pipeline_mode=pl.Buffered(1))

    return pl.pallas_call(
        _ffn_kernel,
        grid=(n // tm,),
        in_specs=[pl.BlockSpec((tm, d), lambda i: (i, 0)),
                  pl.BlockSpec((1, d), lambda i: (0, 0)),
                  const(wg), const(wu), const(wd)],
        out_specs=pl.BlockSpec((tm, d), lambda i: (i, 0)),
        out_shape=jax.ShapeDtypeStruct((n, d), F32),
        scratch_shapes=[pltpu.VMEM((tm, f), BF16)],
        compiler_params=_cparams(("parallel",)),
        name="ffn",
    )(h2, nw, wg, wu, wd)


def _router_kernel(h_ref, nw_ref, rw_ref, u_ref, idx_ref, gate_ref):
    x = h_ref[...]
    ms = jnp.mean(x * x, axis=-1, keepdims=True)
    u = x * lax.rsqrt(ms + EPS) * nw_ref[...]
    u_ref[...] = u.astype(BF16)
    logits = jnp.dot(u, rw_ref[...], preferred_element_type=F32, precision=lax.Precision.HIGHEST)
    lane = lax.broadcasted_iota(jnp.int32, logits.shape, 1)
    logits = jnp.where(lane < N_EXPERTS, logits, -jnp.inf)
    m1 = jnp.max(logits, axis=-1, keepdims=True)
    i1 = jnp.min(jnp.where(logits == m1, lane, LANES), axis=-1, keepdims=True)
    rest = jnp.where(lane == i1, -jnp.inf, logits)
    m2 = jnp.max(rest, axis=-1, keepdims=True)
    i2 = jnp.min(jnp.where(rest == m2, lane, LANES), axis=-1, keepdims=True)
    e2 = jnp.exp(m2 - m1)
    g1 = 1.0 / (1.0 + e2)
    g2 = e2 / (1.0 + e2)
    idx_ref[...] = jnp.where(lane == 0, i1, jnp.where(lane == 1, i2, 0))
    gate_ref[...] = jnp.where(lane == 0, g1, jnp.where(lane == 1, g2, 0.0))


def _router(h2, nw, rw):
    n, d = h2.shape
    tm = 512 if n % 512 == 0 else 128
    return pl.pallas_call(
        _router_kernel,
        grid=(n // tm,),
        in_specs=[pl.BlockSpec((tm, d), lambda i: (i, 0)),
                  pl.BlockSpec((1, d), lambda i: (0, 0)),
                  pl.BlockSpec((d, LANES), lambda i: (0, 0))],
        out_specs=[pl.BlockSpec((tm, d), lambda i: (i, 0)),
                   pl.BlockSpec((tm, LANES), lambda i: (i, 0)),
                   pl.BlockSpec((tm, LANES), lambda i: (i, 0))],
        out_shape=[jax.ShapeDtypeStruct((n, d), BF16),
                   jax.ShapeDtypeStruct((n, LANES), jnp.int32),
                   jax.ShapeDtypeStruct((n, LANES), F32)],
        compiler_params=_cparams(("parallel",)),
        name="router",
    )(h2, nw, rw)


def _moe_kernel(te_ref, tv_ref, x_ref, wg_ref, wu_ref, wd_ref, o_ref, acc_ref):
    i = pl.program_id(0)
    f = pl.program_id(1)

    @pl.when(f == 0)
    def _():
        acc_ref[...] = jnp.zeros(acc_ref.shape, F32)

    @pl.when(tv_ref[i] > 0)
    def _():
        x = x_ref[...]
        g = _dot(x, wg_ref[0])
        up = _dot(x, wu_ref[0])
        a = (g * jax.nn.sigmoid(g) * up).astype(BF16)
        acc_ref[...] += _dot(a, wd_ref[0])

    @pl.when(f == pl.num_programs(1) - 1)
    def _():
        o_ref[...] = acc_ref[...].astype(o_ref.dtype)


MOE_TM = 512
MOE_F_STEPS = 2


def _moe(x_sorted, tile_expert, tile_valid, wg, wu, wd, tm):
    p, d = x_sorted.shape
    f = wg.shape[2]
    nf = MOE_F_STEPS
    fh = f // nf
    grid_spec = pltpu.PrefetchScalarGridSpec(
        num_scalar_prefetch=2,
        grid=(p // tm, nf),
        in_specs=[pl.BlockSpec((tm, d), lambda i, j, te, tv: (i, 0)),
                  pl.BlockSpec((1, d, fh), lambda i, j, te, tv: (te[i], 0, j)),
                  pl.BlockSpec((1, d, fh), lambda i, j, te, tv: (te[i], 0, j)),
                  pl.BlockSpec((1, fh, d), lambda i, j, te, tv: (te[i], j, 0))],
        out_specs=pl.BlockSpec((tm, d), lambda i, j, te, tv: (i, 0)),
        scratch_shapes=[pltpu.VMEM((tm, d), F32)],
    )
    return pl.pallas_call(
        _moe_kernel,
        grid_spec=grid_spec,
        out_shape=jax.ShapeDtypeStruct((p, d), BF16),
        compiler_params=_cparams(("arbitrary", "arbitrary")),
        name="moe",
    )(tile_expert, tile_valid, x_sorted, wg, wu, wd)


def _combine_kernel(h_ref, y0_ref, y1_ref, g_ref, o_ref):
    g = g_ref[...]
    o_ref[...] = (h_ref[...] + g[:, 0:1] * y0_ref[...].astype(F32)
                  + g[:, 1:2] * y1_ref[...].astype(F32))


def _combine(h2, y0, y1, gates):
    n, d = h2.shape
    tm = 1024 if n % 1024 == 0 else 128
    row = lambda w: pl.BlockSpec((tm, w), lambda i: (i, 0))
    return pl.pallas_call(
        _combine_kernel,
        grid=(n // tm,),
        in_specs=[row(d), row(d), row(d), row(LANES)],
        out_specs=row(d),
        out_shape=jax.ShapeDtypeStruct((n, d), F32),
        compiler_params=_cparams(("parallel",)),
        name="combine",
    )(h2, y0, y1, gates)


def _spread_rope_cols(w):
    z = jnp.zeros(w.shape[:-1] + (32,), w.dtype)
    return jnp.concatenate([w[..., 0:32], z, w[..., 32:64], z], axis=-1)


def _relayout_w_in(w):
    dnw = 4 * DN_HEADS * DN_D
    n_gate = 4 * DN_HEADS
    rest = w[:, dnw + n_gate:]
    nqk = 4 * DIFF_HEADS * DIFF_D
    nv = DIFF_HEADS * LANES
    diff_qk = rest[:, :nqk]
    lat = rest[:, nqk + nv:rest.shape[1] - MLA_ROPE]
    kr = _spread_rope_cols(rest[:, rest.shape[1] - MLA_ROPE:])
    gates = jnp.concatenate([w[:, dnw:dnw + n_gate],
                             jnp.zeros((w.shape[0], GATE_COLS - n_gate), w.dtype)], axis=1)
    out = jnp.concatenate([diff_qk, lat, kr, w[:, :dnw], gates], axis=1)
    assert out.shape[1] == MAIN_COLS + GATE_COLS
    return out.astype(BF16)


def _relayout_w_uq(w):
    per = MLA_NOPE + MLA_ROPE
    heads = [jnp.concatenate([w[:, h * per:h * per + MLA_NOPE],
                              _spread_rope_cols(w[:, h * per + MLA_NOPE:(h + 1) * per])], axis=1)
             for h in range(MLA_HEADS)]
    return jnp.concatenate(heads, axis=1).astype(BF16)


def _relayout_w_ukv(w):
    per = MLA_NOPE + LANES
    ks = [w[:, h * per:h * per + MLA_NOPE] for h in range(MLA_HEADS)]
    vs = [w[:, h * per + MLA_NOPE:(h + 1) * per] for h in range(MLA_HEADS)]
    return jnp.concatenate(ks + vs, axis=1).astype(BF16)


def _diff_v_weight_t(w):
    c0 = 4 * DN_HEADS * DN_D + 4 * DN_HEADS + 4 * DIFF_HEADS * DIFF_D
    return w[:, c0:c0 + DIFF_HEADS * LANES].T.astype(BF16)


def _qk_norm_vec(w, scale):
    return (jnp.concatenate([w[:MLA_NOPE], _spread_rope_cols(w[MLA_NOPE:])]) * scale)[None, :]


def _t5_bucket(rel):
    half = T5_BUCKETS // 2
    exact = half // 2
    n = jnp.abs(rel)
    large = exact + (jnp.log(jnp.maximum(n, exact).astype(F32) / exact)
                     / math.log(T5_MAX_DIST / exact) * (half - exact)).astype(jnp.int32)
    large = jnp.minimum(large, half - 1)
    return jnp.where(rel > 0, half, 0) + jnp.where(n < exact, n, large)


def _t5_tables(table, tq):
    i = jnp.arange(tq)[None, :]
    j = jnp.arange(LANES)[:, None]
    rels = jnp.stack([j - i + (d * LANES) for d in (-1, 0, 1)])
    table = table.astype(F32) * LOG2E
    bandt = jnp.transpose(table[_t5_bucket(rels)], (3, 0, 1, 2)).astype(BF16)
    far_rel = jnp.array([-T5_MAX_DIST, T5_MAX_DIST], jnp.int32)
    far = jnp.transpose(table[_t5_bucket(far_rel)], (1, 0))
    far = jnp.broadcast_to(far[:, :, None], far.shape + (LANES,))
    return bandt, far


def _rope_tables(lp):
    half = MLA_ROPE // 2
    inv = 1.0 / (ROPE_THETA ** (jnp.arange(half, dtype=F32) / half))
    pos = (jnp.arange(lp, dtype=jnp.int32) - N_PAD).astype(F32)
    ang = pos[:, None] * inv[None, :]
    c, s, z = jnp.cos(ang), jnp.sin(ang), jnp.zeros((lp, half), F32)
    return jnp.concatenate([c, z, c, z], axis=1), jnp.concatenate([-s, z, s, z], axis=1)


def _exclusive_cumsum(onehot):
    rows, cols = onehot.shape
    blk = 256 if rows % 256 == 0 else rows
    x = onehot.reshape(rows // blk, blk, cols)
    tri = (jnp.arange(blk)[:, None] > jnp.arange(blk)[None, :]).astype(BF16)
    within = jnp.einsum("ij,bjk->bik", tri, x.astype(BF16), preferred_element_type=F32).astype(jnp.int32)
    totals = jnp.sum(x, axis=1)
    offsets = jnp.cumsum(totals, axis=0) - totals
    return (within + offsets[:, None, :]).reshape(rows, cols)


def _moe_plan(idx, n, tm):
    e_flat = idx.reshape(-1)
    onehot = (e_flat[:, None] == jnp.arange(N_EXPERTS)[None, :]).astype(jnp.int32)
    counts = jnp.sum(onehot, axis=0)
    rank = jnp.sum(_exclusive_cumsum(onehot) * onehot, axis=1)
    tiles_per = (counts + tm - 1) // tm
    tile_end = jnp.cumsum(tiles_per)
    start = (tile_end - tiles_per) * tm
    dest = start[e_flat] + rank
    n_tiles = (2 * n) // tm + N_EXPERTS
    p = n_tiles * tm
    slot_tok = jnp.zeros((p,), jnp.int32).at[dest].set(jnp.arange(2 * n, dtype=jnp.int32) // 2)
    t = jnp.arange(n_tiles, dtype=jnp.int32)
    tile_expert = jnp.minimum(jnp.sum((t[:, None] >= tile_end[None, :]).astype(jnp.int32), axis=1),
                              N_EXPERTS - 1).astype(jnp.int32)
    tile_valid = (t < tile_end[-1]).astype(jnp.int32)
    return slot_tok, dest.reshape(n, 2), tile_expert, tile_valid


def kernel(x, meta_tokens, rel_bias_table, attn_norm_w, w_in, dn_conv_w, dn_a_log, dn_dt_bias, dn_norm_w, diff_q_norm_w, diff_k_norm_w, diff_lambda, diff_subln_w, mla_q_lat_norm_w, mla_w_uq, mla_kv_lat_norm_w, mla_w_ukv, mla_q_norm_w, mla_k_norm_w, w_out, ffn_norm_w, ffn_w_gate, ffn_w_up, ffn_w_down, router_w, moe_w_gate, moe_w_up, moe_w_down):
    bsz, seq, d = x.shape
    depth = w_in.shape[0]
    lp = ROW0 + seq
    n = bsz * lp
    assert lp % LANES == 0 and meta_tokens.shape[0] == N_META

    meta = jnp.broadcast_to(meta_tokens[None].astype(x.dtype), (bsz, N_META, d))
    h = jnp.concatenate([jnp.zeros((bsz, N_PAD, d), x.dtype), meta, x], axis=1)

    cos_t, sin_t = _rope_tables(lp)
    bandt, far = _t5_tables(rel_bias_table, LANES)
    n_chunks = lp // DN_CHUNK

    for l in range(depth):
        main, gate, dvt = _inproj(h, attn_norm_w[l][None, :], _relayout_w_in(w_in[l]),
                                  _diff_v_weight_t(w_in[l]))

        qkv = _dnprep(main, dn_conv_w[l])
        gate_row = jnp.transpose(gate[:, :, 0:16].reshape(bsz, n_chunks, DN_CHUNK, 16), (0, 1, 3, 2))
        neg_a = -jnp.exp(dn_a_log[l].astype(F32)).reshape(-1)
        dtb = dn_dt_bias[l].astype(F32).reshape(-1)
        z8 = jnp.zeros((8,), F32)
        pc = jnp.stack([jnp.concatenate([z8, neg_a, jnp.zeros((GATE_COLS - 16,), F32)]),
                        jnp.concatenate([z8, dtb, jnp.zeros((GATE_COLS - 16,), F32)])])
        pr = jnp.concatenate([jnp.broadcast_to(jnp.concatenate([z8, neg_a])[:, None], (16, DN_CHUNK)),
                              jnp.broadcast_to(jnp.concatenate([z8, dtb])[:, None], (16, DN_CHUNK))], axis=1)
        o_f, o_b = _deltanet(qkv, gate, gate_row, pc, pr)

        dscale = DIFF_D ** -0.5 * LOG2E
        mscale = (MLA_NOPE + MLA_ROPE) ** -0.5 * LOG2E
        dq, dk, mq, mk, mv = _attnprep(
            main,
            (jnp.tile(diff_q_norm_w[l], 2) * dscale)[None, :],
            jnp.tile(diff_k_norm_w[l], 2)[None, :],
            mla_q_lat_norm_w[l][None, :], _relayout_w_uq(mla_w_uq[l]),
            mla_kv_lat_norm_w[l][None, :], _relayout_w_ukv(mla_w_ukv[l]),
            _qk_norm_vec(mla_q_norm_w[l], mscale), _qk_norm_vec(mla_k_norm_w[l], 1.0),
            cos_t, sin_t)
        lam_init = 0.8 - 0.6 * math.exp(-0.3 * l)
        lpar = diff_lambda[l].astype(F32)
        lam = jnp.exp(jnp.sum(lpar[0] * lpar[1])) - jnp.exp(jnp.sum(lpar[2] * lpar[3])) + lam_init
        ydf = _diffattn(dq, dk, dvt, bandt, far, jnp.full((1, LANES), lam, F32),
                        (diff_subln_w[l] * (1.0 - lam_init))[None, :])
        yml = _mlaattn(mq, mk, mv)

        h = _outproj(o_f, o_b, main, ydf, yml, h, dn_norm_w[l][None, :], w_out[l].astype(BF16))

        h2 = h.reshape(n, d)
        if l % 2 == 0:
            i = l // 2
            h2 = _ffn(h2, ffn_norm_w[l][None, :], ffn_w_gate[i].astype(BF16),
                      ffn_w_up[i].astype(BF16), ffn_w_down[i].astype(BF16))
        else:
            i = l // 2
            rw = jnp.concatenate([router_w[i], jnp.zeros((d, LANES - N_EXPERTS), F32)], axis=1)
            u, idx, gates = _router(h2, ffn_norm_w[l][None, :], rw)
            tm = MOE_TM if (2 * n) % MOE_TM == 0 else 128
            slot_tok, tok_slots, tile_expert, tile_valid = _moe_plan(idx[:, 0:2], n, tm)
            x_sorted = jnp.take(u, slot_tok, axis=0)
            y_sorted = _moe(x_sorted, tile_expert, tile_valid, moe_w_gate[i].astype(BF16),
                            moe_w_up[i].astype(BF16), moe_w_down[i].astype(BF16), tm)
            h2 = _combine(h2, jnp.take(y_sorted, tok_slots[:, 0], axis=0),
                          jnp.take(y_sorted, tok_slots[:, 1], axis=0), gates)
        h = h2.reshape(bsz, lp, d)
    return h[:, ROW0:]
```

```python
import functools
import math

import jax
import jax.numpy as jnp
from jax import lax
from jax.experimental import pallas as pl
from jax.experimental.pallas import tpu as pltpu

F32 = jnp.float32
BF16 = jnp.bfloat16

N_META = 16
N_PAD = 112
ROW0 = N_PAD + N_META
DN_HEADS = 4
DN_D = 128
DN_CONV = 5
DN_CHUNK = 64
DIFF_HEADS = 4
DIFF_D = 64
MLA_HEADS = 4
MLA_NOPE = 128
MLA_ROPE = 64
MLA_Q_RANK = 256
MLA_KV_RANK = 128
ROPE_THETA = 10000.0
T5_BUCKETS = 32
T5_MAX_DIST = 128
N_EXPERTS = 8
EPS = 1e-6
NEG_BIG = -1e30
LOG2E = 1.4426950408889634
MASK_LANE = 32
LANES = 128
VMEM_LIMIT = 56 * 1024 * 1024

ATTN_COLS = 1536
DN_COL0 = ATTN_COLS
MAIN_COLS = 3584
GATE_COLS = 128


def _dot(a, b):
    return jnp.dot(a, b, preferred_element_type=F32)


def _dot_nt(a, b):
    return lax.dot_general(a, b, (((1,), (1,)), ((), ())), preferred_element_type=F32)


def _ones_bf16(n):
    return jnp.ones((n, n), BF16)


def _row_tile(lp):
    return 384 if lp % 384 == 0 else 128


def _cparams(sem):
    return pltpu.CompilerParams(dimension_semantics=sem, vmem_limit_bytes=VMEM_LIMIT)


def _inproj_kernel(h_ref, nw_ref, w_ref, wvt_ref, main_ref, gate_ref, vt_ref):
    x = h_ref[0]
    ms = jnp.mean(x * x, axis=-1, keepdims=True)
    u = (x * lax.rsqrt(ms + EPS) * nw_ref[...]).astype(BF16)
    for c0 in range(0, MAIN_COLS, 512):
        main_ref[0, :, c0:c0 + 512] = _dot(u, w_ref[:, c0:c0 + 512]).astype(BF16)
    gate_ref[0] = _dot(u, w_ref[:, MAIN_COLS:MAIN_COLS + GATE_COLS])
    vt_ref[0] = _dot_nt(wvt_ref[...], u).astype(BF16)


def _inproj(h, norm_w, w, wvt):
    b, lp, d = h.shape
    tm = _row_tile(lp)
    ncol = MAIN_COLS + GATE_COLS
    nv = wvt.shape[0]
    return pl.pallas_call(
        _inproj_kernel,
        grid=(b, lp // tm),
        in_specs=[
            pl.BlockSpec((1, tm, d), lambda i, j: (i, j, 0)),
            pl.BlockSpec((1, d), lambda i, j: (0, 0)),
            pl.BlockSpec((d, ncol), lambda i, j: (0, 0)),
            pl.BlockSpec((nv, d), lambda i, j: (0, 0)),
        ],
        out_specs=[
            pl.BlockSpec((1, tm, MAIN_COLS), lambda i, j: (i, j, 0)),
            pl.BlockSpec((1, tm, GATE_COLS), lambda i, j: (i, j, 0)),
            pl.BlockSpec((1, nv, tm), lambda i, j: (i, 0, j)),
        ],
        out_shape=[
            jax.ShapeDtypeStruct((b, lp, MAIN_COLS), BF16),
            jax.ShapeDtypeStruct((b, lp, GATE_COLS), F32),
            jax.ShapeDtypeStruct((b, nv, lp), BF16),
        ],
        compiler_params=_cparams(("parallel", "parallel")),
        name="inproj",
    )(h, norm_w, w, wvt)


def _dnprep_kernel(x_ref, cw_ref, o_ref, xs_ref):
    j = pl.program_id(1)
    lp = x_ref.shape[1]
    halo = 8
    xs_ref[0:halo + N_PAD, :] = jnp.zeros((halo + N_PAD, LANES), F32)
    xs_ref[halo + N_PAD:halo + lp, :] = x_ref[0, N_PAD:, :].astype(F32)
    xs_ref[halo + lp:2 * halo + lp, :] = jnp.zeros((halo, LANES), F32)
    q_scale = jnp.where(j < DN_HEADS, DN_D ** -0.5, 1.0).astype(F32)
    rc = 128

    def conv_silu(base):
        acc = cw_ref[0:1, :] * xs_ref[base + halo - 2:base + halo - 2 + rc, :]
        for t in range(1, DN_CONV):
            off = base + halo - 2 + t
            acc = acc + cw_ref[t:t + 1, :] * xs_ref[off:off + rc, :]
        return acc * jax.nn.sigmoid(acc)

    @pl.when(j < 2 * DN_HEADS)
    def _():
        for c in range(lp // rc):
            y = conv_silu(c * rc)
            ss = jnp.sum(y * y, axis=-1, keepdims=True)
            o_ref[0, c * rc:(c + 1) * rc, :] = (y * (lax.rsqrt(ss + EPS) * q_scale)).astype(BF16)

    @pl.when(j >= 2 * DN_HEADS)
    def _():
        for c in range(lp // rc):
            o_ref[0, c * rc:(c + 1) * rc, :] = conv_silu(c * rc).astype(BF16)


def _dnprep(main, conv_w):
    b, lp, _ = main.shape
    ncb = 3 * DN_HEADS
    return pl.pallas_call(
        _dnprep_kernel,
        grid=(b, ncb),
        in_specs=[
            pl.BlockSpec((1, lp, LANES), lambda i, j: (i, 0, DN_COL0 // LANES + j)),
            pl.BlockSpec((DN_CONV, LANES), lambda i, j: (0, j)),
        ],
        out_specs=pl.BlockSpec((1, lp, LANES), lambda i, j: (i, 0, j)),
        out_shape=jax.ShapeDtypeStruct((b, lp, ncb * LANES), BF16),
        scratch_shapes=[pltpu.VMEM((lp + 16, LANES), F32)],
        compiler_params=_cparams(("parallel", "parallel")),
        name="dnprep",
    )(main, conv_w)


def _each(fn, *lists):
    return [fn(*args) for args in zip(*lists)]


def _unit_tri_inverse(a_list, row, col, hi):
    c = row.shape[0]
    bf = lambda t: t.astype(BF16)

    def mm(x, y):
        z = jnp.zeros_like(y)
        return _dot(x, jnp.concatenate([jnp.where(hi, z, y), jnp.where(hi, y, z)], axis=0))

    eye = (row == col).astype(F32)
    same = lambda n: (row // n) == (col // n)
    blk16 = same(16)
    a0 = _each(lambda a: jnp.where(blk16, a, 0.0), a_list)
    a0b = _each(bf, a0)
    a2b = _each(lambda x: bf(mm(x, x)), a0b)
    a4b = _each(lambda x: bf(mm(x, x)), a2b)
    a8b = _each(lambda x: bf(mm(x, x)), a4b)
    p = _each(lambda x: eye - x, a0)
    for pw in (a2b, a4b, a8b):
        p = _each(lambda x, y: x + mm(bf(x), y), p, pw)
    size = 16
    while size < c:
        inner, outer = same(size), same(2 * size)
        off = _each(lambda a: bf(jnp.where(outer & (~inner), a, 0.0)), a_list)
        pb = _each(bf, p)
        mid = _each(lambda x, y: bf(mm(x, y)), pb, off)
        p = _each(lambda x, m, xb: x - mm(m, xb), p, mid, pb)
        size *= 2
    return p


def _dn_kernel(qf_ref, kf_ref, vf_ref, qb_ref, kb_ref, vb_ref, gcf_ref, gcb_ref,
               grf_ref, grb_ref, pc_ref, pr_ref, of_ref, ob_ref, st_ref):
    s = pl.program_id(1)
    c = DN_CHUNK
    nb = qf_ref.shape[0]

    @pl.when(s == 0)
    def _():
        st_ref[...] = jnp.zeros(st_ref.shape, F32)

    row = lax.broadcasted_iota(jnp.int32, (c, c), 0)
    col = lax.broadcasted_iota(jnp.int32, (c, c), 1)
    negA_c, dtb_c = pc_ref[0:1, :], pc_ref[1:2, :]
    negA_r, dtb_r = pr_ref[:, 0:c], pr_ref[:, c:2 * c]
    bf = lambda t: t.astype(BF16)
    lead = ROW0 // c

    per_dir = []
    for d, (gc_ref, gr_ref) in enumerate(((gcf_ref, grf_ref), (gcb_ref, grb_ref))):
        if d == 0:
            causal = s >= 0
            chunk = s
        else:
            causal = s < lead
            chunk = jnp.where(s < lead, s, (pl.num_programs(1) - 1 + lead) - s)
        sgn = jnp.where(causal, 1, -1)
        m_b = jnp.where((row - col) * sgn >= 0, 1.0, 0.0).astype(BF16)
        m_tb = jnp.where((col - row) * sgn >= 0, 1.0, 0.0).astype(BF16)
        pos_c = chunk * c + lax.broadcasted_iota(jnp.int32, (c, LANES), 0)
        live_c = pos_c >= N_PAD
        pos_r = chunk * c + lax.broadcasted_iota(jnp.int32, (16, c), 1)
        gates = []
        for bi in range(nb):
            gcol = gc_ref[bi]
            beta_c = jnp.where(live_c, jax.nn.sigmoid(gcol), 0.0)
            g_c = jnp.where(live_c, negA_c * jax.nn.softplus(gcol + dtb_c), 0.0)
            grow = gr_ref[bi, 0]
            g_r = jnp.where(pos_r >= N_PAD, negA_r * jax.nn.softplus(grow + dtb_r), 0.0)
            g_c_hi = bf(g_c)
            g_c_lo = bf(g_c - g_c_hi.astype(F32))
            gc_c = _dot(m_b, g_c_hi) + _dot(m_b, g_c_lo)
            g_r_hi = bf(g_r)
            g_r_lo = bf(g_r - g_r_hi.astype(F32))
            gc_r = _dot(g_r_hi, m_tb) + _dot(g_r_lo, m_tb)
            gtot_c = jnp.sum(g_c, axis=0, keepdims=True)
            gates.append(dict(beta_c=beta_c, gc_c=gc_c, gc_r=gc_r, eg_c=jnp.exp(gc_c),
                              ew_c=jnp.exp(gtot_c - gc_c), et_c=jnp.exp(gtot_c)))
        per_dir.append(dict(sgn=sgn, gates=gates))

    refs = ((qf_ref, kf_ref, vf_ref, of_ref), (qb_ref, kb_ref, vb_ref, ob_ref))
    chains = [(bi, d, h) for bi in range(nb) for d in range(2) for h in range(DN_HEADS)]
    sl = lambda h: slice(h * DN_D, (h + 1) * DN_D)
    bcol = lambda d, h: d * DN_HEADS + h
    acol = lambda d, h: 2 * DN_HEADS + d * DN_HEADS + h
    sidx = lambda bi, d, h: (bi * 2 + d) * DN_HEADS + h
    gate = lambda bi, d, name: per_dir[d]["gates"][bi][name]

    q = [refs[d][0][bi, :, sl(h)] for bi, d, h in chains]
    k = [refs[d][1][bi, :, sl(h)] for bi, d, h in chains]
    v = [refs[d][2][bi, :, sl(h)] for bi, d, h in chains]
    kf = _each(lambda t: t.astype(F32), k)
    beta = [gate(bi, d, "beta_c")[:, bcol(d, h):bcol(d, h) + 1] for bi, d, h in chains]
    eg = [gate(bi, d, "eg_c")[:, acol(d, h):acol(d, h) + 1] for bi, d, h in chains]
    kb = _each(lambda x, b: x * b, kf, beta)
    vb = _each(lambda x, b: x.astype(F32) * b, v, beta)

    assert 2 * c == LANES and DN_HEADS % 2 == 0
    pairs = [(2 * i, 2 * i + 1) for i in range(len(chains) // 2)]
    rowp = lax.broadcasted_iota(jnp.int32, (c, LANES), 0)
    lanep = lax.broadcasted_iota(jnp.int32, (c, LANES), 1)
    hi = lanep >= c
    colp = jnp.where(hi, lanep - c, lanep)
    zero_kd = jnp.zeros((c, DN_D), BF16)
    cat0 = lambda *xs: jnp.concatenate(xs, axis=0)
    cat1 = lambda *xs: jnp.concatenate(xs, axis=1)

    def decay_of(ia, ib):
        (bi, d, ha), (_, _, hb) = chains[ia], chains[ib]
        m_in = (rowp - colp) * per_dir[d]["sgn"] >= 0
        gc_c, gc_r = gate(bi, d, "gc_c"), gate(bi, d, "gc_r")
        colb = jnp.where(hi, gc_c[:, acol(d, hb):acol(d, hb) + 1], gc_c[:, acol(d, ha):acol(d, ha) + 1])
        rowb = cat1(gc_r[acol(d, ha):acol(d, ha) + 1, :], gc_r[acol(d, hb):acol(d, hb) + 1, :])
        return jnp.where(m_in, jnp.exp(jnp.where(m_in, colb - rowb, 0.0)), 0.0)

    decay = [decay_of(ia, ib) for ia, ib in pairs]
    kq = [_dot_nt(cat1(cat0(bf(kb[ia]), q[ia]), cat0(bf(kb[ib]), q[ib])),
                  cat0(cat1(k[ia], zero_kd), cat1(zero_kd, k[ib]))) for ia, ib in pairs]
    a = [jnp.where((rowp - colp) * per_dir[chains[ia][1]]["sgn"] > 0, kq_i[0:c] * dec, 0.0)
         for (ia, ib), kq_i, dec in zip(pairs, kq, decay)]
    qk = _each(lambda x, dec: bf(x[c:2 * c] * dec), kq, decay)
    t = _unit_tri_inverse(a, rowp, colp, hi)
    x = _each(lambda x1, x2, e: bf(jnp.concatenate([x1, x2 * e], axis=1)), vb, kb, eg)
    zero_x = jnp.zeros((c, 2 * DN_D), BF16)
    txp = [_dot(bf(ti), cat0(cat1(x[ia], zero_x), cat1(zero_x, x[ib]))) for (ia, ib), ti in zip(pairs, t)]
    tx = [half for r2 in txp for half in (r2[:, 0:2 * DN_D], r2[:, 2 * DN_D:4 * DN_D])]
    st = [st_ref[sidx(*ch)] for ch in chains]
    lhs = _each(lambda txi, qi, e: bf(jnp.concatenate([txi[:, DN_D:2 * DN_D], qi.astype(F32) * e], axis=0)),
                tx, q, eg)
    r = _each(lambda l, si: _dot(l, bf(si)), lhs, st)
    v_new = _each(lambda txi, ri: bf(txi[:, 0:DN_D] - ri[0:c]), tx, r)
    qkv = [_dot(qki, cat0(cat1(v_new[ia], zero_kd), cat1(zero_kd, v_new[ib]))) for (ia, ib), qki in zip(pairs, qk)]
    qkv = [half for o2 in qkv for half in (o2[:, 0:DN_D], o2[:, DN_D:2 * DN_D])]
    out = _each(lambda ri, o_i: ri[c:2 * c] + o_i, r, qkv)
    kw = [bf(kfi * gate(bi, d, "ew_c")[:, acol(d, h):acol(d, h) + 1]) for (bi, d, h), kfi in zip(chains, kf)]
    upd = _each(lambda x, y: lax.dot_general(x, y, (((0,), (0,)), ((), ())), preferred_element_type=F32),
                kw, v_new)
    for (bi, d, h), o_i, st_i, u_i in zip(chains, out, st, upd):
        refs[d][3][bi, :, sl(h)] = o_i.astype(BF16)
        st_ref[sidx(bi, d, h)] = st_i * gate(bi, d, "et_c")[:, acol(d, h):acol(d, h) + 1] + u_i


DN_BATCH = 4


def _deltanet(qkv, gate_col, gate_row, pc, pr):
    b, lp, _ = qkv.shape
    n = lp // DN_CHUNK
    hw = DN_HEADS * DN_D
    nb = DN_BATCH if b % DN_BATCH == 0 else 1

    def fwd_chunk(s):
        return s

    lead = ROW0 // DN_CHUNK

    def bwd_chunk(s):
        return jnp.where(s < lead, s, n - 1 + lead - s)

    def spec3(cb, chunk):
        return pl.BlockSpec((nb, DN_CHUNK, hw), lambda i, s: (i, chunk(s), cb))

    in_specs = [spec3(0, fwd_chunk), spec3(1, fwd_chunk), spec3(2, fwd_chunk),
                spec3(0, bwd_chunk), spec3(1, bwd_chunk), spec3(2, bwd_chunk),
                pl.BlockSpec((nb, DN_CHUNK, GATE_COLS), lambda i, s: (i, fwd_chunk(s), 0)),
                pl.BlockSpec((nb, DN_CHUNK, GATE_COLS), lambda i, s: (i, bwd_chunk(s), 0)),
                pl.BlockSpec((nb, 1, 16, DN_CHUNK), lambda i, s: (i, fwd_chunk(s), 0, 0)),
                pl.BlockSpec((nb, 1, 16, DN_CHUNK), lambda i, s: (i, bwd_chunk(s), 0, 0)),
                pl.BlockSpec((2, GATE_COLS), lambda i, s: (0, 0)),
                pl.BlockSpec((16, 2 * DN_CHUNK), lambda i, s: (0, 0))]
    out_specs = [pl.BlockSpec((nb, DN_CHUNK, hw), lambda i, s: (i, fwd_chunk(s), 0)),
                 pl.BlockSpec((nb, DN_CHUNK, hw), lambda i, s: (i, bwd_chunk(s), 0))]
    return pl.pallas_call(
        _dn_kernel,
        grid=(b // nb, n),
        in_specs=in_specs,
        out_specs=out_specs,
        out_shape=[jax.ShapeDtypeStruct((b, lp, hw), BF16)] * 2,
        scratch_shapes=[pltpu.VMEM((nb * 2 * DN_HEADS, DN_D, DN_D), F32)],
        compiler_params=_cparams(("parallel", "arbitrary")),
        name="deltanet",
    )(qkv, qkv, qkv, qkv, qkv, qkv, gate_col, gate_col, gate_row, gate_row, pc, pr)


def _attnprep_kernel(x_ref, dqw_ref, dkw_ref, qlw_ref, wuq_ref, klw_ref, wukv_ref,
                     mqw_ref, mkw_ref, cos_ref, sin_ref,
                     dq_ref, dk_ref, mq_ref, mk_ref, mv_ref):
    ones = _ones_bf16(LANES)
    row = lax.broadcasted_iota(jnp.int32, (LANES, LANES), 0)
    col = lax.broadcasted_iota(jnp.int32, (LANES, LANES), 1)
    half_ones = ((row // DIFF_D) == (col // DIFF_D)).astype(BF16)
    cos_t, sin_t = cos_ref[...], sin_ref[...]
    tm = x_ref.shape[1]
    lane_t = lax.broadcasted_iota(jnp.int32, (tm, LANES), 1)
    row_t = pl.program_id(1) * tm + lax.broadcasted_iota(jnp.int32, (tm, LANES), 0)
    q_flag = jnp.where(lane_t == MASK_LANE, 1.0, 0.0)
    k_flag = jnp.where((lane_t == MASK_LANE) & (row_t < N_PAD), NEG_BIG, 0.0)

    def rope(t):
        return t * cos_t + pltpu.roll(t, 64, 1) * sin_t

    for src, w_ref, dst in ((0, dqw_ref, dq_ref), (512, dkw_ref, dk_ref)):
        for h in range(DIFF_HEADS):
            y = x_ref[0, :, src + h * LANES:src + (h + 1) * LANES].astype(F32)
            ms = _dot((y * y).astype(BF16), half_ones) * (1.0 / DIFF_D)
            dst[0, :, h * LANES:(h + 1) * LANES] = (y * lax.rsqrt(ms + EPS) * w_ref[...]).astype(BF16)

    cq0 = x_ref[0, :, 1024:1152].astype(F32)
    cq1 = x_ref[0, :, 1152:1280].astype(F32)
    ms = _dot((cq0 * cq0 + cq1 * cq1).astype(BF16), ones) * (1.0 / MLA_Q_RANK)
    r = lax.rsqrt(ms + EPS)
    cqn = jnp.concatenate([cq0 * r * qlw_ref[:, 0:LANES], cq1 * r * qlw_ref[:, LANES:2 * LANES]],
                          axis=1).astype(BF16)
    q = _dot(cqn, wuq_ref[...])
    inv_d = 1.0 / (MLA_NOPE + MLA_ROPE)
    for h in range(MLA_HEADS):
        q0 = q[:, 256 * h:256 * h + LANES]
        q1 = q[:, 256 * h + LANES:256 * (h + 1)]
        ms = _dot((q0 * q0 + q1 * q1).astype(BF16), ones) * inv_d
        r = lax.rsqrt(ms + EPS)
        mq_ref[0, :, 256 * h:256 * h + LANES] = (q0 * r * mqw_ref[:, 0:LANES]).astype(BF16)
        mq_ref[0, :, 256 * h + LANES:256 * (h + 1)] = (
            rope(q1 * r * mqw_ref[:, LANES:2 * LANES]) + q_flag).astype(BF16)

    ckv = x_ref[0, :, 1280:1408].astype(F32)
    ms = _dot((ckv * ckv).astype(BF16), ones) * (1.0 / MLA_KV_RANK)
    ckvn = (ckv * lax.rsqrt(ms + EPS) * klw_ref[...]).astype(BF16)
    kv = _dot(ckvn, wukv_ref[...])
    mv_ref[0] = kv[:, 512:1024].astype(BF16)
    kr = x_ref[0, :, 1408:1536].astype(F32)
    kr2 = kr * kr
    for h in range(MLA_HEADS):
        k0 = kv[:, LANES * h:LANES * (h + 1)]
        ms = _dot((k0 * k0 + kr2).astype(BF16), ones) * inv_d
        r = lax.rsqrt(ms + EPS)
        mk_ref[0, :, 256 * h:256 * h + LANES] = (k0 * r * mkw_ref[:, 0:LANES]).astype(BF16)
        mk_ref[0, :, 256 * h + LANES:256 * (h + 1)] = (
            rope(kr * r * mkw_ref[:, LANES:2 * LANES]) + k_flag).astype(BF16)


def _attnprep(main, dqw, dkw, qlw, wuq, klw, wukv, mqw, mkw, cos_t, sin_t):
    b, lp, _ = main.shape
    tm = _row_tile(lp)

    def full(a):
        return pl.BlockSpec(a.shape, lambda i, j: (0,) * a.ndim)

    def rows(width):
        return pl.BlockSpec((1, tm, width), lambda i, j: (i, j, 0))

    return pl.pallas_call(
        _attnprep_kernel,
        grid=(b, lp // tm),
        in_specs=[pl.BlockSpec((1, tm, ATTN_COLS), lambda i, j: (i, j, 0)),
                  full(dqw), full(dkw), full(qlw), full(wuq), full(klw), full(wukv),
                  full(mqw), full(mkw),
                  pl.BlockSpec((tm, LANES), lambda i, j: (j, 0)),
                  pl.BlockSpec((tm, LANES), lambda i, j: (j, 0))],
        out_specs=[rows(512), rows(512), rows(1024), rows(1024), rows(512)],
        out_shape=[jax.ShapeDtypeStruct((b, lp, 512), BF16),
                   jax.ShapeDtypeStruct((b, lp, 512), BF16),
                   jax.ShapeDtypeStruct((b, lp, 1024), BF16),
                   jax.ShapeDtypeStruct((b, lp, 1024), BF16),
                   jax.ShapeDtypeStruct((b, lp, 512), BF16)],
        compiler_params=_cparams(("parallel", "parallel")),
        name="attnprep",
    )(main, dqw, dkw, qlw, wuq, klw, wukv, mqw, mkw, cos_t, sin_t)


KEY_PARTS = 3


def _key_splits(lp, parts=None):
    nblk = lp // LANES
    nparts = min(parts or KEY_PARTS, nblk)
    edges = [(nblk * i // nparts) * LANES for i in range(nparts + 1)]
    return tuple(zip(edges[:-1], edges[1:]))


def _diff_attend_t(score_fns, vt_ref, lam, lp):
    splits = _key_splits(lp)
    tq = LANES
    nt = len(score_fns)
    add = lambda x, y: x + y

    def scores(t):
        return [score_fns[t](k0, k1) for k0, k1 in splits]

    def softmax(st):
        m = functools.reduce(jnp.maximum, [x.max(axis=0, keepdims=True) for x in st])
        pt = [jnp.exp2(x - m) for x in st]
        lt = functools.reduce(add, [x.sum(axis=0, keepdims=True) for x in pt])
        l1, l2 = lt[:, 0:tq], lt[:, tq:2 * tq]
        r = lam * l1 / l2
        return [(x[:, 0:tq] - x[:, tq:2 * tq] * r).astype(BF16) for x in pt], 1.0 / l1

    def values(grp):
        acc = None
        for i, (k0, k1) in enumerate(splits):
            w = grp[0][0][i] if len(grp) == 1 else jnp.concatenate([grp[0][0][i], grp[1][0][i]], axis=1)
            part = _dot(vt_ref[0, :, k0:k1], w)
            acc = part if acc is None else acc + part
        return [acc[:, j * tq:(j + 1) * tq] * grp[j][1] for j in range(len(grp))]

    s = [scores(0)]
    sm, outs = [], []
    for t in range(nt):
        if t + 1 < nt:
            s.append(scores(t + 1))
        sm.append(softmax(s[t]))
        if t % 2 == 1:
            outs.extend(values(sm[t - 1:t + 1]))
    if nt % 2 == 1:
        outs.extend(values(sm[nt - 1:nt]))
    return outs


def _diffattn_kernel(q_ref, k_ref, vt_ref, bandt_ref, far_ref, lam_ref, sw_ref, o_ref, kx_ref):
    qi = pl.program_id(2)
    tq = LANES
    lp = k_ref.shape[1]
    nt = kx_ref.shape[0]
    nblk = lp // LANES
    c_neg = jnp.broadcast_to(far_ref[0, 0:1, :], (LANES, LANES)).astype(BF16)

    @pl.when(qi == 0)
    def _():
        c_pos = jnp.broadcast_to(far_ref[0, 1:2, :], (lp, LANES)).astype(BF16)
        for t in range(nt):
            kx_ref[t, :, 0:LANES] = k_ref[0]
            kx_ref[t, :, LANES:2 * LANES] = c_pos

    lane = lax.broadcasted_iota(jnp.int32, (tq, LANES), 1)
    rowi = lax.broadcasted_iota(jnp.int32, (tq, LANES), 0)
    eye = jnp.where(lane == rowi, 1.0, 0.0).astype(BF16)
    score_fns = []
    for t in range(nt):
        tile = qi * nt + t

        def put(kb, val, t=t):
            @pl.when((kb >= 0) & (kb < nblk))
            def _():
                kx_ref[t, pl.ds(pl.multiple_of(kb * LANES, LANES), LANES), LANES:2 * LANES] = val

        for back in range(nt):
            put(tile - 2 - back, c_neg)
        for dd in range(3):
            put(tile - 1 + dd, bandt_ref[0, dd])

        @pl.when(tile <= nt + 1)
        def _(t=t):
            kx_ref[t, 0:N_PAD, LANES:2 * LANES] = jnp.full((N_PAD, LANES), NEG_BIG, BF16)

        q = q_ref[0, t * tq:(t + 1) * tq, :]
        zero = jnp.zeros_like(q)
        lhs = jnp.concatenate([jnp.concatenate([jnp.where(lane < DIFF_D, q, zero), eye], axis=1),
                               jnp.concatenate([jnp.where(lane >= DIFF_D, q, zero), eye], axis=1)], axis=0)
        score_fns.append(lambda k0, k1, t=t, lhs=lhs: _dot_nt(kx_ref[t, k0:k1, :], lhs))
    for t, ot in enumerate(_diff_attend_t(score_fns, vt_ref, lam_ref[0:1, 0:1], lp)):
        o = ot.T
        ms = jnp.mean(o * o, axis=-1, keepdims=True)
        o_ref[0, t * tq:(t + 1) * tq, :] = (o * lax.rsqrt(ms + EPS) * sw_ref[...]).astype(BF16)


DIFF_TILES = 3


def _diffattn(dq, dk, dvt, bandt, far, lam, sw):
    b, lp, _ = dq.shape
    nblk = lp // LANES
    nt = DIFF_TILES
    rows = nt * LANES
    return pl.pallas_call(
        _diffattn_kernel,
        grid=(b, DIFF_HEADS, pl.cdiv(nblk, nt)),
        in_specs=[pl.BlockSpec((1, rows, LANES), lambda i, h, j: (i, j, h)),
                  pl.BlockSpec((1, lp, LANES), lambda i, h, j: (i, 0, h)),
                  pl.BlockSpec((1, LANES, lp), lambda i, h, j: (i, h, 0)),
                  pl.BlockSpec((1, 3, LANES, LANES), lambda i, h, j: (h, 0, 0, 0)),
                  pl.BlockSpec((1, 2, LANES), lambda i, h, j: (h, 0, 0)),
                  pl.BlockSpec((1, LANES), lambda i, h, j: (0, 0)),
                  pl.BlockSpec((1, LANES), lambda i, h, j: (0, 0))],
        out_specs=pl.BlockSpec((1, rows, LANES), lambda i, h, j: (i, j, h)),
        out_shape=jax.ShapeDtypeStruct((b, lp, DIFF_HEADS * LANES), BF16),
        scratch_shapes=[pltpu.VMEM((nt, lp, 2 * LANES), BF16)],
        compiler_params=_cparams(("parallel", "parallel", "arbitrary")),
        name="diffattn",
    )(dq, dk, dvt, bandt, far, lam, sw)


def _mlaattn_kernel(q_ref, k_ref, v_ref, o_ref, vx_ref):
    lp = k_ref.shape[1]

    @pl.when(pl.program_id(2) == 0)
    def _():
        vx_ref[:, 0:LANES] = v_ref[0]
        vx_ref[:, LANES:2 * LANES] = jnp.ones((lp, LANES), BF16)

    q = q_ref[0]
    splits = _key_splits(lp, MLA_KEY_PARTS)
    s = [_dot_nt(q, k_ref[0, k0:k1, :]) for k0, k1 in splits]
    m = [x.max(axis=1, keepdims=True) for x in s]
    p = [jnp.exp2((x - mi).astype(BF16)) for x, mi in zip(s, m)]
    o = [_dot(pi, vx_ref[k0:k1, :]) for (k0, k1), pi in zip(splits, p)]
    m_all = functools.reduce(jnp.maximum, m)
    acc = functools.reduce(lambda a, b: a + b, [oi * jnp.exp2(mi - m_all) for oi, mi in zip(o, m)])
    o_ref[0] = (acc[:, 0:LANES] / acc[:, LANES:LANES + 1]).astype(BF16)


MLA_KEY_PARTS = 4
MLA_Q_STEPS = 4


def _mlaattn(mq, mk, mv):
    b, lp, _ = mq.shape
    tq = lp // MLA_Q_STEPS if lp % (16 * MLA_Q_STEPS) == 0 else LANES
    return pl.pallas_call(
        _mlaattn_kernel,
        grid=(b, MLA_HEADS, lp // tq),
        in_specs=[pl.BlockSpec((1, tq, 256), lambda i, h, j: (i, j, h)),
                  pl.BlockSpec((1, lp, 256), lambda i, h, j: (i, 0, h)),
                  pl.BlockSpec((1, lp, LANES), lambda i, h, j: (i, 0, h))],
        out_specs=pl.BlockSpec((1, tq, LANES), lambda i, h, j: (i, j, h)),
        out_shape=jax.ShapeDtypeStruct((b, lp, MLA_HEADS * LANES), BF16),
        scratch_shapes=[pltpu.VMEM((lp, 2 * LANES), BF16)],
        compiler_params=_cparams(("parallel", "parallel", "arbitrary")),
        name="mlaattn",
    )(mq, mk, mv)


def _outproj_kernel(of_ref, ob_ref, z_ref, df_ref, ml_ref, h_ref, dnw_ref, w_ref, o_ref):
    ones = _ones_bf16(LANES)
    parts = []
    for h in range(DN_HEADS):
        sl = slice(h * DN_D, (h + 1) * DN_D)
        o = of_ref[0, :, sl].astype(F32) + ob_ref[0, :, sl].astype(F32)
        ms = _dot((o * o).astype(BF16), ones) * (1.0 / DN_D)
        z = z_ref[0, :, sl].astype(F32)
        parts.append((o * lax.rsqrt(ms + EPS) * dnw_ref[...] * (z * jax.nn.sigmoid(z))).astype(BF16))
    ydn = jnp.concatenate(parts, axis=1)
    acc = h_ref[0] + _dot(ydn, w_ref[0:512, :])
    acc = acc + _dot(df_ref[0], w_ref[512:1024, :])
    acc = acc + _dot(ml_ref[0], w_ref[1024:1536, :])
    o_ref[0] = acc


def _outproj(o_f, o_b, main, ydf, yml, h, dnw, w):
    b, lp, d = h.shape
    tm = _row_tile(lp)

    def rows(width, cb=0):
        return pl.BlockSpec((1, tm, width), lambda i, j: (i, j, cb))

    return pl.pallas_call(
        _outproj_kernel,
        grid=(b, lp // tm),
        in_specs=[rows(512), rows(512), rows(512, (DN_COL0 + 1536) // 512), rows(512), rows(512), rows(d),
                  pl.BlockSpec((1, LANES), lambda i, j: (0, 0)),
                  pl.BlockSpec(w.shape, lambda i, j: (0, 0))],
        out_specs=rows(d),
        out_shape=jax.ShapeDtypeStruct((b, lp, d), F32),
        compiler_params=_cparams(("parallel", "parallel")),
        name="outproj",
    )(o_f, o_b, main, ydf, yml, h, dnw, w)


def _ffn_kernel(h_ref, nw_ref, wg_ref, wu_ref, wd_ref, o_ref, a_ref):
    x = h_ref[...]
    ms = jnp.mean(x * x, axis=-1, keepdims=True)
    u = (x * lax.rsqrt(ms + EPS) * nw_ref[...]).astype(BF16)
    f = wg_ref.shape[1]
    fc = 256
    for c0 in range(0, f, fc):
        g = _dot(u, wg_ref[:, c0:c0 + fc])
        up = _dot(u, wu_ref[:, c0:c0 + fc])
        a_ref[:, c0:c0 + fc] = (g * jax.nn.sigmoid(g) * up).astype(BF16)
    o_ref[...] = x + _dot(a_ref[...], wd_ref[...])


def _ffn(h2, nw, wg, wu, wd):
    n, d = h2.shape
    f = wg.shape[1]
    tm = 512 if n % 512 == 0 else 128

    def const(a):
        return pl.BlockSpec(a.shape, lambda i: (0, 0), pipeline_mode=pl.Buffered(1))

    return pl.pallas_call(
        _ffn_kernel,
        grid=(n // tm,),
        in_specs=[pl.BlockSpec((tm, d), lambda i: (i, 0)),
                  pl.BlockSpec((1, d), lambda i: (0, 0)),
                  const(wg), const(wu), const(wd)],
        out_specs=pl.BlockSpec((tm, d), lambda i: (i, 0)),
        out_shape=jax.ShapeDtypeStruct((n, d), F32),
        scratch_shapes=[pltpu.VMEM((tm, f), BF16)],
        compiler_params=_cparams(("parallel",)),
        name="ffn",
    )(h2, nw, wg, wu, wd)


def _router_kernel(h_ref, nw_ref, rw_ref, u_ref, idx_ref, gate_ref):
    x = h_ref[...]
    ms = jnp.mean(x * x, axis=-1, keepdims=True)
    u = x * lax.rsqrt(ms + EPS) * nw_ref[...]
    u_hi = u.astype(BF16)
    u_ref[...] = u_hi
    u_lo = (u - u_hi.astype(F32)).astype(BF16)
    rw = rw_ref[...]
    rw_hi = rw.astype(BF16)
    rw_lo = (rw - rw_hi.astype(F32)).astype(BF16)
    logits = _dot(u_hi, rw_hi) + (_dot(u_lo, rw_hi) + _dot(u_hi, rw_lo))
    lane = lax.broadcasted_iota(jnp.int32, logits.shape, 1)
    logits = jnp.where(lane < N_EXPERTS, logits, -jnp.inf)
    m1 = jnp.max(logits, axis=-1, keepdims=True)
    i1 = jnp.min(jnp.where(logits == m1, lane, LANES), axis=-1, keepdims=True)
    rest = jnp.where(lane == i1, -jnp.inf, logits)
    m2 = jnp.max(rest, axis=-1, keepdims=True)
    i2 = jnp.min(jnp.where(rest == m2, lane, LANES), axis=-1, keepdims=True)
    e2 = jnp.exp(m2 - m1)
    g1 = 1.0 / (1.0 + e2)
    g2 = e2 / (1.0 + e2)
    idx_ref[...] = jnp.where(lane == 0, i1, jnp.where(lane == 1, i2, 0))
    gate_ref[...] = jnp.where(lane == 0, g1, jnp.where(lane == 1, g2, 0.0))


def _router(h2, nw, rw):
    n, d = h2.shape
    tm = 512 if n % 512 == 0 else 128
    return pl.pallas_call(
        _router_kernel,
        grid=(n // tm,),
        in_specs=[pl.BlockSpec((tm, d), lambda i: (i, 0)),
                  pl.BlockSpec((1, d), lambda i: (0, 0)),
                  pl.BlockSpec((d, LANES), lambda i: (0, 0))],
        out_specs=[pl.BlockSpec((tm, d), lambda i: (i, 0)),
                   pl.BlockSpec((tm, LANES), lambda i: (i, 0)),
                   pl.BlockSpec((tm, LANES), lambda i: (i, 0))],
        out_shape=[jax.ShapeDtypeStruct((n, d), BF16),
                   jax.ShapeDtypeStruct((n, LANES), jnp.int32),
                   jax.ShapeDtypeStruct((n, LANES), F32)],
        compiler_params=_cparams(("parallel",)),
        name="router",
    )(h2, nw, rw)


def _moe_kernel(te_ref, tv_ref, x_ref, wg_ref, wu_ref, wd_ref, o_ref, acc_ref):
    i = pl.program_id(0)
    f = pl.program_id(1)

    @pl.when(f == 0)
    def _():
        acc_ref[...] = jnp.zeros(acc_ref.shape, F32)

    @pl.when(tv_ref[i] > 0)
    def _():
        x = x_ref[...]
        g = _dot(x, wg_ref[0])
        up = _dot(x, wu_ref[0])
        a = (g * jax.nn.sigmoid(g) * up).astype(BF16)
        acc_ref[...] += _dot(a, wd_ref[0])

    @pl.when(f == pl.num_programs(1) - 1)
    def _():
        o_ref[...] = acc_ref[...].astype(o_ref.dtype)


MOE_TM = 512
MOE_F_STEPS = 2


def _moe(x_sorted, tile_expert, tile_valid, wg, wu, wd, tm):
    p, d = x_sorted.shape
    f = wg.shape[2]
    nf = MOE_F_STEPS
    fh = f // nf
    grid_spec = pltpu.PrefetchScalarGridSpec(
        num_scalar_prefetch=2,
        grid=(p // tm, nf),
        in_specs=[pl.BlockSpec((tm, d), lambda i, j, te, tv: (i, 0)),
                  pl.BlockSpec((1, d, fh), lambda i, j, te, tv: (te[i], 0, j)),
                  pl.BlockSpec((1, d, fh), lambda i, j, te, tv: (te[i], 0, j)),
                  pl.BlockSpec((1, fh, d), lambda i, j, te, tv: (te[i], j, 0))],
        out_specs=pl.BlockSpec((tm, d), lambda i, j, te, tv: (i, 0)),
        scratch_shapes=[pltpu.VMEM((tm, d), F32)],
    )
    return pl.pallas_call(
        _moe_kernel,
        grid_spec=grid_spec,
        out_shape=jax.ShapeDtypeStruct((p, d), BF16),
        compiler_params=_cparams(("arbitrary", "arbitrary")),
        name="moe",
    )(tile_expert, tile_valid, x_sorted, wg, wu, wd)


def _combine_kernel(h_ref, y0_ref, y1_ref, g_ref, o_ref):
    g = g_ref[...]
    o_ref[...] = (h_ref[...] + g[:, 0:1] * y0_ref[...].astype(F32)
                  + g[:, 1:2] * y1_ref[...].astype(F32))


def _combine(h2, y0, y1, gates):
    n, d = h2.shape
    tm = 1024 if n % 1024 == 0 else 128
    row = lambda w: pl.BlockSpec((tm, w), lambda i: (i, 0))
    return pl.pallas_call(
        _combine_kernel,
        grid=(n // tm,),
        in_specs=[row(d), row(d), row(d), row(LANES)],
        out_specs=row(d),
        out_shape=jax.ShapeDtypeStruct((n, d), F32),
        compiler_params=_cparams(("parallel",)),
        name="combine",
    )(h2, y0, y1, gates)


def _spread_rope_cols(w):
    z = jnp.zeros(w.shape[:-1] + (32,), w.dtype)
    return jnp.concatenate([w[..., 0:32], z, w[..., 32:64], z], axis=-1)


def _relayout_w_in(w):
    dnw = 4 * DN_HEADS * DN_D
    n_gate = 4 * DN_HEADS
    rest = w[:, dnw + n_gate:]
    nqk = 4 * DIFF_HEADS * DIFF_D
    nv = DIFF_HEADS * LANES
    diff_qk = rest[:, :nqk]
    lat = rest[:, nqk + nv:rest.shape[1] - MLA_ROPE]
    kr = _spread_rope_cols(rest[:, rest.shape[1] - MLA_ROPE:])
    gates = jnp.concatenate([w[:, dnw:dnw + n_gate],
                             jnp.zeros((w.shape[0], GATE_COLS - n_gate), w.dtype)], axis=1)
    out = jnp.concatenate([diff_qk, lat, kr, w[:, :dnw], gates], axis=1)
    assert out.shape[1] == MAIN_COLS + GATE_COLS
    return out.astype(BF16)


def _relayout_w_uq(w):
    per = MLA_NOPE + MLA_ROPE
    heads = [jnp.concatenate([w[:, h * per:h * per + MLA_NOPE],
                              _spread_rope_cols(w[:, h * per + MLA_NOPE:(h + 1) * per])], axis=1)
             for h in range(MLA_HEADS)]
    return jnp.concatenate(heads, axis=1).astype(BF16)


def _relayout_w_ukv(w):
    per = MLA_NOPE + LANES
    ks = [w[:, h * per:h * per + MLA_NOPE] for h in range(MLA_HEADS)]
    vs = [w[:, h * per + MLA_NOPE:(h + 1) * per] for h in range(MLA_HEADS)]
    return jnp.concatenate(ks + vs, axis=1).astype(BF16)


def _diff_v_weight_t(w):
    c0 = 4 * DN_HEADS * DN_D + 4 * DN_HEADS + 4 * DIFF_HEADS * DIFF_D
    return w[:, c0:c0 + DIFF_HEADS * LANES].T.astype(BF16)


def _qk_norm_vec(w, scale):
    return (jnp.concatenate([w[:MLA_NOPE], _spread_rope_cols(w[MLA_NOPE:])]) * scale)[None, :]


def _t5_bucket(rel):
    half = T5_BUCKETS // 2
    exact = half // 2
    n = jnp.abs(rel)
    large = exact + (jnp.log(jnp.maximum(n, exact).astype(F32) / exact)
                     / math.log(T5_MAX_DIST / exact) * (half - exact)).astype(jnp.int32)
    large = jnp.minimum(large, half - 1)
    return jnp.where(rel > 0, half, 0) + jnp.where(n < exact, n, large)


def _t5_tables(table, tq):
    i = jnp.arange(tq)[None, :]
    j = jnp.arange(LANES)[:, None]
    rels = jnp.stack([j - i + (d * LANES) for d in (-1, 0, 1)])
    table = table.astype(F32) * LOG2E
    bandt = jnp.transpose(table[_t5_bucket(rels)], (3, 0, 1, 2)).astype(BF16)
    far_rel = jnp.array([-T5_MAX_DIST, T5_MAX_DIST], jnp.int32)
    far = jnp.transpose(table[_t5_bucket(far_rel)], (1, 0))
    far = jnp.broadcast_to(far[:, :, None], far.shape + (LANES,))
    return bandt, far


def _rope_tables(lp):
    half = MLA_ROPE // 2
    inv = 1.0 / (ROPE_THETA ** (jnp.arange(half, dtype=F32) / half))
    pos = (jnp.arange(lp, dtype=jnp.int32) - N_PAD).astype(F32)
    ang = pos[:, None] * inv[None, :]
    c, s, z = jnp.cos(ang), jnp.sin(ang), jnp.zeros((lp, half), F32)
    return jnp.concatenate([c, z, c, z], axis=1), jnp.concatenate([-s, z, s, z], axis=1)


def _exclusive_cumsum(onehot):
    rows, cols = onehot.shape
    blk = 256 if rows % 256 == 0 else rows
    x = onehot.reshape(rows // blk, blk, cols)
    tri = (jnp.arange(blk)[:, None] > jnp.arange(blk)[None, :]).astype(BF16)
    within = jnp.einsum("ij,bjk->bik", tri, x.astype(BF16), preferred_element_type=F32).astype(jnp.int32)
    totals = jnp.sum(x, axis=1)
    offsets = jnp.cumsum(totals, axis=0) - totals
    return (within + offsets[:, None, :]).reshape(rows, cols)


def _moe_plan(idx, n, tm):
    e_flat = idx.reshape(-1)
    onehot = (e_flat[:, None] == jnp.arange(N_EXPERTS)[None, :]).astype(jnp.int32)
    counts = jnp.sum(onehot, axis=0)
    rank = jnp.sum(_exclusive_cumsum(onehot) * onehot, axis=1)
    tiles_per = (counts + tm - 1) // tm
    tile_end = jnp.cumsum(tiles_per)
    start = (tile_end - tiles_per) * tm
    dest = start[e_flat] + rank
    n_tiles = (2 * n) // tm + N_EXPERTS
    p = n_tiles * tm
    slot_tok = jnp.zeros((p,), jnp.int32).at[dest].set(jnp.arange(2 * n, dtype=jnp.int32) // 2)
    t = jnp.arange(n_tiles, dtype=jnp.int32)
    tile_expert = jnp.minimum(jnp.sum((t[:, None] >= tile_end[None, :]).astype(jnp.int32), axis=1),
                              N_EXPERTS - 1).astype(jnp.int32)
    tile_valid = (t < tile_end[-1]).astype(jnp.int32)
    return slot_tok, dest.reshape(n, 2), tile_expert, tile_valid


def kernel(x, meta_tokens, rel_bias_table, attn_norm_w, w_in, dn_conv_w, dn_a_log, dn_dt_bias, dn_norm_w, diff_q_norm_w, diff_k_norm_w, diff_lambda, diff_subln_w, mla_q_lat_norm_w, mla_w_uq, mla_kv_lat_norm_w, mla_w_ukv, mla_q_norm_w, mla_k_norm_w, w_out, ffn_norm_w, ffn_w_gate, ffn_w_up, ffn_w_down, router_w, moe_w_gate, moe_w_up, moe_w_down):
    bsz, seq, d = x.shape
    depth = w_in.shape[0]
    lp = ROW0 + seq
    n = bsz * lp
    assert lp % LANES == 0 and meta_tokens.shape[0] == N_META

    meta = jnp.broadcast_to(meta_tokens[None].astype(x.dtype), (bsz, N_META, d))
    h = jnp.concatenate([jnp.zeros((bsz, N_PAD, d), x.dtype), meta, x], axis=1)

    cos_t, sin_t = _rope_tables(lp)
    bandt, far = _t5_tables(rel_bias_table, LANES)
    n_chunks = lp // DN_CHUNK

    for l in range(depth):
        main, gate, dvt = _inproj(h, attn_norm_w[l][None, :], _relayout_w_in(w_in[l]),
                                  _diff_v_weight_t(w_in[l]))

        qkv = _dnprep(main, dn_conv_w[l])
        gate_row = jnp.transpose(gate[:, :, 0:16].reshape(bsz, n_chunks, DN_CHUNK, 16), (0, 1, 3, 2))
        neg_a = -jnp.exp(dn_a_log[l].astype(F32)).reshape(-1)
        dtb = dn_dt_bias[l].astype(F32).reshape(-1)
        z8 = jnp.zeros((8,), F32)
        pc = jnp.stack([jnp.concatenate([z8, neg_a, jnp.zeros((GATE_COLS - 16,), F32)]),
                        jnp.concatenate([z8, dtb, jnp.zeros((GATE_COLS - 16,), F32)])])
        pr = jnp.concatenate([jnp.broadcast_to(jnp.concatenate([z8, neg_a])[:, None], (16, DN_CHUNK)),
                              jnp.broadcast_to(jnp.concatenate([z8, dtb])[:, None], (16, DN_CHUNK))], axis=1)
        o_f, o_b = _deltanet(qkv, gate, gate_row, pc, pr)

        dscale = DIFF_D ** -0.5 * LOG2E
        mscale = (MLA_NOPE + MLA_ROPE) ** -0.5 * LOG2E
        dq, dk, mq, mk, mv = _attnprep(
            main,
            (jnp.tile(diff_q_norm_w[l], 2) * dscale)[None, :],
            jnp.tile(diff_k_norm_w[l], 2)[None, :],
            mla_q_lat_norm_w[l][None, :], _relayout_w_uq(mla_w_uq[l]),
            mla_kv_lat_norm_w[l][None, :], _relayout_w_ukv(mla_w_ukv[l]),
            _qk_norm_vec(mla_q_norm_w[l], mscale), _qk_norm_vec(mla_k_norm_w[l], 1.0),
            cos_t, sin_t)
        lam_init = 0.8 - 0.6 * math.exp(-0.3 * l)
        lpar = diff_lambda[l].astype(F32)
        lam = jnp.exp(jnp.sum(lpar[0] * lpar[1])) - jnp.exp(jnp.sum(lpar[2] * lpar[3])) + lam_init
        ydf = _diffattn(dq, dk, dvt, bandt, far, jnp.full((1, LANES), lam, F32),
                        (diff_subln_w[l] * (1.0 - lam_init))[None, :])
        yml = _mlaattn(mq, mk, mv)

        h = _outproj(o_f, o_b, main, ydf, yml, h, dn_norm_w[l][None, :], w_out[l].astype(BF16))

        h2 = h.reshape(n, d)
        if l % 2 == 0:
            i = l // 2
            h2 = _ffn(h2, ffn_norm_w[l][None, :], ffn_w_gate[i].astype(BF16),
                      ffn_w_up[i].astype(BF16), ffn_w_down[i].astype(BF16))
        else:
            i = l // 2
            rw = jnp.concatenate([router_w[i], jnp.zeros((d, LANES - N_EXPERTS), F32)], axis=1)
            u, idx, gates = _router(h2, ffn_norm_w[l][None, :], rw)
            tm = MOE_TM if (2 * n) % MOE_TM == 0 else 128
            slot_tok, tok_slots, tile_expert, tile_valid = _moe_plan(idx[:, 0:2], n, tm)
            x_sorted = jnp.take(u, slot_tok, axis=0)
            y_sorted = _moe(x_sorted, tile_expert, tile_valid, moe_w_gate[i].astype(BF16),
                            moe_w_up[i].astype(BF16), moe_w_down[i].astype(BF16), tm)
            h2 = _combine(h2, jnp.take(y_sorted, tok_slots[:, 0], axis=0),
                          jnp.take(y_sorted, tok_slots[:, 1], axis=0), gates)
        h = h2.reshape(bsz, lp, d)
    return h[:, ROW0:]
```

```python
import functools
import math

import jax
import jax.numpy as jnp
from jax import lax
from jax.experimental import pallas as pl
from jax.experimental.pallas import tpu as pltpu

F32 = jnp.float32
BF16 = jnp.bfloat16

N_META = 16
N_PAD = 112
ROW0 = N_PAD + N_META
DN_HEADS = 4
DN_D = 128
DN_CONV = 5
DN_CHUNK = 64
DIFF_HEADS = 4
DIFF_D = 64
MLA_HEADS = 4
MLA_NOPE = 128
MLA_ROPE = 64
MLA_Q_RANK = 256
MLA_KV_RANK = 128
ROPE_THETA = 10000.0
T5_BUCKETS = 32
T5_MAX_DIST = 128
N_EXPERTS = 8
EPS = 1e-6
NEG_BIG = -1e30
LOG2E = 1.4426950408889634
MASK_LANE = 32
LANES = 128
VMEM_LIMIT = 56 * 1024 * 1024

ATTN_COLS = 1536
DN_COL0 = ATTN_COLS
MAIN_COLS = 3584
GATE_COLS = 128


def _dot(a, b):
    return jnp.dot(a, b, preferred_element_type=F32)


def _dot_nt(a, b):
    return lax.dot_general(a, b, (((1,), (1,)), ((), ())), preferred_element_type=F32)


def _ones_bf16(n):
    return jnp.ones((n, n), BF16)


def _row_tile(lp):
    return 384 if lp % 384 == 0 else 128


def _cparams(sem):
    return pltpu.CompilerParams(dimension_semantics=sem, vmem_limit_bytes=VMEM_LIMIT)


def _inproj_kernel(h_ref, nw_ref, w_ref, wvt_ref, main_ref, gate_ref, vt_ref):
    x = h_ref[0]
    ms = jnp.mean(x * x, axis=-1, keepdims=True)
    u = (x * lax.rsqrt(ms + EPS) * nw_ref[...]).astype(BF16)
    for c0 in range(0, MAIN_COLS, 512):
        main_ref[0, :, c0:c0 + 512] = _dot(u, w_ref[:, c0:c0 + 512]).astype(BF16)
    gate_ref[0] = _dot(u, w_ref[:, MAIN_COLS:MAIN_COLS + GATE_COLS])
    vt_ref[0] = _dot_nt(wvt_ref[...], u).astype(BF16)


def _inproj(h, norm_w, w, wvt):
    b, lp, d = h.shape
    tm = _row_tile(lp)
    ncol = MAIN_COLS + GATE_COLS
    nv = wvt.shape[0]
    return pl.pallas_call(
        _inproj_kernel,
        grid=(b, lp // tm),
        in_specs=[
            pl.BlockSpec((1, tm, d), lambda i, j: (i, j, 0)),
            pl.BlockSpec((1, d), lambda i, j: (0, 0)),
            pl.BlockSpec((d, ncol), lambda i, j: (0, 0)),
            pl.BlockSpec((nv, d), lambda i, j: (0, 0)),
        ],
        out_specs=[
            pl.BlockSpec((1, tm, MAIN_COLS), lambda i, j: (i, j, 0)),
            pl.BlockSpec((1, tm, GATE_COLS), lambda i, j: (i, j, 0)),
            pl.BlockSpec((1, nv, tm), lambda i, j: (i, 0, j)),
        ],
        out_shape=[
            jax.ShapeDtypeStruct((b, lp, MAIN_COLS), BF16),
            jax.ShapeDtypeStruct((b, lp, GATE_COLS), F32),
            jax.ShapeDtypeStruct((b, nv, lp), BF16),
        ],
        compiler_params=_cparams(("parallel", "parallel")),
        name="inproj",
    )(h, norm_w, w, wvt)


def _dnprep_kernel(x_ref, cw_ref, o_ref, xs_ref):
    j = pl.program_id(1)
    lp = x_ref.shape[1]
    halo = 8
    xs_ref[0:halo + N_PAD, :] = jnp.zeros((halo + N_PAD, LANES), F32)
    xs_ref[halo + N_PAD:halo + lp, :] = x_ref[0, N_PAD:, :].astype(F32)
    xs_ref[halo + lp:2 * halo + lp, :] = jnp.zeros((halo, LANES), F32)
    q_scale = jnp.where(j < DN_HEADS, DN_D ** -0.5, 1.0).astype(F32)
    rc = 128

    def conv_silu(base):
        acc = cw_ref[0:1, :] * xs_ref[base + halo - 2:base + halo - 2 + rc, :]
        for t in range(1, DN_CONV):
            off = base + halo - 2 + t
            acc = acc + cw_ref[t:t + 1, :] * xs_ref[off:off + rc, :]
        return acc * jax.nn.sigmoid(acc)

    @pl.when(j < 2 * DN_HEADS)
    def _():
        for c in range(lp // rc):
            y = conv_silu(c * rc)
            ss = jnp.sum(y * y, axis=-1, keepdims=True)
            o_ref[0, c * rc:(c + 1) * rc, :] = (y * (lax.rsqrt(ss + EPS) * q_scale)).astype(BF16)

    @pl.when(j >= 2 * DN_HEADS)
    def _():
        for c in range(lp // rc):
            o_ref[0, c * rc:(c + 1) * rc, :] = conv_silu(c * rc).astype(BF16)


def _dnprep(main, conv_w):
    b, lp, _ = main.shape
    ncb = 3 * DN_HEADS
    return pl.pallas_call(
        _dnprep_kernel,
        grid=(b, ncb),
        in_specs=[
            pl.BlockSpec((1, lp, LANES), lambda i, j: (i, 0, DN_COL0 // LANES + j)),
            pl.BlockSpec((DN_CONV, LANES), lambda i, j: (0, j)),
        ],
        out_specs=pl.BlockSpec((1, lp, LANES), lambda i, j: (i, 0, j)),
        out_shape=jax.ShapeDtypeStruct((b, lp, ncb * LANES), BF16),
        scratch_shapes=[pltpu.VMEM((lp + 16, LANES), F32)],
        compiler_params=_cparams(("parallel", "parallel")),
        name="dnprep",
    )(main, conv_w)


def _each(fn, *lists):
    return [fn(*args) for args in zip(*lists)]


def _unit_tri_inverse(a_list, row, col, hi):
    c = row.shape[0]
    bf = lambda t: t.astype(BF16)

    def mm(x, y):
        z = jnp.zeros_like(y)
        return _dot(x, jnp.concatenate([jnp.where(hi, z, y), jnp.where(hi, y, z)], axis=0))

    eye = (row == col).astype(F32)
    same = lambda n: (row // n) == (col // n)
    blk16 = same(16)
    a0 = _each(lambda a: jnp.where(blk16, a, 0.0), a_list)
    a0b = _each(bf, a0)
    a2b = _each(lambda x: bf(mm(x, x)), a0b)
    a4b = _each(lambda x: bf(mm(x, x)), a2b)
    a8b = _each(lambda x: bf(mm(x, x)), a4b)
    p = _each(lambda x: eye - x, a0)
    for pw in (a2b, a4b, a8b):
        p = _each(lambda x, y: x + mm(bf(x), y), p, pw)
    size = 16
    while size < c:
        inner, outer = same(size), same(2 * size)
        off = _each(lambda a: bf(jnp.where(outer & (~inner), a, 0.0)), a_list)
        pb = _each(bf, p)
        mid = _each(lambda x, y: bf(mm(x, y)), pb, off)
        p = _each(lambda x, m, xb: x - mm(m, xb), p, mid, pb)
        size *= 2
    return p


def _dn_kernel(qf_ref, kf_ref, vf_ref, qb_ref, kb_ref, vb_ref, gcf_ref, gcb_ref,
               grf_ref, grb_ref, pc_ref, pr_ref, of_ref, ob_ref, st_ref):
    s = pl.program_id(1)
    c = DN_CHUNK
    nb = qf_ref.shape[0]

    @pl.when(s == 0)
    def _():
        st_ref[...] = jnp.zeros(st_ref.shape, F32)

    row = lax.broadcasted_iota(jnp.int32, (c, c), 0)
    col = lax.broadcasted_iota(jnp.int32, (c, c), 1)
    negA_c, dtb_c = pc_ref[0:1, :], pc_ref[1:2, :]
    negA_r, dtb_r = pr_ref[:, 0:c], pr_ref[:, c:2 * c]
    bf = lambda t: t.astype(BF16)
    lead = ROW0 // c

    per_dir = []
    for d, (gc_ref, gr_ref) in enumerate(((gcf_ref, grf_ref), (gcb_ref, grb_ref))):
        if d == 0:
            causal = s >= 0
            chunk = s
        else:
            causal = s < lead
            chunk = jnp.where(s < lead, s, (pl.num_programs(1) - 1 + lead) - s)
        sgn = jnp.where(causal, 1, -1)
        m_b = jnp.where((row - col) * sgn >= 0, 1.0, 0.0).astype(BF16)
        m_tb = jnp.where((col - row) * sgn >= 0, 1.0, 0.0).astype(BF16)
        pos_c = chunk * c + lax.broadcasted_iota(jnp.int32, (c, LANES), 0)
        live_c = pos_c >= N_PAD
        pos_r = chunk * c + lax.broadcasted_iota(jnp.int32, (16, c), 1)
        gates = []
        for bi in range(nb):
            gcol = gc_ref[bi]
            beta_c = jnp.where(live_c, jax.nn.sigmoid(gcol), 0.0)
            g_c = jnp.where(live_c, negA_c * jax.nn.softplus(gcol + dtb_c), 0.0)
            grow = gr_ref[bi, 0]
            g_r = jnp.where(pos_r >= N_PAD, negA_r * jax.nn.softplus(grow + dtb_r), 0.0)
            g_c_hi = bf(g_c)
            g_c_lo = bf(g_c - g_c_hi.astype(F32))
            gc_c = _dot(m_b, g_c_hi) + _dot(m_b, g_c_lo)
            g_r_hi = bf(g_r)
            g_r_lo = bf(g_r - g_r_hi.astype(F32))
            gc_r = _dot(g_r_hi, m_tb) + _dot(g_r_lo, m_tb)
            gtot_c = jnp.sum(g_c, axis=0, keepdims=True)
            gates.append(dict(beta_c=beta_c, gc_c=gc_c, gc_r=gc_r, eg_c=jnp.exp(gc_c),
                              ew_c=jnp.exp(gtot_c - gc_c), et_c=jnp.exp(gtot_c)))
        per_dir.append(dict(sgn=sgn, gates=gates))

    refs = ((qf_ref, kf_ref, vf_ref, of_ref), (qb_ref, kb_ref, vb_ref, ob_ref))
    chains = [(bi, d, h) for bi in range(nb) for d in range(2) for h in range(DN_HEADS)]
    sl = lambda h: slice(h * DN_D, (h + 1) * DN_D)
    bcol = lambda d, h: d * DN_HEADS + h
    acol = lambda d, h: 2 * DN_HEADS + d * DN_HEADS + h
    sidx = lambda bi, d, h: (bi * 2 + d) * DN_HEADS + h
    gate = lambda bi, d, name: per_dir[d]["gates"][bi][name]

    q = [refs[d][0][bi, :, sl(h)] for bi, d, h in chains]
    k = [refs[d][1][bi, :, sl(h)] for bi, d, h in chains]
    v = [refs[d][2][bi, :, sl(h)] for bi, d, h in chains]
    kf = _each(lambda t: t.astype(F32), k)
    beta = [gate(bi, d, "beta_c")[:, bcol(d, h):bcol(d, h) + 1] for bi, d, h in chains]
    eg = [gate(bi, d, "eg_c")[:, acol(d, h):acol(d, h) + 1] for bi, d, h in chains]
    kb = _each(lambda x, b: x * b, kf, beta)
    vb = _each(lambda x, b: x.astype(F32) * b, v, beta)

    assert 2 * c == LANES and DN_HEADS % 2 == 0
    pairs = [(2 * i, 2 * i + 1) for i in range(len(chains) // 2)]
    rowp = lax.broadcasted_iota(jnp.int32, (c, LANES), 0)
    lanep = lax.broadcasted_iota(jnp.int32, (c, LANES), 1)
    hi = lanep >= c
    colp = jnp.where(hi, lanep - c, lanep)
    zero_kd = jnp.zeros((c, DN_D), BF16)
    cat0 = lambda *xs: jnp.concatenate(xs, axis=0)
    cat1 = lambda *xs: jnp.concatenate(xs, axis=1)

    def decay_of(ia, ib):
        (bi, d, ha), (_, _, hb) = chains[ia], chains[ib]
        m_in = (rowp - colp) * per_dir[d]["sgn"] >= 0
        gc_c, gc_r = gate(bi, d, "gc_c"), gate(bi, d, "gc_r")
        colb = jnp.where(hi, gc_c[:, acol(d, hb):acol(d, hb) + 1], gc_c[:, acol(d, ha):acol(d, ha) + 1])
        rowb = cat1(gc_r[acol(d, ha):acol(d, ha) + 1, :], gc_r[acol(d, hb):acol(d, hb) + 1, :])
        return jnp.where(m_in, jnp.exp(jnp.where(m_in, colb - rowb, 0.0)), 0.0)

    decay = [decay_of(ia, ib) for ia, ib in pairs]
    kq = [_dot_nt(cat1(cat0(bf(kb[ia]), q[ia]), cat0(bf(kb[ib]), q[ib])),
                  cat0(cat1(k[ia], zero_kd), cat1(zero_kd, k[ib]))) for ia, ib in pairs]
    a = [jnp.where((rowp - colp) * per_dir[chains[ia][1]]["sgn"] > 0, kq_i[0:c] * dec, 0.0)
         for (ia, ib), kq_i, dec in zip(pairs, kq, decay)]
    qk = _each(lambda x, dec: bf(x[c:2 * c] * dec), kq, decay)
    t = _unit_tri_inverse(a, rowp, colp, hi)
    x = _each(lambda x1, x2, e: bf(jnp.concatenate([x1, x2 * e], axis=1)), vb, kb, eg)
    zero_x = jnp.zeros((c, 2 * DN_D), BF16)
    txp = [_dot(bf(ti), cat0(cat1(x[ia], zero_x), cat1(zero_x, x[ib]))) for (ia, ib), ti in zip(pairs, t)]
    tx = [half for r2 in txp for half in (r2[:, 0:2 * DN_D], r2[:, 2 * DN_D:4 * DN_D])]
    st = [st_ref[sidx(*ch)] for ch in chains]
    lhs = _each(lambda txi, qi, e: bf(jnp.concatenate([txi[:, DN_D:2 * DN_D], qi.astype(F32) * e], axis=0)),
                tx, q, eg)
    r = _each(lambda l, si: _dot(l, bf(si)), lhs, st)
    v_new = _each(lambda txi, ri: bf(txi[:, 0:DN_D] - ri[0:c]), tx, r)
    qkv = [_dot(qki, cat0(cat1(v_new[ia], zero_kd), cat1(zero_kd, v_new[ib]))) for (ia, ib), qki in zip(pairs, qk)]
    qkv = [half for o2 in qkv for half in (o2[:, 0:DN_D], o2[:, DN_D:2 * DN_D])]
    out = _each(lambda ri, o_i: ri[c:2 * c] + o_i, r, qkv)
    kw = [bf(kfi * gate(bi, d, "ew_c")[:, acol(d, h):acol(d, h) + 1]) for (bi, d, h), kfi in zip(chains, kf)]
    upd = _each(lambda x, y: lax.dot_general(x, y, (((0,), (0,)), ((), ())), preferred_element_type=F32),
                kw, v_new)
    for (bi, d, h), o_i, st_i, u_i in zip(chains, out, st, upd):
        refs[d][3][bi, :, sl(h)] = o_i.astype(BF16)
        st_ref[sidx(bi, d, h)] = st_i * gate(bi, d, "et_c")[:, acol(d, h):acol(d, h) + 1] + u_i


DN_BATCH = 4


def _deltanet(qkv, gate_col, gate_row, pc, pr):
    b, lp, _ = qkv.shape
    n = lp // DN_CHUNK
    hw = DN_HEADS * DN_D
    nb = DN_BATCH if b % DN_BATCH == 0 else 1

    def fwd_chunk(s):
        return s

    lead = ROW0 // DN_CHUNK

    def bwd_chunk(s):
        return jnp.where(s < lead, s, n - 1 + lead - s)

    def spec3(cb, chunk):
        return pl.BlockSpec((nb, DN_CHUNK, hw), lambda i, s: (i, chunk(s), cb))

    in_specs = [spec3(0, fwd_chunk), spec3(1, fwd_chunk), spec3(2, fwd_chunk),
                spec3(0, bwd_chunk), spec3(1, bwd_chunk), spec3(2, bwd_chunk),
                pl.BlockSpec((nb, DN_CHUNK, GATE_COLS), lambda i, s: (i, fwd_chunk(s), 0)),
                pl.BlockSpec((nb, DN_CHUNK, GATE_COLS), lambda i, s: (i, bwd_chunk(s), 0)),
                pl.BlockSpec((nb, 1, 16, DN_CHUNK), lambda i, s: (i, fwd_chunk(s), 0, 0)),
                pl.BlockSpec((nb, 1, 16, DN_CHUNK), lambda i, s: (i, bwd_chunk(s), 0, 0)),
                pl.BlockSpec((2, GATE_COLS), lambda i, s: (0, 0)),
                pl.BlockSpec((16, 2 * DN_CHUNK), lambda i, s: (0, 0))]
    out_specs = [pl.BlockSpec((nb, DN_CHUNK, hw), lambda i, s: (i, fwd_chunk(s), 0)),
                 pl.BlockSpec((nb, DN_CHUNK, hw), lambda i, s: (i, bwd_chunk(s), 0))]
    return pl.pallas_call(
        _dn_kernel,
        grid=(b // nb, n),
        in_specs=in_specs,
        out_specs=out_specs,
        out_shape=[jax.ShapeDtypeStruct((b, lp, hw), BF16)] * 2,
        scratch_shapes=[pltpu.VMEM((nb * 2 * DN_HEADS, DN_D, DN_D), F32)],
        compiler_params=_cparams(("parallel", "arbitrary")),
        name="deltanet",
    )(qkv, qkv, qkv, qkv, qkv, qkv, gate_col, gate_col, gate_row, gate_row, pc, pr)


def _attnprep_kernel(x_ref, dqw_ref, dkw_ref, qlw_ref, wuq_ref, klw_ref, wukv_ref,
                     mqw_ref, mkw_ref, cos_ref, sin_ref,
                     dq_ref, dk_ref, mq_ref, mk_ref, mv_ref):
    ones = _ones_bf16(LANES)
    row = lax.broadcasted_iota(jnp.int32, (LANES, LANES), 0)
    col = lax.broadcasted_iota(jnp.int32, (LANES, LANES), 1)
    half_ones = ((row // DIFF_D) == (col // DIFF_D)).astype(BF16)
    cos_t, sin_t = cos_ref[...], sin_ref[...]
    tm = x_ref.shape[1]
    lane_t = lax.broadcasted_iota(jnp.int32, (tm, LANES), 1)
    row_t = pl.program_id(1) * tm + lax.broadcasted_iota(jnp.int32, (tm, LANES), 0)
    q_flag = jnp.where(lane_t == MASK_LANE, 1.0, 0.0)
    k_flag = jnp.where((lane_t == MASK_LANE) & (row_t < N_PAD), NEG_BIG, 0.0)

    def rope(t):
        return t * cos_t + pltpu.roll(t, 64, 1) * sin_t

    for src, w_ref, dst in ((0, dqw_ref, dq_ref), (512, dkw_ref, dk_ref)):
        for h in range(DIFF_HEADS):
            y = x_ref[0, :, src + h * LANES:src + (h + 1) * LANES].astype(F32)
            ms = _dot((y * y).astype(BF16), half_ones) * (1.0 / DIFF_D)
            dst[0, :, h * LANES:(h + 1) * LANES] = (y * lax.rsqrt(ms + EPS) * w_ref[...]).astype(BF16)

    cq0 = x_ref[0, :, 1024:1152].astype(F32)
    cq1 = x_ref[0, :, 1152:1280].astype(F32)
    ms = _dot((cq0 * cq0 + cq1 * cq1).astype(BF16), ones) * (1.0 / MLA_Q_RANK)
    r = lax.rsqrt(ms + EPS)
    cqn = jnp.concatenate([cq0 * r * qlw_ref[:, 0:LANES], cq1 * r * qlw_ref[:, LANES:2 * LANES]],
                          axis=1).astype(BF16)
    q = _dot(cqn, wuq_ref[...])
    inv_d = 1.0 / (MLA_NOPE + MLA_ROPE)
    for h in range(MLA_HEADS):
        q0 = q[:, 256 * h:256 * h + LANES]
        q1 = q[:, 256 * h + LANES:256 * (h + 1)]
        ms = _dot((q0 * q0 + q1 * q1).astype(BF16), ones) * inv_d
        r = lax.rsqrt(ms + EPS)
        mq_ref[0, :, 256 * h:256 * h + LANES] = (q0 * r * mqw_ref[:, 0:LANES]).astype(BF16)
        mq_ref[0, :, 256 * h + LANES:256 * (h + 1)] = (
            rope(q1 * r * mqw_ref[:, LANES:2 * LANES]) + q_flag).astype(BF16)

    ckv = x_ref[0, :, 1280:1408].astype(F32)
    ms = _dot((ckv * ckv).astype(BF16), ones) * (1.0 / MLA_KV_RANK)
    ckvn = (ckv * lax.rsqrt(ms + EPS) * klw_ref[...]).astype(BF16)
    kv = _dot(ckvn, wukv_ref[...])
    mv_ref[0] = kv[:, 512:1024].astype(BF16)
    kr = x_ref[0, :, 1408:1536].astype(F32)
    kr2 = kr * kr
    for h in range(MLA_HEADS):
        k0 = kv[:, LANES * h:LANES * (h + 1)]
        ms = _dot((k0 * k0 + kr2).astype(BF16), ones) * inv_d
        r = lax.rsqrt(ms + EPS)
        mk_ref[0, :, 256 * h:256 * h + LANES] = (k0 * r * mkw_ref[:, 0:LANES]).astype(BF16)
        mk_ref[0, :, 256 * h + LANES:256 * (h + 1)] = (
            rope(kr * r * mkw_ref[:, LANES:2 * LANES]) + k_flag).astype(BF16)


def _attnprep(main, dqw, dkw, qlw, wuq, klw, wukv, mqw, mkw, cos_t, sin_t):
    b, lp, _ = main.shape
    tm = _row_tile(lp)

    def full(a):
        return pl.BlockSpec(a.shape, lambda i, j: (0,) * a.ndim)

    def rows(width):
        return pl.BlockSpec((1, tm, width), lambda i, j: (i, j, 0))

    return pl.pallas_call(
        _attnprep_kernel,
        grid=(b, lp // tm),
        in_specs=[pl.BlockSpec((1, tm, ATTN_COLS), lambda i, j: (i, j, 0)),
                  full(dqw), full(dkw), full(qlw), full(wuq), full(klw), full(wukv),
                  full(mqw), full(mkw),
                  pl.BlockSpec((tm, LANES), lambda i, j: (j, 0)),
                  pl.BlockSpec((tm, LANES), lambda i, j: (j, 0))],
        out_specs=[rows(512), rows(512), rows(1024), rows(1024), rows(512)],
        out_shape=[jax.ShapeDtypeStruct((b, lp, 512), BF16),
                   jax.ShapeDtypeStruct((b, lp, 512), BF16),
                   jax.ShapeDtypeStruct((b, lp, 1024), BF16),
                   jax.ShapeDtypeStruct((b, lp, 1024), BF16),
                   jax.ShapeDtypeStruct((b, lp, 512), BF16)],
        compiler_params=_cparams(("parallel", "parallel")),
        name="attnprep",
    )(main, dqw, dkw, qlw, wuq, klw, wukv, mqw, mkw, cos_t, sin_t)


KEY_PARTS = 3


def _key_splits(lp, parts=None):
    nblk = lp // LANES
    nparts = min(parts or KEY_PARTS, nblk)
    edges = [(nblk * i // nparts) * LANES for i in range(nparts + 1)]
    return tuple(zip(edges[:-1], edges[1:]))


def _diff_attend_t(score_fns, vt_ref, lam, lp):
    splits = _key_splits(lp)
    tq = LANES
    nt = len(score_fns)
    add = lambda x, y: x + y

    def scores(t):
        return [score_fns[t](k0, k1) for k0, k1 in splits]

    def softmax(st):
        m = functools.reduce(jnp.maximum, [x.max(axis=0, keepdims=True) for x in st])
        pt = [jnp.exp2(x - m) for x in st]
        lt = functools.reduce(add, [x.sum(axis=0, keepdims=True) for x in pt])
        l1, l2 = lt[:, 0:tq], lt[:, tq:2 * tq]
        r = lam * l1 / l2
        return [(x[:, 0:tq] - x[:, tq:2 * tq] * r).astype(BF16) for x in pt], 1.0 / l1

    def values(grp):
        acc = None
        for i, (k0, k1) in enumerate(splits):
            w = grp[0][0][i] if len(grp) == 1 else jnp.concatenate([grp[0][0][i], grp[1][0][i]], axis=1)
            part = _dot(vt_ref[0, :, k0:k1], w)
            acc = part if acc is None else acc + part
        return [acc[:, j * tq:(j + 1) * tq] * grp[j][1] for j in range(len(grp))]

    s = [scores(0)]
    sm, outs = [], []
    for t in range(nt):
        if t + 1 < nt:
            s.append(scores(t + 1))
        sm.append(softmax(s[t]))
        if t % 2 == 1:
            outs.extend(values(sm[t - 1:t + 1]))
    if nt % 2 == 1:
        outs.extend(values(sm[nt - 1:nt]))
    return outs


def _diffattn_kernel(q_ref, k_ref, vt_ref, bandt_ref, far_ref, lam_ref, sw_ref, o_ref, kx_ref):
    qi = pl.program_id(2)
    tq = LANES
    lp = k_ref.shape[1]
    nt = kx_ref.shape[0]
    nblk = lp // LANES
    c_neg = jnp.broadcast_to(far_ref[0, 0:1, :], (LANES, LANES)).astype(BF16)

    @pl.when(qi == 0)
    def _():
        c_pos = jnp.broadcast_to(far_ref[0, 1:2, :], (lp, LANES)).astype(BF16)
        for t in range(nt):
            kx_ref[t, :, 0:LANES] = k_ref[0]
            kx_ref[t, :, LANES:2 * LANES] = c_pos

    lane = lax.broadcasted_iota(jnp.int32, (tq, LANES), 1)
    rowi = lax.broadcasted_iota(jnp.int32, (tq, LANES), 0)
    eye = jnp.where(lane == rowi, 1.0, 0.0).astype(BF16)
    score_fns = []
    for t in range(nt):
        tile = qi * nt + t

        def put(kb, val, t=t):
            @pl.when((kb >= 0) & (kb < nblk))
            def _():
                kx_ref[t, pl.ds(pl.multiple_of(kb * LANES, LANES), LANES), LANES:2 * LANES] = val

        for back in range(nt):
            put(tile - 2 - back, c_neg)
        for dd in range(3):
            put(tile - 1 + dd, bandt_ref[0, dd])

        @pl.when(tile <= nt + 1)
        def _(t=t):
            kx_ref[t, 0:N_PAD, LANES:2 * LANES] = jnp.full((N_PAD, LANES), NEG_BIG, BF16)

        q = q_ref[0, t * tq:(t + 1) * tq, :]
        zero = jnp.zeros_like(q)
        lhs = jnp.concatenate([jnp.concatenate([jnp.where(lane < DIFF_D, q, zero), eye], axis=1),
                               jnp.concatenate([jnp.where(lane >= DIFF_D, q, zero), eye], axis=1)], axis=0)
        score_fns.append(lambda k0, k1, t=t, lhs=lhs: _dot_nt(kx_ref[t, k0:k1, :], lhs))
    for t, ot in enumerate(_diff_attend_t(score_fns, vt_ref, lam_ref[0:1, 0:1], lp)):
        o = ot.T
        ms = jnp.mean(o * o, axis=-1, keepdims=True)
        o_ref[0, t * tq:(t + 1) * tq, :] = (o * lax.rsqrt(ms + EPS) * sw_ref[...]).astype(BF16)


DIFF_TILES = 3


def _diffattn(dq, dk, dvt, bandt, far, lam, sw):
    b, lp, _ = dq.shape
    nblk = lp // LANES
    nt = DIFF_TILES
    rows = nt * LANES
    return pl.pallas_call(
        _diffattn_kernel,
        grid=(b, DIFF_HEADS, pl.cdiv(nblk, nt)),
        in_specs=[pl.BlockSpec((1, rows, LANES), lambda i, h, j: (i, j, h)),
                  pl.BlockSpec((1, lp, LANES), lambda i, h, j: (i, 0, h)),
                  pl.BlockSpec((1, LANES, lp), lambda i, h, j: (i, h, 0)),
                  pl.BlockSpec((1, 3, LANES, LANES), lambda i, h, j: (h, 0, 0, 0)),
                  pl.BlockSpec((1, 2, LANES), lambda i, h, j: (h, 0, 0)),
                  pl.BlockSpec((1, LANES), lambda i, h, j: (0, 0)),
                  pl.BlockSpec((1, LANES), lambda i, h, j: (0, 0))],
        out_specs=pl.BlockSpec((1, rows, LANES), lambda i, h, j: (i, j, h)),
        out_shape=jax.ShapeDtypeStruct((b, lp, DIFF_HEADS * LANES), BF16),
        scratch_shapes=[pltpu.VMEM((nt, lp, 2 * LANES), BF16)],
        compiler_params=_cparams(("parallel", "parallel", "arbitrary")),
        name="diffattn",
    )(dq, dk, dvt, bandt, far, lam, sw)


def _mlaattn_kernel(q_ref, k_ref, v_ref, o_ref, vx_ref):
    lp = k_ref.shape[1]

    @pl.when(pl.program_id(2) == 0)
    def _():
        vx_ref[:, 0:LANES] = v_ref[0]
        vx_ref[:, LANES:2 * LANES] = jnp.ones((lp, LANES), BF16)

    q = q_ref[0]
    splits = _key_splits(lp, MLA_KEY_PARTS)
    s = [_dot_nt(q, k_ref[0, k0:k1, :]) for k0, k1 in splits]
    m = [x.max(axis=1, keepdims=True) for x in s]
    p = [jnp.exp2((x - mi).astype(BF16)) for x, mi in zip(s, m)]
    o = [_dot(pi, vx_ref[k0:k1, :]) for (k0, k1), pi in zip(splits, p)]
    m_all = functools.reduce(jnp.maximum, m)
    acc = functools.reduce(lambda a, b: a + b, [oi * jnp.exp2(mi - m_all) for oi, mi in zip(o, m)])
    o_ref[0] = (acc[:, 0:LANES] / acc[:, LANES:LANES + 1]).astype(BF16)


MLA_KEY_PARTS = 4
MLA_Q_STEPS = 4


def _mlaattn(mq, mk, mv):
    b, lp, _ = mq.shape
    tq = lp // MLA_Q_STEPS if lp % (16 * MLA_Q_STEPS) == 0 else LANES
    return pl.pallas_call(
        _mlaattn_kernel,
        grid=(b, MLA_HEADS, lp // tq),
        in_specs=[pl.BlockSpec((1, tq, 256), lambda i, h, j: (i, j, h)),
                  pl.BlockSpec((1, lp, 256), lambda i, h, j: (i, 0, h)),
                  pl.BlockSpec((1, lp, LANES), lambda i, h, j: (i, 0, h))],
        out_specs=pl.BlockSpec((1, tq, LANES), lambda i, h, j: (i, j, h)),
        out_shape=jax.ShapeDtypeStruct((b, lp, MLA_HEADS * LANES), BF16),
        scratch_shapes=[pltpu.VMEM((lp, 2 * LANES), BF16)],
        compiler_params=_cparams(("parallel", "parallel", "arbitrary")),
        name="mlaattn",
    )(mq, mk, mv)


def _outproj_tile(of_ref, ob_ref, z_ref, df_ref, ml_ref, h_ref, dnw_ref, w_ref):
    ones = _ones_bf16(LANES)
    parts = []
    for h in range(DN_HEADS):
        sl = slice(h * DN_D, (h + 1) * DN_D)
        o = of_ref[0, :, sl].astype(F32) + ob_ref[0, :, sl].astype(F32)
        ms = _dot((o * o).astype(BF16), ones) * (1.0 / DN_D)
        z = z_ref[0, :, sl].astype(F32)
        parts.append((o * lax.rsqrt(ms + EPS) * dnw_ref[...] * (z * jax.nn.sigmoid(z))).astype(BF16))
    ydn = jnp.concatenate(parts, axis=1)
    acc = h_ref[0] + _dot(ydn, w_ref[0:512, :])
    acc = acc + _dot(df_ref[0], w_ref[512:1024, :])
    acc = acc + _dot(ml_ref[0], w_ref[1024:1536, :])
    return acc


def _outproj_specs(tm, d, w):
    def rows(width, cb=0):
        return pl.BlockSpec((1, tm, width), lambda i, j: (i, j, cb))

    return [rows(512), rows(512), rows(512, (DN_COL0 + 1536) // 512), rows(512), rows(512), rows(d),
            pl.BlockSpec((1, LANES), lambda i, j: (0, 0)),
            pl.BlockSpec(w.shape, lambda i, j: (0, 0))], rows


def _outproj_ffn_kernel(of_ref, ob_ref, z_ref, df_ref, ml_ref, h_ref, dnw_ref, w_ref,
                        nw_ref, wg_ref, wu_ref, wd_ref, o_ref, a_ref):
    x = _outproj_tile(of_ref, ob_ref, z_ref, df_ref, ml_ref, h_ref, dnw_ref, w_ref)
    ms = jnp.mean(x * x, axis=-1, keepdims=True)
    u = (x * lax.rsqrt(ms + EPS) * nw_ref[...]).astype(BF16)
    f = wg_ref.shape[1]
    fc = 256
    for c0 in range(0, f, fc):
        g = _dot(u, wg_ref[:, c0:c0 + fc])
        up = _dot(u, wu_ref[:, c0:c0 + fc])
        a_ref[:, c0:c0 + fc] = (g * jax.nn.sigmoid(g) * up).astype(BF16)
    o_ref[0] = x + _dot(a_ref[...], wd_ref[...])


def _outproj_ffn(o_f, o_b, main, ydf, yml, h, dnw, w, nw, wg, wu, wd):
    b, lp, d = h.shape
    f = wg.shape[1]
    tm = _row_tile(lp)
    specs, rows = _outproj_specs(tm, d, w)

    def const(a):
        return pl.BlockSpec(a.shape, lambda i, j: (0, 0), pipeline_mode=pl.Buffered(1))

    return pl.pallas_call(
        _outproj_ffn_kernel,
        grid=(b, lp // tm),
        in_specs=specs + [pl.BlockSpec((1, d), lambda i, j: (0, 0)), const(wg), const(wu), const(wd)],
        out_specs=rows(d),
        out_shape=jax.ShapeDtypeStruct((b, lp, d), F32),
        scratch_shapes=[pltpu.VMEM((tm, f), BF16)],
        compiler_params=_cparams(("parallel", "parallel")),
        name="outproj_ffn",
    )(o_f, o_b, main, ydf, yml, h, dnw, w, nw, wg, wu, wd)


def _outproj_kernel(of_ref, ob_ref, z_ref, df_ref, ml_ref, h_ref, dnw_ref, w_ref, o_ref):
    o_ref[0] = _outproj_tile(of_ref, ob_ref, z_ref, df_ref, ml_ref, h_ref, dnw_ref, w_ref)


def _outproj(o_f, o_b, main, ydf, yml, h, dnw, w):
    b, lp, d = h.shape
    tm = _row_tile(lp)
    specs, rows = _outproj_specs(tm, d, w)
    return pl.pallas_call(
        _outproj_kernel,
        grid=(b, lp // tm),
        in_specs=specs,
        out_specs=rows(d),
        out_shape=jax.ShapeDtypeStruct((b, lp, d), F32),
        compiler_params=_cparams(("parallel", "parallel")),
        name="outproj",
    )(o_f, o_b, main, ydf, yml, h, dnw, w)


def _router_kernel(h_ref, nw_ref, rw_ref, u_ref, idx_ref, gate_ref):
    x = h_ref[...]
    ms = jnp.mean(x * x, axis=-1, keepdims=True)
    u = x * lax.rsqrt(ms + EPS) * nw_ref[...]
    u_hi = u.astype(BF16)
    u_ref[...] = u_hi
    u_lo = (u - u_hi.astype(F32)).astype(BF16)
    rw = rw_ref[...]
    rw_hi = rw.astype(BF16)
    rw_lo = (rw - rw_hi.astype(F32)).astype(BF16)
    logits = _dot(u_hi, rw_hi) + (_dot(u_lo, rw_hi) + _dot(u_hi, rw_lo))
    lane = lax.broadcasted_iota(jnp.int32, logits.shape, 1)
    logits = jnp.where(lane < N_EXPERTS, logits, -jnp.inf)
    m1 = jnp.max(logits, axis=-1, keepdims=True)
    i1 = jnp.min(jnp.where(logits == m1, lane, LANES), axis=-1, keepdims=True)
    rest = jnp.where(lane == i1, -jnp.inf, logits)
    m2 = jnp.max(rest, axis=-1, keepdims=True)
    i2 = jnp.min(jnp.where(rest == m2, lane, LANES), axis=-1, keepdims=True)
    e2 = jnp.exp(m2 - m1)
    g1 = 1.0 / (1.0 + e2)
    g2 = e2 / (1.0 + e2)
    idx_ref[...] = jnp.where(lane == 0, i1, jnp.where(lane == 1, i2, 0))
    gate_ref[...] = jnp.where(lane == 0, g1, jnp.where(lane == 1, g2, 0.0))


def _router(h2, nw, rw):
    n, d = h2.shape
    tm = 512 if n % 512 == 0 else 128
    return pl.pallas_call(
        _router_kernel,
        grid=(n // tm,),
        in_specs=[pl.BlockSpec((tm, d), lambda i: (i, 0)),
                  pl.BlockSpec((1, d), lambda i: (0, 0)),
                  pl.BlockSpec((d, LANES), lambda i: (0, 0))],
        out_specs=[pl.BlockSpec((tm, d), lambda i: (i, 0)),
                   pl.BlockSpec((tm, LANES), lambda i: (i, 0)),
                   pl.BlockSpec((tm, LANES), lambda i: (i, 0))],
        out_shape=[jax.ShapeDtypeStruct((n, d), BF16),
                   jax.ShapeDtypeStruct((n, LANES), jnp.int32),
                   jax.ShapeDtypeStruct((n, LANES), F32)],
        compiler_params=_cparams(("parallel",)),
        name="router",
    )(h2, nw, rw)


def _moe_kernel(te_ref, tv_ref, x_ref, wg_ref, wu_ref, wd_ref, o_ref, acc_ref):
    i = pl.program_id(0)
    f = pl.program_id(1)

    @pl.when(f == 0)
    def _():
        acc_ref[...] = jnp.zeros(acc_ref.shape, F32)

    @pl.when(tv_ref[i] > 0)
    def _():
        x = x_ref[...]
        g = _dot(x, wg_ref[0])
        up = _dot(x, wu_ref[0])
        a = (g * jax.nn.sigmoid(g) * up).astype(BF16)
        acc_ref[...] += _dot(a, wd_ref[0])

    @pl.when(f == pl.num_programs(1) - 1)
    def _():
        o_ref[...] = acc_ref[...].astype(o_ref.dtype)


MOE_TM = 512
MOE_F_STEPS = 2


def _moe(x_sorted, tile_expert, tile_valid, wg, wu, wd, tm):
    p, d = x_sorted.shape
    f = wg.shape[2]
    nf = MOE_F_STEPS
    fh = f // nf
    grid_spec = pltpu.PrefetchScalarGridSpec(
        num_scalar_prefetch=2,
        grid=(p // tm, nf),
        in_specs=[pl.BlockSpec((tm, d), lambda i, j, te, tv: (i, 0)),
                  pl.BlockSpec((1, d, fh), lambda i, j, te, tv: (te[i], 0, j)),
                  pl.BlockSpec((1, d, fh), lambda i, j, te, tv: (te[i], 0, j)),
                  pl.BlockSpec((1, fh, d), lambda i, j, te, tv: (te[i], j, 0))],
        out_specs=pl.BlockSpec((tm, d), lambda i, j, te, tv: (i, 0)),
        scratch_shapes=[pltpu.VMEM((tm, d), F32)],
    )
    return pl.pallas_call(
        _moe_kernel,
        grid_spec=grid_spec,
        out_shape=jax.ShapeDtypeStruct((p, d), BF16),
        compiler_params=_cparams(("arbitrary", "arbitrary")),
        name="moe",
    )(tile_expert, tile_valid, x_sorted, wg, wu, wd)


def _combine_kernel(h_ref, y0_ref, y1_ref, g_ref, o_ref):
    g = g_ref[...]
    o_ref[...] = (h_ref[...] + g[:, 0:1] * y0_ref[...].astype(F32)
                  + g[:, 1:2] * y1_ref[...].astype(F32))


def _combine(h2, y0, y1, gates):
    n, d = h2.shape
    tm = 1024 if n % 1024 == 0 else 128
    row = lambda w: pl.BlockSpec((tm, w), lambda i: (i, 0))
    return pl.pallas_call(
        _combine_kernel,
        grid=(n // tm,),
        in_specs=[row(d), row(d), row(d), row(LANES)],
        out_specs=row(d),
        out_shape=jax.ShapeDtypeStruct((n, d), F32),
        compiler_params=_cparams(("parallel",)),
        name="combine",
    )(h2, y0, y1, gates)


def _spread_rope_cols(w):
    z = jnp.zeros(w.shape[:-1] + (32,), w.dtype)
    return jnp.concatenate([w[..., 0:32], z, w[..., 32:64], z], axis=-1)


def _relayout_w_in(w):
    dnw = 4 * DN_HEADS * DN_D
    n_gate = 4 * DN_HEADS
    rest = w[:, dnw + n_gate:]
    nqk = 4 * DIFF_HEADS * DIFF_D
    nv = DIFF_HEADS * LANES
    diff_qk = rest[:, :nqk]
    lat = rest[:, nqk + nv:rest.shape[1] - MLA_ROPE]
    kr = _spread_rope_cols(rest[:, rest.shape[1] - MLA_ROPE:])
    gates = jnp.concatenate([w[:, dnw:dnw + n_gate],
                             jnp.zeros((w.shape[0], GATE_COLS - n_gate), w.dtype)], axis=1)
    out = jnp.concatenate([diff_qk, lat, kr, w[:, :dnw], gates], axis=1)
    assert out.shape[1] == MAIN_COLS + GATE_COLS
    return out.astype(BF16)


def _relayout_w_uq(w):
    per = MLA_NOPE + MLA_ROPE
    heads = [jnp.concatenate([w[:, h * per:h * per + MLA_NOPE],
                              _spread_rope_cols(w[:, h * per + MLA_NOPE:(h + 1) * per])], axis=1)
             for h in range(MLA_HEADS)]
    return jnp.concatenate(heads, axis=1).astype(BF16)


def _relayout_w_ukv(w):
    per = MLA_NOPE + LANES
    ks = [w[:, h * per:h * per + MLA_NOPE] for h in range(MLA_HEADS)]
    vs = [w[:, h * per + MLA_NOPE:(h + 1) * per] for h in range(MLA_HEADS)]
    return jnp.concatenate(ks + vs, axis=1).astype(BF16)


def _diff_v_weight_t(w):
    c0 = 4 * DN_HEADS * DN_D + 4 * DN_HEADS + 4 * DIFF_HEADS * DIFF_D
    return w[:, c0:c0 + DIFF_HEADS * LANES].T.astype(BF16)


def _qk_norm_vec(w, scale):
    return (jnp.concatenate([w[:MLA_NOPE], _spread_rope_cols(w[MLA_NOPE:])]) * scale)[None, :]


def _t5_bucket(rel):
    half = T5_BUCKETS // 2
    exact = half // 2
    n = jnp.abs(rel)
    large = exact + (jnp.log(jnp.maximum(n, exact).astype(F32) / exact)
                     / math.log(T5_MAX_DIST / exact) * (half - exact)).astype(jnp.int32)
    large = jnp.minimum(large, half - 1)
    return jnp.where(rel > 0, half, 0) + jnp.where(n < exact, n, large)


def _t5_tables(table, tq):
    i = jnp.arange(tq)[None, :]
    j = jnp.arange(LANES)[:, None]
    rels = jnp.stack([j - i + (d * LANES) for d in (-1, 0, 1)])
    table = table.astype(F32) * LOG2E
    bandt = jnp.transpose(table[_t5_bucket(rels)], (3, 0, 1, 2)).astype(BF16)
    far_rel = jnp.array([-T5_MAX_DIST, T5_MAX_DIST], jnp.int32)
    far = jnp.transpose(table[_t5_bucket(far_rel)], (1, 0))
    far = jnp.broadcast_to(far[:, :, None], far.shape + (LANES,))
    return bandt, far


def _rope_tables(lp):
    half = MLA_ROPE // 2
    inv = 1.0 / (ROPE_THETA ** (jnp.arange(half, dtype=F32) / half))
    pos = (jnp.arange(lp, dtype=jnp.int32) - N_PAD).astype(F32)
    ang = pos[:, None] * inv[None, :]
    c, s, z = jnp.cos(ang), jnp.sin(ang), jnp.zeros((lp, half), F32)
    return jnp.concatenate([c, z, c, z], axis=1), jnp.concatenate([-s, z, s, z], axis=1)


def _exclusive_cumsum(onehot):
    rows, cols = onehot.shape
    blk = 256 if rows % 256 == 0 else rows
    x = onehot.reshape(rows // blk, blk, cols)
    tri = (jnp.arange(blk)[:, None] > jnp.arange(blk)[None, :]).astype(BF16)
    within = jnp.einsum("ij,bjk->bik", tri, x.astype(BF16), preferred_element_type=F32).astype(jnp.int32)
    totals = jnp.sum(x, axis=1)
    offsets = jnp.cumsum(totals, axis=0) - totals
    return (within + offsets[:, None, :]).reshape(rows, cols)


def _moe_plan(idx, n, tm):
    e_flat = idx.reshape(-1)
    onehot = (e_flat[:, None] == jnp.arange(N_EXPERTS)[None, :]).astype(jnp.int32)
    counts = jnp.sum(onehot, axis=0)
    rank = jnp.sum(_exclusive_cumsum(onehot) * onehot, axis=1)
    tiles_per = (counts + tm - 1) // tm
    tile_end = jnp.cumsum(tiles_per)
    start = (tile_end - tiles_per) * tm
    dest = start[e_flat] + rank
    n_tiles = (2 * n) // tm + N_EXPERTS
    p = n_tiles * tm
    slot_tok = jnp.zeros((p,), jnp.int32).at[dest].set(jnp.arange(2 * n, dtype=jnp.int32) // 2)
    t = jnp.arange(n_tiles, dtype=jnp.int32)
    tile_expert = jnp.minimum(jnp.sum((t[:, None] >= tile_end[None, :]).astype(jnp.int32), axis=1),
                              N_EXPERTS - 1).astype(jnp.int32)
    tile_valid = (t < tile_end[-1]).astype(jnp.int32)
    return slot_tok, dest.reshape(n, 2), tile_expert, tile_valid


def kernel(x, meta_tokens, rel_bias_table, attn_norm_w, w_in, dn_conv_w, dn_a_log, dn_dt_bias, dn_norm_w, diff_q_norm_w, diff_k_norm_w, diff_lambda, diff_subln_w, mla_q_lat_norm_w, mla_w_uq, mla_kv_lat_norm_w, mla_w_ukv, mla_q_norm_w, mla_k_norm_w, w_out, ffn_norm_w, ffn_w_gate, ffn_w_up, ffn_w_down, router_w, moe_w_gate, moe_w_up, moe_w_down):
    bsz, seq, d = x.shape
    depth = w_in.shape[0]
    lp = ROW0 + seq
    n = bsz * lp
    assert lp % LANES == 0 and meta_tokens.shape[0] == N_META

    meta = jnp.broadcast_to(meta_tokens[None].astype(x.dtype), (bsz, N_META, d))
    h = jnp.concatenate([jnp.zeros((bsz, N_PAD, d), x.dtype), meta, x], axis=1)

    cos_t, sin_t = _rope_tables(lp)
    bandt, far = _t5_tables(rel_bias_table, LANES)
    n_chunks = lp // DN_CHUNK

    for l in range(depth):
        main, gate, dvt = _inproj(h, attn_norm_w[l][None, :], _relayout_w_in(w_in[l]),
                                  _diff_v_weight_t(w_in[l]))

        qkv = _dnprep(main, dn_conv_w[l])
        gate_row = jnp.transpose(gate[:, :, 0:16].reshape(bsz, n_chunks, DN_CHUNK, 16), (0, 1, 3, 2))
        neg_a = -jnp.exp(dn_a_log[l].astype(F32)).reshape(-1)
        dtb = dn_dt_bias[l].astype(F32).reshape(-1)
        z8 = jnp.zeros((8,), F32)
        pc = jnp.stack([jnp.concatenate([z8, neg_a, jnp.zeros((GATE_COLS - 16,), F32)]),
                        jnp.concatenate([z8, dtb, jnp.zeros((GATE_COLS - 16,), F32)])])
        pr = jnp.concatenate([jnp.broadcast_to(jnp.concatenate([z8, neg_a])[:, None], (16, DN_CHUNK)),
                              jnp.broadcast_to(jnp.concatenate([z8, dtb])[:, None], (16, DN_CHUNK))], axis=1)
        o_f, o_b = _deltanet(qkv, gate, gate_row, pc, pr)

        dscale = DIFF_D ** -0.5 * LOG2E
        mscale = (MLA_NOPE + MLA_ROPE) ** -0.5 * LOG2E
        dq, dk, mq, mk, mv = _attnprep(
            main,
            (jnp.tile(diff_q_norm_w[l], 2) * dscale)[None, :],
            jnp.tile(diff_k_norm_w[l], 2)[None, :],
            mla_q_lat_norm_w[l][None, :], _relayout_w_uq(mla_w_uq[l]),
            mla_kv_lat_norm_w[l][None, :], _relayout_w_ukv(mla_w_ukv[l]),
            _qk_norm_vec(mla_q_norm_w[l], mscale), _qk_norm_vec(mla_k_norm_w[l], 1.0),
            cos_t, sin_t)
        lam_init = 0.8 - 0.6 * math.exp(-0.3 * l)
        lpar = diff_lambda[l].astype(F32)
        lam = jnp.exp(jnp.sum(lpar[0] * lpar[1])) - jnp.exp(jnp.sum(lpar[2] * lpar[3])) + lam_init
        ydf = _diffattn(dq, dk, dvt, bandt, far, jnp.full((1, LANES), lam, F32),
                        (diff_subln_w[l] * (1.0 - lam_init))[None, :])
        yml = _mlaattn(mq, mk, mv)

        mix_in = (o_f, o_b, main, ydf, yml, h, dn_norm_w[l][None, :], w_out[l].astype(BF16))
        if l % 2 == 0:
            i = l // 2
            h2 = _outproj_ffn(*mix_in, ffn_norm_w[l][None, :], ffn_w_gate[i].astype(BF16),
                              ffn_w_up[i].astype(BF16), ffn_w_down[i].astype(BF16)).reshape(n, d)
        else:
            i = l // 2
            rw = jnp.concatenate([router_w[i], jnp.zeros((d, LANES - N_EXPERTS), F32)], axis=1)
            h2 = _outproj(*mix_in).reshape(n, d)
            u, idx, gates = _router(h2, ffn_norm_w[l][None, :], rw)
            tm = MOE_TM if (2 * n) % MOE_TM == 0 else 128
            slot_tok, tok_slots, tile_expert, tile_valid = _moe_plan(idx[:, 0:2], n, tm)
            x_sorted = jnp.take(u, slot_tok, axis=0)
            y_sorted = _moe(x_sorted, tile_expert, tile_valid, moe_w_gate[i].astype(BF16),
                            moe_w_up[i].astype(BF16), moe_w_down[i].astype(BF16), tm)
            h2 = _combine(h2, jnp.take(y_sorted, tok_slots[:, 0], axis=0),
                          jnp.take(y_sorted, tok_slots[:, 1], axis=0), gates)
        h = h2.reshape(bsz, lp, d)
    return h[:, ROW0:]
```
